```python
import math
import jax, jax.numpy as jnp
from jax import lax
import numpy as np

D_MODEL = 1024
BATCH = 8
SEQ = 2048
DEPTH = 2
DEC_BATCH = 128
DEC_SEQ = 8
PAST_LEN = 2048
PAGE_SIZE = 128

F32 = jnp.float32
N_BRANCH = 4
BRANCH_WIDTH = D_MODEL // N_BRANCH
HEAD_DIM = 64
GLA_HEADS = BRANCH_WIDTH // HEAD_DIM
GLA_WIDTH = GLA_HEADS * HEAD_DIM
GLA_GATE_RANK = 16
GLA_TAU = 16.0
GLA_CHUNK = 64
MOBA_HEADS = BRANCH_WIDTH // HEAD_DIM
MOBA_WIDTH = MOBA_HEADS * HEAD_DIM
MOBA_BLOCK = 256
MOBA_TOPK = 3
MOBA_QBLOCK = 32
FOX_HEADS = BRANCH_WIDTH // HEAD_DIM
FOX_WIDTH = FOX_HEADS * HEAD_DIM
FOX_QBLOCK = 128
S5_GROUP = 16
S5_WIDTH = BRANCH_WIDTH
S5_GROUPS = S5_WIDTH // S5_GROUP
S5_STATE = 64
PEER_HEADS = 8
PEER_DKEY = 256
PEER_NKEYS = 128
PEER_TOPK = 16
PEER_EXPERTS = PEER_NKEYS * PEER_NKEYS
PEER_BLOCK = 256
IN_SIZES = (GLA_WIDTH,) * 4 + (GLA_GATE_RANK,) + (MOBA_WIDTH,) * 3 + (FOX_WIDTH,) * 3 + (FOX_HEADS, S5_WIDTH, N_BRANCH * D_MODEL)
IN_WIDTH = sum(IN_SIZES)
DN_ALPHA = (2.0 * DEPTH) ** 0.25
DN_BETA = (8.0 * DEPTH) ** -0.25
LN_EPS = 1e-5

kernel_name = 'hybrid_gla_moba_fox_s5_peer_step'


def _ln(x, g=None, b=None):
    xf = x.astype(F32)
    mu = jnp.mean(xf, -1, keepdims=True)
    var = jnp.mean(jnp.square(xf - mu), -1, keepdims=True)
    y = (xf - mu) * lax.rsqrt(var + LN_EPS)
    if g is not None:
        y = y * g.astype(F32) + b.astype(F32)
    return y


def _gla(q, k, v, log_a, s0):
    B, T, H, DK = q.shape
    DV = v.shape[-1]
    C = math.gcd(T, GLA_CHUNK)
    N = T // C

    def chunks(a):
        return a.astype(F32).reshape(B, N, C, H, a.shape[-1]).transpose(1, 0, 3, 2, 4)

    causal = jnp.tril(jnp.ones((C, C), bool))

    def step(S, inp):
        qc, kc, vc, gc = inp
        b = jnp.cumsum(gc, axis=2)
        o_inter = jnp.einsum('bhtd,bhde->bhte', qc * jnp.exp(b), S)
        diff = jnp.where(causal[None, None, :, :, None], b[:, :, :, None, :] - b[:, :, None, :, :], -jnp.inf)
        att = jnp.einsum('bhtd,bhsd,bhtsd->bhts', qc, kc, jnp.exp(diff))
        o = o_inter + jnp.einsum('bhts,bhse->bhte', att, vc)
        b_end = b[:, :, -1, :]
        S = jnp.exp(b_end)[..., None] * S + jnp.einsum('bhsd,bhse->bhde', kc * jnp.exp(b_end[:, :, None, :] - b), vc)
        return S, o

    S, o = lax.scan(step, s0.astype(F32), (chunks(q), chunks(k), chunks(v), chunks(log_a)))
    return o.transpose(1, 0, 3, 2, 4).reshape(B, T, H, DV), S


def _moba(q, k, v, pos0):
    B, T, H, Dh = q.shape
    L = k.shape[1]
    nb = -(-L // MOBA_BLOCK)
    padl = nb * MOBA_BLOCK - L
    kp = jnp.pad(k.astype(F32), ((0, 0), (0, padl), (0, 0), (0, 0)))
    vp = jnp.pad(v.astype(F32), ((0, 0), (0, padl), (0, 0), (0, 0)))
    kb = kp.reshape(B, nb, MOBA_BLOCK, H, Dh).transpose(0, 3, 1, 2, 4)
    vb = vp.reshape(B, nb, MOBA_BLOCK, H, Dh).transpose(0, 3, 1, 2, 4)
    kmean = jnp.mean(kb, axis=3)
    qf = q.astype(F32).transpose(0, 2, 1, 3)
    qpos = pos0 + jnp.arange(T)
    own = qpos // MOBA_BLOCK
    bscore = jnp.einsum('bhtd,bhnd->bhtn', qf, kmean)
    fully_past = jnp.arange(nb)[None, :] < own[:, None]
    bscore = jnp.where(fully_past, bscore, -jnp.inf)
    ksel = min(MOBA_TOPK, nb)
    _, sel = lax.top_k(bscore, ksel)
    sel_ok = sel < own[:, None]
    blocks = jnp.concatenate([sel, jnp.broadcast_to(own[:, None], (B, H, T, 1))], -1)
    ok = jnp.concatenate([sel_ok, jnp.ones((B, H, T, 1), bool)], -1)
    S = ksel + 1
    qb = math.gcd(T, MOBA_QBLOCK)
    nq = T // qb
    slopes = jnp.asarray([2.0 ** (-8.0 * (i + 1) / MOBA_HEADS) for i in range(MOBA_HEADS)], F32)
    scale = Dh ** -0.5

    def per_seq(args):
        q_s, kb_s, vb_s, blk_s, ok_s = args

        def per_block(bargs):
            q_b, blk_b, ok_b, pos_b = bargs
            kg = jax.vmap(lambda kk, bb: kk[bb])(kb_s, blk_b)
            vg = jax.vmap(lambda vv, bb: vv[bb])(vb_s, blk_b)
            kpos = blk_b[..., None] * MOBA_BLOCK + jnp.arange(MOBA_BLOCK)
            dist = (pos_b[None, :, None, None] - kpos).astype(F32)
            mask = ok_b[..., None] & (dist >= 0)
            logits = jnp.einsum('hqd,hqskd->hqsk', q_b, kg) * scale - slopes[:, None, None, None] * dist
            logits = jnp.where(mask, logits, -jnp.inf).reshape(H, qb, S * MOBA_BLOCK)
            w = jax.nn.softmax(logits, axis=-1).reshape(kpos.shape)
            return jnp.einsum('hqsk,hqskd->hqd', w, vg)

        xs = (q_s.reshape(H, nq, qb, Dh).transpose(1, 0, 2, 3),
              blk_s.reshape(H, nq, qb, S).transpose(1, 0, 2, 3),
              ok_s.reshape(H, nq, qb, S).transpose(1, 0, 2, 3),
              qpos.reshape(nq, qb))
        o = lax.map(per_block, xs)
        return o.transpose(1, 0, 2, 3).reshape(H, T, Dh)

    out = lax.map(per_seq, (qf, kb, vb, blocks, ok))
    return out.transpose(0, 2, 1, 3)


def _fox(q, k, v, logf, pos0):
    B, T, H, Dh = q.shape
    L = k.shape[1]
    cum = jnp.cumsum(logf.astype(F32), axis=1)
    cq = cum[:, pos0:]
    cumT = cum.transpose(0, 2, 1)
    qf = q.astype(F32) * Dh ** -0.5
    kf = k.astype(F32)
    vf = v.astype(F32)
    qb = math.gcd(T, FOX_QBLOCK)
    nq = T // qb
    kpos = jnp.arange(L)

    def per_block(args):
        q_b, c_b, pos_b = args
        logits = jnp.einsum('bqhd,bkhd->bhqk', q_b, kf) + c_b.transpose(0, 2, 1)[..., None] - cumT[:, :, None, :]
        logits = jnp.where(kpos[None, None, None, :] <= pos_b[None, None, :, None], logits, -jnp.inf)
        w = jax.nn.softmax(logits, axis=-1)
        return jnp.einsum('bhqk,bkhd->bqhd', w, vf)

    xs = (qf.reshape(B, nq, qb, H, Dh).transpose(1, 0, 2, 3, 4),
          cq.reshape(B, nq, qb, H).transpose(1, 0, 2, 3),
          (pos0 + jnp.arange(T)).reshape(nq, qb))
    o = lax.map(per_block, xs)
    return o.transpose(1, 0, 2, 3, 4).reshape(B, T, H, Dh)


def _cplx_affine_combine(e1, e2):
    a1r, a1i, b1r, b1i = e1
    a2r, a2i, b2r, b2i = e2
    return (a2r * a1r - a2i * a1i, a2r * a1i + a2i * a1r,
            a2r * b1r - a2i * b1i + b2r, a2r * b1i + a2i * b1r + b2i)


def _s5(u, h0_re, h0_im, p):
    B, T, _ = u.shape
    lam_re = p['s5_lambda_re'].astype(F32)
    lam_im = p['s5_lambda_im'].astype(F32)
    dt = jnp.exp(p['s5_log_dt'].astype(F32))[:, None]
    mag = jnp.exp(lam_re * dt)
    ab_re = mag * jnp.cos(lam_im * dt)
    ab_im = mag * jnp.sin(lam_im * dt)
    den = lam_re * lam_re + lam_im * lam_im
    z_re = ((ab_re - 1.0) * lam_re + ab_im * lam_im) / den
    z_im = (ab_im * lam_re - (ab_re - 1.0) * lam_im) / den
    b_re = p['s5_b_re'].astype(F32)
    b_im = p['s5_b_im'].astype(F32)
    bb_re = z_re[..., None] * b_re - z_im[..., None] * b_im
    bb_im = z_re[..., None] * b_im + z_im[..., None] * b_re
    ug = u.astype(F32).reshape(B, T, S5_GROUPS, S5_GROUP)
    bu_re = jnp.einsum('gpi,btgi->btgp', bb_re, ug)
    bu_im = jnp.einsum('gpi,btgi->btgp', bb_im, ug)
    a_re = jnp.broadcast_to(ab_re, bu_re.shape)
    a_im = jnp.broadcast_to(ab_im, bu_im.shape)
    A_re, A_im, H_re, H_im = lax.associative_scan(_cplx_affine_combine, (a_re, a_im, bu_re, bu_im), axis=1)
    h0r = h0_re.astype(F32)[:, None]
    h0i = h0_im.astype(F32)[:, None]
    h_re = H_re + A_re * h0r - A_im * h0i
    h_im = H_im + A_re * h0i + A_im * h0r
    y = (jnp.einsum('gip,btgp->btgi', p['s5_c_re'].astype(F32), h_re)
         - jnp.einsum('gip,btgp->btgi', p['s5_c_im'].astype(F32), h_im)
         + p['s5_d'].astype(F32) * ug)
    return y.reshape(B, T, S5_WIDTH), h_re[:, -1], h_im[:, -1]


def _peer(h, w_q, sub_keys, u_tab, v_tab):
    B, T, D = h.shape
    n = B * T
    nblk = -(-n // PEER_BLOCK)
    hp = jnp.pad(h.reshape(n, D), ((0, nblk * PEER_BLOCK - n), (0, 0))).reshape(nblk, PEER_BLOCK, D)

    def block(hb):
        q = (hb @ w_q).astype(F32).reshape(PEER_BLOCK, PEER_HEADS, 2, PEER_DKEY // 2)
        s = jnp.einsum('nhpc,pec->nhpe', q, sub_keys.astype(F32))
        sv, si = lax.top_k(s, PEER_TOPK)
        cand = (sv[:, :, 0, :, None] + sv[:, :, 1, None, :]).reshape(PEER_BLOCK, PEER_HEADS, -1)
        cid = (si[:, :, 0, :, None] * PEER_NKEYS + si[:, :, 1, None, :]).reshape(PEER_BLOCK, PEER_HEADS, -1)
        tv, ti = lax.top_k(cand, PEER_TOPK)
        eid = jnp.take_along_axis(cid, ti, axis=-1)
        g = jax.nn.softmax(tv, axis=-1)
        act = jax.nn.gelu(jnp.einsum('nhkd,nd->nhk', u_tab[eid].astype(F32), hb.astype(F32)), approximate=False)
        return jnp.einsum('nhk,nhkd->nd', g * act, v_tab[eid].astype(F32))

    out = lax.map(block, hp)
    return out.reshape(-1, D)[:n].reshape(B, T, D)


def _token_mixing(h, p, past):
    B, T, _ = h.shape
    pos0 = past['moba_k'].shape[1]
    (gq, gk, gv, gr, ga, mq, mk, mv, fq, fk, fv, fg, su, bg) = jnp.split(
        h @ p['w_in'], np.cumsum(IN_SIZES)[:-1].tolist(), axis=-1)

    def heads(a):
        return a.reshape(B, T, -1, HEAD_DIM)

    log_a = jax.nn.log_sigmoid((ga @ p['gla_w_gate2'] + p['gla_b_gate']).astype(F32)) / GLA_TAU
    o_a, gla_state = _gla(heads(gq).astype(F32) * HEAD_DIM ** -0.5, heads(gk), heads(gv), heads(log_a), past['gla'])
    o_a = o_a * lax.rsqrt(jnp.mean(jnp.square(o_a), -1, keepdims=True) + LN_EPS)
    o_a = o_a.reshape(B, T, GLA_WIDTH) * p['gla_norm'].astype(F32) * jax.nn.silu(gr.astype(F32))
    mk_h, mv_h = heads(mk), heads(mv)
    o_b = _moba(heads(mq), jnp.concatenate([past['moba_k'], mk_h], 1),
                jnp.concatenate([past['moba_v'], mv_h], 1), pos0).reshape(B, T, MOBA_WIDTH)
    fk_h, fv_h = heads(fk), heads(fv)
    logf = jax.nn.log_sigmoid((fg + p['fox_b_f']).astype(F32))
    o_c = _fox(heads(fq), jnp.concatenate([past['fox_k'], fk_h], 1), jnp.concatenate([past['fox_v'], fv_h], 1),
               jnp.concatenate([past['fox_logf'].astype(F32), logf], 1), pos0).reshape(B, T, FOX_WIDTH)
    y, s_re, s_im = _s5(su, past['s5_re'], past['s5_im'], p)
    z = jax.nn.gelu(y, approximate=False)
    o_d = z * jax.nn.sigmoid(z @ p['s5_w_glu'].astype(F32) + p['s5_b_glu'].astype(F32))
    branches = jnp.stack([o_a, o_b, o_c, o_d], axis=2).astype(h.dtype)
    up = jnp.einsum('btnw,nwd->btnd', branches, p['w_branch'])
    gate = jax.nn.sigmoid(bg.reshape(B, T, N_BRANCH, D_MODEL).astype(F32))
    merged = jnp.sum(gate * up.astype(F32), axis=2).astype(h.dtype)
    out = merged @ p['w_out']
    new = {'moba_k': mk_h, 'moba_v': mv_h, 'fox_k': fk_h, 'fox_v': fv_h, 'fox_logf': logf,
           'gla': gla_state, 's5_re': s_re, 's5_im': s_im}
    return out, new


def _trunk_layer(x, c, p, past):
    mod = jax.nn.silu(c.astype(F32)) @ p['w_mod'].astype(F32) + p['b_mod'].astype(F32)
    sh1, sc1, g1, sh2, sc2, g2 = [m[:, None, :] for m in jnp.split(mod, 6, axis=-1)]
    h1 = (_ln(x) * (1.0 + sc1) + sh1).astype(x.dtype)
    mix, new_state = _token_mixing(h1, p, past)
    x1 = _ln(DN_ALPHA * x.astype(F32) + (1.0 + g1) * mix.astype(F32), p['ln1_g'], p['ln1_b']).astype(x.dtype)
    h2 = (_ln(x1) * (1.0 + sc2) + sh2).astype(x.dtype)
    ff = _peer(h2, p['peer_w_q'], p['peer_sub_keys'], p['peer_u'], p['peer_v'])
    x2 = _ln(DN_ALPHA * x1.astype(F32) + (1.0 + g2) * ff, p['ln2_g'], p['ln2_b']).astype(x.dtype)
    return x2, new_state


def setup_inputs(seed: int = 0) -> dict:
    key = jax.random.key(seed)
    ks = iter(jax.random.split(key, 64))

    def nrm(shape, s=1.0):
        return s * jax.random.normal(next(ks), shape, F32)

    n_pages = PAST_LEN // PAGE_SIZE
    n_used = DEC_BATCH * n_pages
    n_pool = n_used + max(1, n_used // 4)
    D = D_MODEL
    x_prompt = nrm((BATCH, SEQ, D))
    x_sample = nrm((DEC_BATCH, DEC_SEQ, D))
    c_prompt = nrm((BATCH, D))
    c_sample = nrm((DEC_BATCH, D))
    page_table = jax.random.permutation(next(ks), n_pool)[:n_used].reshape(DEC_BATCH, n_pages).astype(jnp.int32)
    cache_moba_k = nrm((DEPTH, n_pool, PAGE_SIZE, MOBA_HEADS, HEAD_DIM))
    cache_moba_v = nrm((DEPTH, n_pool, PAGE_SIZE, MOBA_HEADS, HEAD_DIM))
    cache_fox_k = nrm((DEPTH, n_pool, PAGE_SIZE, FOX_HEADS, HEAD_DIM))
    cache_fox_v = nrm((DEPTH, n_pool, PAGE_SIZE, FOX_HEADS, HEAD_DIM))
    cache_fox_logf = jax.nn.log_sigmoid(2.0 + nrm((DEPTH, n_pool, PAGE_SIZE, FOX_HEADS)))
    state_gla = nrm((DEPTH, DEC_BATCH, GLA_HEADS, HEAD_DIM, HEAD_DIM))
    state_s5_re = nrm((DEPTH, DEC_BATCH, S5_GROUPS, S5_STATE), 0.1)
    state_s5_im = nrm((DEPTH, DEC_BATCH, S5_GROUPS, S5_STATE), 0.1)
    w_mod = nrm((DEPTH, D, 6 * D), 0.1 * D ** -0.5)
    b_mod = nrm((DEPTH, 6 * D), 0.01)
    w_in = nrm((DEPTH, D, IN_WIDTH), D ** -0.5)
    gla_w_gate2 = nrm((DEPTH, GLA_GATE_RANK, GLA_WIDTH), GLA_GATE_RANK ** -0.5)
    gla_b_gate = nrm((DEPTH, GLA_WIDTH), 0.01)
    gla_norm = 1.0 + nrm((DEPTH, GLA_WIDTH), 0.01)
    fox_b_f = 2.0 + nrm((DEPTH, FOX_HEADS), 0.1)
    n_idx = jnp.arange(S5_STATE, dtype=F32)
    s5_lambda_re = -0.5 + nrm((DEPTH, S5_GROUPS, S5_STATE), 0.01)
    s5_lambda_im = jnp.pi * n_idx + nrm((DEPTH, S5_GROUPS, S5_STATE), 0.01)
    s5_log_dt = jax.random.uniform(next(ks), (DEPTH, S5_GROUPS), F32, math.log(0.001), math.log(0.1))
    s5_b_re = nrm((DEPTH, S5_GROUPS, S5_STATE, S5_GROUP), (2.0 * S5_GROUP) ** -0.5)
    s5_b_im = nrm((DEPTH, S5_GROUPS, S5_STATE, S5_GROUP), (2.0 * S5_GROUP) ** -0.5)
    s5_c_re = nrm((DEPTH, S5_GROUPS, S5_GROUP, S5_STATE), 0.5)
    s5_c_im = nrm((DEPTH, S5_GROUPS, S5_GROUP, S5_STATE), 0.5)
    s5_d = nrm((DEPTH, S5_GROUPS, S5_GROUP), 0.5)
    s5_w_glu = nrm((DEPTH, S5_WIDTH, S5_WIDTH), S5_WIDTH ** -0.5)
    s5_b_glu = nrm((DEPTH, S5_WIDTH), 0.01)
    w_branch = nrm((DEPTH, N_BRANCH, BRANCH_WIDTH, D), DN_BETA * BRANCH_WIDTH ** -0.5)
    w_out = nrm((DEPTH, D, D), DN_BETA * D ** -0.5)
    ln1_g = 1.0 + nrm((DEPTH, D), 0.01)
    ln1_b = nrm((DEPTH, D), 0.01)
    peer_w_q = nrm((DEPTH, D, PEER_HEADS * PEER_DKEY), D ** -0.5)
    peer_sub_keys = nrm((DEPTH, 2, PEER_NKEYS, PEER_DKEY // 2), (PEER_DKEY // 2) ** -0.5)
    peer_u = nrm((DEPTH, PEER_EXPERTS, D), D ** -0.5)
    peer_v = nrm((DEPTH, PEER_EXPERTS, D), DN_BETA * PEER_HEADS ** -0.5)
    ln2_g = 1.0 + nrm((DEPTH, D), 0.01)
    ln2_b = nrm((DEPTH, D), 0.01)
    return {'x_prompt': x_prompt, 'x_sample': x_sample, 'c_prompt': c_prompt, 'c_sample': c_sample,
            'page_table': page_table, 'cache_moba_k': cache_moba_k, 'cache_moba_v': cache_moba_v,
            'cache_fox_k': cache_fox_k, 'cache_fox_v': cache_fox_v, 'cache_fox_logf': cache_fox_logf,
            'state_gla': state_gla, 'state_s5_re': state_s5_re, 'state_s5_im': state_s5_im,
            'w_mod': w_mod, 'b_mod': b_mod, 'w_in': w_in, 'gla_w_gate2': gla_w_gate2, 'gla_b_gate': gla_b_gate,
            'gla_norm': gla_norm, 'fox_b_f': fox_b_f, 's5_lambda_re': s5_lambda_re, 's5_lambda_im': s5_lambda_im,
            's5_log_dt': s5_log_dt, 's5_b_re': s5_b_re, 's5_b_im': s5_b_im, 's5_c_re': s5_c_re, 's5_c_im': s5_c_im,
            's5_d': s5_d, 's5_w_glu': s5_w_glu, 's5_b_glu': s5_b_glu, 'w_branch': w_branch, 'w_out': w_out,
            'ln1_g': ln1_g, 'ln1_b': ln1_b, 'peer_w_q': peer_w_q, 'peer_sub_keys': peer_sub_keys,
            'peer_u': peer_u, 'peer_v': peer_v, 'ln2_g': ln2_g, 'ln2_b': ln2_b}


def reference(x_prompt, x_sample, c_prompt, c_sample, page_table, cache_moba_k, cache_moba_v, cache_fox_k,
              cache_fox_v, cache_fox_logf, state_gla, state_s5_re, state_s5_im, w_mod, b_mod, w_in, gla_w_gate2,
              gla_b_gate, gla_norm, fox_b_f, s5_lambda_re, s5_lambda_im, s5_log_dt, s5_b_re, s5_b_im, s5_c_re,
              s5_c_im, s5_d, s5_w_glu, s5_b_glu, w_branch, w_out, ln1_g, ln1_b, peer_w_q, peer_sub_keys, peer_u,
              peer_v, ln2_g, ln2_b):
    dtype = x_prompt.dtype
    bp = x_prompt.shape[0]
    past_len = page_table.shape[1] * cache_moba_k.shape[2]

    def paged(pool, l):
        rows = pool[l][page_table]
        return rows.reshape((rows.shape[0], past_len) + rows.shape[3:])

    y_p, y_s = x_prompt, x_sample
    st_p, st_s = [], []
    for l in range(DEPTH):
        p = {'w_mod': w_mod[l], 'b_mod': b_mod[l], 'w_in': w_in[l], 'gla_w_gate2': gla_w_gate2[l],
             'gla_b_gate': gla_b_gate[l], 'gla_norm': gla_norm[l], 'fox_b_f': fox_b_f[l],
             's5_lambda_re': s5_lambda_re[l], 's5_lambda_im': s5_lambda_im[l], 's5_log_dt': s5_log_dt[l],
             's5_b_re': s5_b_re[l], 's5_b_im': s5_b_im[l], 's5_c_re': s5_c_re[l], 's5_c_im': s5_c_im[l],
             's5_d': s5_d[l], 's5_w_glu': s5_w_glu[l], 's5_b_glu': s5_b_glu[l], 'w_branch': w_branch[l],
             'w_out': w_out[l], 'ln1_g': ln1_g[l], 'ln1_b': ln1_b[l], 'peer_w_q': peer_w_q[l],
             'peer_sub_keys': peer_sub_keys[l], 'peer_u': peer_u[l], 'peer_v': peer_v[l],
             'ln2_g': ln2_g[l], 'ln2_b': ln2_b[l]}
        past_p = {'moba_k': jnp.zeros((bp, 0, MOBA_HEADS, HEAD_DIM), dtype),
                  'moba_v': jnp.zeros((bp, 0, MOBA_HEADS, HEAD_DIM), dtype),
                  'fox_k': jnp.zeros((bp, 0, FOX_HEADS, HEAD_DIM), dtype),
                  'fox_v': jnp.zeros((bp, 0, FOX_HEADS, HEAD_DIM), dtype),
                  'fox_logf': jnp.zeros((bp, 0, FOX_HEADS), F32),
                  'gla': jnp.zeros((bp, GLA_HEADS, HEAD_DIM, HEAD_DIM), F32),
                  's5_re': jnp.zeros((bp, S5_GROUPS, S5_STATE), F32),
                  's5_im': jnp.zeros((bp, S5_GROUPS, S5_STATE), F32)}
        past_s = {'moba_k': paged(cache_moba_k, l), 'moba_v': paged(cache_moba_v, l),
                  'fox_k': paged(cache_fox_k, l), 'fox_v': paged(cache_fox_v, l),
                  'fox_logf': paged(cache_fox_logf, l), 'gla': state_gla[l],
                  's5_re': state_s5_re[l], 's5_im': state_s5_im[l]}
        y_p, new_p = _trunk_layer(y_p, c_prompt, p, past_p)
        y_s, new_s = _trunk_layer(y_s, c_sample, p, past_s)
        st_p.append(new_p)
        st_s.append(new_s)

    def stk(states, name):
        return jnp.stack([s[name] for s in states])

    moba_k_p, moba_v_p = stk(st_p, 'moba_k'), stk(st_p, 'moba_v')
    fox_k_p, fox_v_p, fox_logf_p = stk(st_p, 'fox_k'), stk(st_p, 'fox_v'), stk(st_p, 'fox_logf')
    gla_p, s5_re_p, s5_im_p = stk(st_p, 'gla'), stk(st_p, 's5_re'), stk(st_p, 's5_im')
    moba_k_s, moba_v_s = stk(st_s, 'moba_k'), stk(st_s, 'moba_v')
    fox_k_s, fox_v_s, fox_logf_s = stk(st_s, 'fox_k'), stk(st_s, 'fox_v'), stk(st_s, 'fox_logf')
    gla_s, s5_re_s, s5_im_s = stk(st_s, 'gla'), stk(st_s, 's5_re'), stk(st_s, 's5_im')
    return (y_p, y_s, moba_k_p, moba_v_p, fox_k_p, fox_v_p, fox_logf_p, gla_p, s5_re_p, s5_im_p,
            moba_k_s, moba_v_s, fox_k_s, fox_v_s, fox_logf_s, gla_s, s5_re_s, s5_im_s)
```

```python
import functools
import math

import numpy as np
import jax
import jax.numpy as jnp
from jax import lax
from jax.experimental import pallas as pl
from jax.experimental.pallas import tpu as pltpu

F32 = jnp.float32
MXU_DTYPE = jnp.bfloat16

HEAD_DIM = 64
N_HEADS = 4
BRANCH_WIDTH = N_HEADS * HEAD_DIM
N_BRANCH = 4
GLA_GATE_RANK = 16
GLA_TAU = 16.0
GLA_CHUNK = 64
GLA_SUBCHUNK = 16
MOBA_BLOCK = 256
MOBA_TOPK = 3
S5_GROUP = 16
S5_GROUPS = BRANCH_WIDTH // S5_GROUP
S5_STATE = 64
S5_WIDTH_STATE = S5_GROUPS * S5_STATE
PEER_HEADS = 8
PEER_DKEY = 256
PEER_NKEYS = 128
PEER_TOPK = 16
LN_EPS = 1e-5
NEG_BIG = -1e30

V7X_VMEM_LIMIT = 56 * 1024 * 1024
LANES = 128


def _cparams(*sem):
    return pltpu.CompilerParams(dimension_semantics=tuple(sem), vmem_limit_bytes=V7X_VMEM_LIMIT)


def _dot(a, b):
    return jnp.dot(a.astype(MXU_DTYPE), b.astype(MXU_DTYPE), preferred_element_type=F32)


def _dot_nt(a, b):
    return lax.dot_general(a.astype(MXU_DTYPE), b.astype(MXU_DTYPE), (((1,), (1,)), ((), ())),
                           preferred_element_type=F32)


def _dot_tn(a, b):
    return lax.dot_general(a.astype(MXU_DTYPE), b.astype(MXU_DTYPE), (((0,), (0,)), ((), ())),
                           preferred_element_type=F32)


def _split3(x):
    hi = x.astype(jnp.bfloat16)
    r1 = x - hi.astype(F32)
    mid = r1.astype(jnp.bfloat16)
    lo = (r1 - mid.astype(F32)).astype(jnp.bfloat16)
    return hi, mid, lo


def _dot_exact01(x, m01):
    hi, mid, lo = _split3(x)
    m = m01.astype(jnp.bfloat16)
    d = lambda a: jnp.dot(a, m, preferred_element_type=F32)
    return d(hi) + d(mid) + d(lo)


def _dot01_exact(m01, x):
    hi, mid, lo = _split3(x)
    m = m01.astype(jnp.bfloat16)
    d = lambda a: jnp.dot(m, a, preferred_element_type=F32)
    return d(hi) + d(mid) + d(lo)


def _ln(x):
    mu = jnp.mean(x, axis=-1, keepdims=True)
    xc = x - mu
    var = jnp.mean(xc * xc, axis=-1, keepdims=True)
    return xc * lax.rsqrt(var + LN_EPS)


def _sigmoid(x):
    return 1.0 / (1.0 + jnp.exp(-x))


def _log_sigmoid(x):
    return jnp.minimum(x, 0.0) - jnp.log1p(jnp.exp(-jnp.abs(x)))


def _gelu(x):
    return 0.5 * x * (1.0 + lax.erf(x * (1.0 / math.sqrt(2.0))))


def _iota(shape, axis):
    return lax.broadcasted_iota(jnp.int32, shape, axis)


def _mod_kernel(c_ref, w_ref, b_ref, o_ref):
    c = c_ref[...]
    o_ref[...] = _dot(c * _sigmoid(c), w_ref[...]) + b_ref[...]


def _modulation(c_all, w_mod, b_mod):
    n, d = c_all.shape
    width = w_mod.shape[1]
    tn = 1536
    return pl.pallas_call(
        _mod_kernel,
        grid=(width // tn,),
        in_specs=[pl.BlockSpec((n, d), lambda j: (0, 0)),
                  pl.BlockSpec((d, tn), lambda j: (0, j)),
                  pl.BlockSpec((1, tn), lambda j: (0, j))],
        out_specs=pl.BlockSpec((n, tn), lambda j: (0, j)),
        out_shape=jax.ShapeDtypeStruct((n, width), F32),
        compiler_params=_cparams("parallel"),
        name="modulation",
    )(c_all, w_mod, b_mod.reshape(1, width))


N_PROJ = 11


def _inproj_kernel(x_ref, sh_ref, sc_ref, wa_ref, ws_ref, gw2_ref, gb_ref, fb_ref,
                   h1_ref, gq_ref, gk_ref, gv_ref, gr_ref, mq_ref, mk_ref, mv_ref, fq_ref, fk_ref, fv_ref,
                   su_ref, la_ref, sm_ref):
    h1 = (_ln(x_ref[...]) * (1.0 + sc_ref[0]) + sh_ref[0]).astype(MXU_DTYPE)
    h1_ref[...] = h1
    outs = (gq_ref, gk_ref, gv_ref, gr_ref, mq_ref, mk_ref, mv_ref, fq_ref, fk_ref, fv_ref, su_ref)
    for i, r in enumerate(outs):
        r[...] = jnp.dot(h1, wa_ref[:, i * BRANCH_WIDTH:(i + 1) * BRANCH_WIDTH],
                         preferred_element_type=F32).astype(r.dtype)
    ps = jnp.dot(h1, ws_ref[...], preferred_element_type=F32)
    la_ref[...] = _log_sigmoid(_dot(ps, gw2_ref[...]) + gb_ref[...]) * (1.0 / GLA_TAU)
    sm_ref[...] = _log_sigmoid(ps + fb_ref[...])


def _inproj(x, sh, sc, wa, ws, gw2, gb, fb, tm):
    n, d = x.shape
    nt = n // tm
    mrow = sh.shape[1]
    per_tile = sh.shape[0] == nt and mrow == tm
    tiles_per_seq = nt // sh.shape[0] if not per_tile else 1
    mod_map = (lambda i: (i, 0, 0)) if per_tile else (lambda i: (i // tiles_per_seq, 0, 0))
    tok = lambda w: pl.BlockSpec((tm, w), lambda i: (i, 0))
    full = lambda a: pl.BlockSpec(a.shape, lambda i: (0,) * a.ndim)
    bw = BRANCH_WIDTH
    odt = [MXU_DTYPE] + [F32] * (N_PROJ + 2)
    owd = [d] + [bw] * N_PROJ + [bw, LANES]
    return pl.pallas_call(
        _inproj_kernel,
        grid=(nt,),
        in_specs=[tok(d), pl.BlockSpec((1, mrow, d), mod_map), pl.BlockSpec((1, mrow, d), mod_map),
                  full(wa), full(ws), full(gw2), full(gb), full(fb)],
        out_specs=[tok(w) for w in owd],
        out_shape=[jax.ShapeDtypeStruct((n, w), t) for w, t in zip(owd, odt)],
        compiler_params=_cparams("parallel"),
        name="inproj",
    )(x, sh, sc, wa, ws, gw2, gb, fb)


def _gla_kernel(q_ref, k_ref, v_ref, la_ref, gr_ref, gn_ref, s0_ref, o_ref, sT_ref, *, seq, chunk, sub):
    n_chunks = seq // chunk
    n_sub = chunk // sub
    scale = HEAD_DIM ** -0.5
    tril = (_iota((chunk, chunk), 1) <= _iota((chunk, chunk), 0))
    gn = gn_ref[...]

    def one_chunk(c, state):
        r0 = c * chunk if isinstance(c, int) else pl.multiple_of(c * chunk, chunk)
        rows = pl.ds(r0, chunk)
        g = la_ref[rows, :]
        bcum = _dot01_exact(tril, g)
        q = q_ref[rows, :] * scale
        k = k_ref[rows, :]
        v = v_ref[rows, :]
        b_end = bcum[chunk - 1:chunk, :]
        qe = q * jnp.exp(bcum)
        kd = k * jnp.exp(b_end - bcum)
        qs, ks = [], []
        for i in range(n_sub):
            base = bcum[i * sub - 1:i * sub, :] if i > 0 else jnp.zeros_like(b_end)
            nk = (i + 1) * sub
            qs.append(q[i * sub:nk, :] * jnp.exp(bcum[i * sub:nk, :] - base))
            ks.append(k[:nk, :] * jnp.exp(base - bcum[:nk, :]))
        new_state, outs = [], []
        for h in range(N_HEADS):
            sl = slice(h * HEAD_DIM, (h + 1) * HEAD_DIM)
            s_t = state[h]
            o_h = _dot_nt(qe[:, sl], s_t)
            att_rows = []
            for i in range(n_sub):
                nk = (i + 1) * sub
                a = _dot_nt(qs[i][:, sl], ks[i][:, sl])
                causal = _iota((sub, nk), 1) <= _iota((sub, nk), 0) + i * sub
                a = jnp.where(causal, a, 0.0)
                if nk < chunk:
                    a = jnp.concatenate([a, jnp.zeros((sub, chunk - nk), F32)], axis=1)
                att_rows.append(a)
            att = att_rows[0] if n_sub == 1 else jnp.concatenate(att_rows, axis=0)
            o_h = o_h + _dot(att, v[:, sl])
            new_state.append(s_t * jnp.exp(b_end[:, sl]) + _dot_tn(v[:, sl], kd[:, sl]))
            o_h = o_h * lax.rsqrt(jnp.mean(o_h * o_h, axis=-1, keepdims=True) + LN_EPS)
            outs.append(o_h)
        o = jnp.concatenate(outs, axis=1)
        gr = gr_ref[rows, :]
        o_ref[rows, :] = (o * gn * (gr * _sigmoid(gr))).astype(o_ref.dtype)
        return tuple(new_state)

    state0 = tuple(s0_ref[0, h] for h in range(N_HEADS))
    if n_chunks == 1:
        state = one_chunk(0, state0)
    else:
        state = lax.fori_loop(0, n_chunks, one_chunk, state0)
    for h in range(N_HEADS):
        sT_ref[0, h] = state[h]


def _gla(gq, gk, gv, la, gr, gn, s0_t, n_seq, seq):
    chunk = math.gcd(seq, GLA_CHUNK)
    sub = math.gcd(chunk, GLA_SUBCHUNK)
    tok = pl.BlockSpec((seq, BRANCH_WIDTH), lambda b: (b, 0))
    st = pl.BlockSpec((1, N_HEADS, HEAD_DIM, HEAD_DIM), lambda b: (b, 0, 0, 0))
    return pl.pallas_call(
        functools.partial(_gla_kernel, seq=seq, chunk=chunk, sub=sub),
        grid=(n_seq,),
        in_specs=[tok, tok, tok, tok, tok, pl.BlockSpec((1, BRANCH_WIDTH), lambda b: (0, 0)), st],
        out_specs=[tok, st],
        out_shape=[jax.ShapeDtypeStruct((n_seq * seq, BRANCH_WIDTH), F32),
                   jax.ShapeDtypeStruct((n_seq, N_HEADS, HEAD_DIM, HEAD_DIM), F32)],
        compiler_params=_cparams("parallel"),
        name="gla",
    )(gq, gk, gv, la, gr, gn, s0_t)


def _cum_kernel(lf_ref, o_ref):
    rows, length = lf_ref.shape
    triu = _iota((LANES, LANES), 0) <= _iota((LANES, LANES), 1)
    carry = jnp.zeros((rows, 1), F32)
    for c in range(length // LANES):
        cs = _dot_exact01(lf_ref[:, c * LANES:(c + 1) * LANES], triu) + carry
        o_ref[:, c * LANES:(c + 1) * LANES] = cs
        carry = cs[:, LANES - 1:LANES]


def _fox_cumsum(lf_rows):
    return pl.pallas_call(
        _cum_kernel,
        out_shape=jax.ShapeDtypeStruct(lf_rows.shape, F32),
        compiler_params=pltpu.CompilerParams(vmem_limit_bytes=V7X_VMEM_LIMIT),
        name="fox_cumsum",
    )(lf_rows)


def _alibi_slope(h):
    return 2.0 ** (-8.0 * (h + 1) / N_HEADS)


def _attn_prompt_kernel(q_ref, k_ref, v_ref, *rest, seq, moba):
    if moba:
        o_ref, kh_ref, vh_ref, km_ref, mask_ref = rest
    else:
        cum_ref, o_ref, kh_ref, vh_ref = rest
    blk = MOBA_BLOCK
    nb = seq // blk
    i = pl.program_id(1)

    @pl.when(i == 0)
    def _():
        for h in range(N_HEADS):
            sl = slice(h * HEAD_DIM, (h + 1) * HEAD_DIM)
            kf = k_ref[:, sl]
            kh_ref[h] = kf.astype(MXU_DTYPE)
            vh_ref[h] = v_ref[:, sl].astype(MXU_DTYPE)
            if moba:
                km = jnp.sum(kf.reshape(nb, blk, HEAD_DIM), axis=1) * (1.0 / blk)
                pad = km_ref.shape[1] - nb
                km_ref[h] = jnp.concatenate([km, jnp.zeros((pad, HEAD_DIM), F32)], axis=0) if pad else km

    nbp = km_ref.shape[1] if moba else 0
    q_pos = i * blk + _iota((blk, blk), 0)
    outs = []
    for h in range(N_HEADS):
        q_h = q_ref[:, h * HEAD_DIM:(h + 1) * HEAD_DIM]
        if moba:
            bs = _dot_nt(km_ref[h], q_h)
            jrow = _iota((nbp, blk), 0)
            bs = jnp.where(jrow < i, bs, -jnp.inf)
            rank = jnp.zeros((nbp, blk), F32)
            for j2 in range(nb):
                row = bs[j2:j2 + 1, :]
                beats = (row > bs) | ((row == bs) & (j2 < jrow))
                rank = rank + beats.astype(F32)
            sel = ((jrow < i) & (rank < MOBA_TOPK)) | (jrow == i)
            expand = (_iota((nbp, seq), 1) // blk == _iota((nbp, seq), 0))
            mask_ref[...] = _dot(sel.astype(F32).T, expand.astype(F32))
            slope = _alibi_slope(h)

        def body(j, carry):
            m, l, acc = carry
            c0 = pl.multiple_of(j * blk, blk)
            s = _dot_nt(q_h, kh_ref[h, pl.ds(c0, blk), :]) * (HEAD_DIM ** -0.5)
            dist = (q_pos - (j * blk + _iota((blk, blk), 1))).astype(F32)
            if moba:
                s = s - slope * dist
                ok = (mask_ref[:, pl.ds(c0, blk)] > 0.5) & (dist >= 0)
            else:
                s = s - cum_ref[0, h:h + 1, pl.ds(c0, blk)]
                ok = dist >= 0
            s = jnp.where(ok, s, NEG_BIG)
            m_new = jnp.maximum(m, jnp.max(s, axis=-1, keepdims=True))
            alpha = jnp.exp(m - m_new)
            p = jnp.exp(s - m_new)
            l = alpha * l + jnp.sum(p, axis=-1, keepdims=True)
            acc = alpha * acc + _dot(p, vh_ref[h, pl.ds(c0, blk), :])
            return m_new, l, acc

        init = (jnp.full((blk, 1), NEG_BIG, F32), jnp.zeros((blk, 1), F32), jnp.zeros((blk, HEAD_DIM), F32))
        m, l, acc = lax.fori_loop(0, i + 1, body, init)
        outs.append(acc / l)
    o_ref[...] = jnp.concatenate(outs, axis=1).astype(o_ref.dtype)


def _attn_prompt(q, k, v, cum, n_seq, seq, moba):
    blk = MOBA_BLOCK
    nq = seq // blk
    nbp = -(-nq // 8) * 8
    tile = pl.BlockSpec((blk, BRANCH_WIDTH), lambda b, i: (b * nq + i, 0))
    whole = pl.BlockSpec((seq, BRANCH_WIDTH), lambda b, i: (b, 0))
    in_specs = [tile, whole, whole]
    args = [q, k, v]
    scratch = [pltpu.VMEM((N_HEADS, seq, HEAD_DIM), MXU_DTYPE), pltpu.VMEM((N_HEADS, seq, HEAD_DIM), MXU_DTYPE)]
    if moba:
        scratch += [pltpu.VMEM((N_HEADS, nbp, HEAD_DIM), F32), pltpu.VMEM((blk, seq), F32)]
    else:
        in_specs.append(pl.BlockSpec((1, N_HEADS, seq), lambda b, i: (b, 0, 0)))
        args.append(cum)
    return pl.pallas_call(
        functools.partial(_attn_prompt_kernel, seq=seq, moba=moba),
        grid=(n_seq, nq),
        in_specs=in_specs,
        out_specs=tile,
        out_shape=jax.ShapeDtypeStruct((n_seq * seq, BRANCH_WIDTH), F32),
        scratch_shapes=scratch,
        compiler_params=_cparams("parallel", "arbitrary"),
        name="moba_prompt" if moba else "fox_prompt",
    )(*args)


def _block_diag_queries(q):
    t = q.shape[0]
    col_head = _iota((t, BRANCH_WIDTH), 1) // HEAD_DIM
    return jnp.concatenate([jnp.where(col_head == h, q, jnp.zeros_like(q)) for h in range(N_HEADS)], axis=0)


def _head_diag(acc, t):
    return jnp.concatenate([acc[h * t:(h + 1) * t, h * HEAD_DIM:(h + 1) * HEAD_DIM] for h in range(N_HEADS)], axis=1)


def _attn_sample_kernel(pt_ref, mq_ref, mkn_ref, mvn_ref, fq_ref, fkn_ref, fvn_ref, lfn_ref,
                        mkc_ref, mvc_ref, fkc_ref, fvc_ref, lfc_ref,
                        ob_ref, oc_ref,
                        qm_ref, qf_ref, mpg_ref, lpg_ref, apg_ref, ksum_ref, mf_ref, lf_ref, af_ref, cc_ref,
                        *, t_new, n_pages, page, past_len):
    del pt_ref
    j = pl.program_id(1)
    rows = N_HEADS * t_new
    scale = HEAD_DIM ** -0.5
    row_head = _iota((rows, 1), 0) // t_new
    row_t = _iota((rows, 1), 0) % t_new
    slope = jnp.zeros((rows, 1), F32)
    for h in range(N_HEADS):
        slope = jnp.where(row_head == h, _alibi_slope(h), slope)
    q_pos = (past_len + row_t).astype(F32)

    @pl.when(j == 0)
    def _():
        qm_ref[...] = _block_diag_queries(mq_ref[...])
        qf_ref[...] = _block_diag_queries(fq_ref[...])
        mpg_ref[...] = jnp.full(mpg_ref.shape, NEG_BIG, F32)
        lpg_ref[...] = jnp.zeros(lpg_ref.shape, F32)
        mf_ref[...] = jnp.full(mf_ref.shape, NEG_BIG, F32)
        lf_ref[...] = jnp.zeros(lf_ref.shape, F32)
        af_ref[...] = jnp.zeros(af_ref.shape, F32)
        cc_ref[...] = jnp.zeros(cc_ref.shape, F32)

    lane = _iota((rows, page), 1)
    kc = mkc_ref[...]
    ksum_ref[pl.ds(j, 1), :] = jnp.sum(kc, axis=0, keepdims=True)
    k_pos = (j * page + lane).astype(F32)
    s = _dot_nt(qm_ref[...], kc) * scale - slope * (q_pos - k_pos)
    m_p = jnp.max(s, axis=-1, keepdims=True)
    p = jnp.exp(s - m_p)
    pg_lane = _iota((rows, LANES), 1) == j
    mpg_ref[...] = jnp.where(pg_lane, m_p, mpg_ref[...])
    lpg_ref[...] = jnp.where(pg_lane, jnp.sum(p, axis=-1, keepdims=True), lpg_ref[...])
    apg_ref[j] = _dot(p, mvc_ref[...])
    triu = _iota((page, page), 0) <= _iota((page, page), 1)
    cs = _dot_exact01(lfc_ref[...], triu) + cc_ref[:, 0:1]
    cc_ref[...] = jnp.broadcast_to(cs[:, page - 1:page], cc_ref.shape)
    bias = jnp.broadcast_to(cs[:, None, :], (N_HEADS, t_new, page)).reshape(rows, page)
    s = _dot_nt(qf_ref[...], fkc_ref[...]) * scale - bias
    m_old = mf_ref[...]
    m_new = jnp.maximum(m_old, jnp.max(s, axis=-1, keepdims=True))
    alpha = jnp.exp(m_old - m_new)
    p = jnp.exp(s - m_new)
    mf_ref[...] = m_new
    lf_ref[...] = alpha * lf_ref[...] + jnp.sum(p, axis=-1, keepdims=True)
    af_ref[...] = alpha * af_ref[...] + _dot(p, fvc_ref[...])

    @pl.when(j == n_pages - 1)
    def _():
        lane_n = _iota((rows, t_new), 1)
        causal = lane_n <= row_t
        s_n = _dot_nt(qm_ref[...], mkn_ref[...]) * scale - slope * (row_t - lane_n).astype(F32)
        s_n = jnp.where(causal, s_n, NEG_BIG)
        m_n = jnp.max(s_n, axis=-1, keepdims=True)
        p_n = jnp.exp(s_n - m_n)
        l_n = jnp.sum(p_n, axis=-1, keepdims=True)
        a_n = _dot(p_n, mvn_ref[...])
        pages_per_block = MOBA_BLOCK // page
        n_blocks = n_pages // pages_per_block
        kmean = jnp.sum(ksum_ref[...].reshape(n_blocks, pages_per_block, BRANCH_WIDTH), axis=1) * (1.0 / MOBA_BLOCK)
        bs = _dot_nt(qm_ref[...], kmean)
        jl = _iota((rows, n_blocks), 1)
        rank = jnp.zeros((rows, n_blocks), F32)
        for j2 in range(n_blocks):
            col = bs[:, j2:j2 + 1]
            rank = rank + ((col > bs) | ((col == bs) & (j2 < jl))).astype(F32)
        sel = rank < MOBA_TOPK
        m_pg = mpg_ref[...]
        l_pg = lpg_ref[...]
        m_all = m_n
        for pg in range(n_pages):
            b = pg // pages_per_block
            m_all = jnp.maximum(m_all, jnp.where(sel[:, b:b + 1], m_pg[:, pg:pg + 1], NEG_BIG))
        w_n = jnp.exp(m_n - m_all)
        l_all = w_n * l_n
        a_all = w_n * a_n
        for pg in range(n_pages):
            b = pg // pages_per_block
            w = jnp.where(sel[:, b:b + 1], jnp.exp(jnp.minimum(m_pg[:, pg:pg + 1] - m_all, 0.0)), 0.0)
            l_all = l_all + w * l_pg[:, pg:pg + 1]
            a_all = a_all + w * apg_ref[pg]
        ob_ref[...] = _head_diag(a_all / l_all, t_new).astype(ob_ref.dtype)
        lfn = lfn_ref[...]
        cols, c = [], cc_ref[:, 0:1]
        for t in range(t_new):
            c = c + lfn[:, t:t + 1]
            cols.append(c)
        cum_n = jnp.concatenate(cols, axis=1)
        bias_n = jnp.broadcast_to(cum_n[:, None, :], (N_HEADS, t_new, t_new)).reshape(rows, t_new)
        s_f = _dot_nt(qf_ref[...], fkn_ref[...]) * scale - bias_n
        s_f = jnp.where(causal, s_f, NEG_BIG)
        m_old = mf_ref[...]
        m_new = jnp.maximum(m_old, jnp.max(s_f, axis=-1, keepdims=True))
        alpha = jnp.exp(m_old - m_new)
        p_f = jnp.exp(s_f - m_new)
        l_f = alpha * lf_ref[...] + jnp.sum(p_f, axis=-1, keepdims=True)
        a_f = alpha * af_ref[...] + _dot(p_f, fvn_ref[...])
        oc_ref[...] = _head_diag(a_f / l_f, t_new).astype(oc_ref.dtype)


def _attn_sample(page_table, mq, mkn, mvn, fq, fkn, fvn, lfn, mkc, mvc, fkc, fvc, lfc, layer):
    n_seq, t_new, _ = mq.shape
    n_pages = page_table.shape[1]
    page = mkc.shape[2]
    past_len = n_pages * page
    assert past_len % MOBA_BLOCK == 0 and MOBA_BLOCK % page == 0 and t_new <= MOBA_BLOCK and n_pages <= LANES
    rows = N_HEADS * t_new
    new = pl.BlockSpec((None, t_new, BRANCH_WIDTH), lambda b, j, pt: (b, 0, 0))
    cache = pl.BlockSpec((None, None, page, BRANCH_WIDTH), lambda b, j, pt: (layer, pt[b, j], 0, 0))
    grid_spec = pltpu.PrefetchScalarGridSpec(
        num_scalar_prefetch=1,
        grid=(n_seq, n_pages),
        in_specs=[new, new, new, new, new, new,
                  pl.BlockSpec((None, N_HEADS, t_new), lambda b, j, pt: (b, 0, 0)),
                  cache, cache, cache, cache,
                  pl.BlockSpec((None, None, N_HEADS, page), lambda b, j, pt: (layer, pt[b, j], 0, 0))],
        out_specs=[new, new],
        scratch_shapes=[pltpu.VMEM((rows, BRANCH_WIDTH), F32), pltpu.VMEM((rows, BRANCH_WIDTH), F32),
                        pltpu.VMEM((rows, LANES), F32), pltpu.VMEM((rows, LANES), F32),
                        pltpu.VMEM((n_pages, rows, BRANCH_WIDTH), F32), pltpu.VMEM((n_pages, BRANCH_WIDTH), F32),
                        pltpu.VMEM((rows, 1), F32), pltpu.VMEM((rows, 1), F32), pltpu.VMEM((rows, BRANCH_WIDTH), F32),
                        pltpu.VMEM((N_HEADS, LANES), F32)],
    )
    return pl.pallas_call(
        functools.partial(_attn_sample_kernel, t_new=t_new, n_pages=n_pages, page=page, past_len=past_len),
        grid_spec=grid_spec,
        out_shape=[jax.ShapeDtypeStruct((n_seq, t_new, BRANCH_WIDTH), F32)] * 2,
        compiler_params=_cparams("parallel", "arbitrary"),
        name="attn_sample",
    )(page_table, mq, mkn, mvn, fq, fkn, fvn, lfn, mkc, mvc, fkc, fvc, lfc)


def _s5_param_kernel(lre_ref, lim_ref, ldt_ref, bre_ref, bim_ref, are_ref, aim_ref, bbre_ref, bbim_ref):
    lam_re = lre_ref[...]
    lam_im = lim_ref[...]
    dt = jnp.exp(ldt_ref[...])
    mag = jnp.exp(lam_re * dt)
    ab_re = mag * jnp.cos(lam_im * dt)
    ab_im = mag * jnp.sin(lam_im * dt)
    den = lam_re * lam_re + lam_im * lam_im
    z_re = ((ab_re - 1.0) * lam_re + ab_im * lam_im) / den
    z_im = (ab_im * lam_re - (ab_re - 1.0) * lam_im) / den
    are_ref[...] = ab_re
    aim_ref[...] = ab_im
    b_re = bre_ref[...]
    b_im = bim_ref[...]
    bbre_ref[...] = z_re[:, None, :] * b_re - z_im[:, None, :] * b_im
    bbim_ref[...] = z_re[:, None, :] * b_im + z_im[:, None, :] * b_re


def _s5_params(lam_re, lam_im, log_dt, b_re, b_im):
    g, p = lam_re.shape
    i = b_re.shape[-1]
    outs = pl.pallas_call(
        _s5_param_kernel,
        out_shape=[jax.ShapeDtypeStruct((g, p), F32)] * 2 + [jax.ShapeDtypeStruct((g, i, p), F32)] * 2,
        name="s5_params",
    )(lam_re, lam_im, log_dt.reshape(g, 1), b_re.transpose(0, 2, 1), b_im.transpose(0, 2, 1))
    return outs


def _s5_kernel(u_ref, bd_ref, cd_ref, are_ref, aim_ref, d_ref, wg_ref, bg_ref, h0r_ref, h0i_ref,
               o_ref, hr_ref, hi_ref, bu_ref, hs_ref, *, steps, nb):
    ns = S5_WIDTH_STATE

    @pl.when(pl.program_id(0) == 0)
    def _():
        hr_ref[...] = h0r_ref[...]
        hi_ref[...] = h0i_ref[...]

    u = u_ref[...].reshape(steps * nb, BRANCH_WIDTH)
    bu_ref[...] = _dot(u, bd_ref[...])
    a_re = jnp.broadcast_to(are_ref[...], (nb, ns))
    a_im = jnp.broadcast_to(aim_ref[...], (nb, ns))

    def step(t, carry):
        h_re, h_im = carry
        rows = pl.ds(pl.multiple_of(t * nb, nb), nb)
        n_re = a_re * h_re - a_im * h_im + bu_ref[rows, 0:ns]
        n_im = a_re * h_im + a_im * h_re + bu_ref[rows, ns:2 * ns]
        hs_ref[rows, 0:ns] = n_re
        hs_ref[rows, ns:2 * ns] = n_im
        return n_re, n_im

    h_re, h_im = lax.fori_loop(0, steps, step, (hr_ref[...], hi_ref[...]))
    hr_ref[...] = h_re
    hi_ref[...] = h_im
    y = _dot(hs_ref[...], cd_ref[...]) + d_ref[...] * u
    z = _gelu(y)
    o = z * _sigmoid(_dot(z, wg_ref[...]) + bg_ref[...])
    o_ref[...] = o.reshape(steps, nb, BRANCH_WIDTH).astype(o_ref.dtype)


def _s5(u_tb, bd, cd, a_re, a_im, d, wg, bg, h0r, h0i, steps):
    t_len, nb, _ = u_tb.shape
    full = lambda a: pl.BlockSpec(a.shape, lambda i: (0,) * a.ndim)
    blk = pl.BlockSpec((steps, nb, BRANCH_WIDTH), lambda i: (i, 0, 0))
    st = pl.BlockSpec((nb, S5_WIDTH_STATE), lambda i: (0, 0))
    return pl.pallas_call(
        functools.partial(_s5_kernel, steps=steps, nb=nb),
        grid=(t_len // steps,),
        in_specs=[blk, full(bd), full(cd), full(a_re), full(a_im), full(d), full(wg), full(bg), st, st],
        out_specs=[blk, st, st],
        out_shape=[jax.ShapeDtypeStruct((t_len, nb, BRANCH_WIDTH), F32),
                   jax.ShapeDtypeStruct((nb, S5_WIDTH_STATE), F32), jax.ShapeDtypeStruct((nb, S5_WIDTH_STATE), F32)],
        scratch_shapes=[pltpu.VMEM((steps * nb, 2 * S5_WIDTH_STATE), F32),
                        pltpu.VMEM((steps * nb, 2 * S5_WIDTH_STATE), F32)],
        compiler_params=_cparams("arbitrary"),
        name="s5",
    )(u_tb, bd, cd, a_re, a_im, d, wg, bg, h0r, h0i)


def _merge_kernel(x_ref, h1_ref, oa_ref, ob_ref, oc_ref, od_ref, g1_ref, sh2_ref, sc2_ref,
                  wg_ref, wb_ref, wo_ref, lg_ref, lb_ref, x1_ref, h2_ref, *, alpha):
    d = x_ref.shape[1]
    h1 = h1_ref[...]
    merged = None
    for n, br in enumerate((oa_ref, ob_ref, oc_ref, od_ref)):
        gate = _sigmoid(jnp.dot(h1, wg_ref[:, n * d:(n + 1) * d], preferred_element_type=F32))
        up = _dot(br[...], wb_ref[n])
        merged = gate * up if merged is None else merged + gate * up
    mix = _dot(merged, wo_ref[...])
    x1 = _ln(alpha * x_ref[...] + (1.0 + g1_ref[0]) * mix) * lg_ref[...] + lb_ref[...]
    x1_ref[...] = x1
    h2_ref[...] = (_ln(x1) * (1.0 + sc2_ref[0]) + sh2_ref[0]).astype(h2_ref.dtype)


def _merge(x, h1, oa, ob, oc, od, g1, sh2, sc2, wg, wb, wo, lg, lb, tm, alpha):
    n, d = x.shape
    nt = n // tm
    mrow = g1.shape[1]
    per_tile = g1.shape[0] == nt and mrow == tm
    tiles_per_seq = nt // g1.shape[0] if not per_tile else 1
    mod_map = (lambda i: (i, 0, 0)) if per_tile else (lambda i: (i // tiles_per_seq, 0, 0))
    mod = pl.BlockSpec((1, mrow, d), mod_map)
    tok = lambda w: pl.BlockSpec((tm, w), lambda i: (i, 0))
    full = lambda a: pl.BlockSpec(a.shape, lambda i: (0,) * a.ndim)
    bw = BRANCH_WIDTH
    return pl.pallas_call(
        functools.partial(_merge_kernel, alpha=alpha),
        grid=(nt,),
        in_specs=[tok(d), tok(d), tok(bw), tok(bw), tok(bw), tok(bw), mod, mod, mod,
                  full(wg), full(wb), full(wo), full(lg), full(lb)],
        out_specs=[tok(d), tok(d)],
        out_shape=[jax.ShapeDtypeStruct((n, d), F32), jax.ShapeDtypeStruct((n, d), MXU_DTYPE)],
        compiler_params=_cparams("parallel"),
        name="merge",
    )(x, h1, oa, ob, oc, od, g1, sh2, sc2, wg, wb, wo, lg, lb)


def _topk_rows(s, k):
    nrow = s.shape[0]
    rid = _iota(s.shape, 0)
    rank = jnp.full(s.shape, float(k), F32)
    vals = []
    for r in range(k):
        m = jnp.max(s, axis=0, keepdims=True)
        idx = jnp.min(jnp.where(s == m, rid, nrow), axis=0, keepdims=True)
        hit = rid == idx
        rank = jnp.where(hit, float(r), rank)
        s = jnp.where(hit, -jnp.inf, s)
        vals.append(m)
    return jnp.concatenate(vals, axis=0), rank


def _peer_route_kernel(h_ref, wq_ref, sk_ref, cnt_ref, e0_ref, e1_ref, r1_ref, q_ref):
    kk = PEER_TOPK
    half = PEER_DKEY // 2
    tn = h_ref.shape[0]
    q_ref[...] = jnp.dot(h_ref[...], wq_ref[...], preferred_element_type=F32)

    @pl.loop(0, PEER_HEADS)
    def _(h):
        s, sv, rk = [], [], []
        for p in range(2):
            c0 = pl.multiple_of(h * PEER_DKEY + p * half, half)
            sp = _dot_nt(sk_ref[p], q_ref[:, pl.ds(c0, half)])
            v, r = _topk_rows(sp, kk)
            s.append(sp)
            sv.append(v)
            rk.append(r)
        cand = (sv[0][:, None, :] + sv[1][None, :, :]).reshape(kk * kk, tn)
        _, crank = _topk_rows(cand, kk)
        taken = crank < float(kk)
        cnt = jnp.sum(taken.astype(F32).reshape(kk, kk, tn), axis=1)
        top = sv[0][0:1, :] + sv[1][0:1, :]
        z = jnp.sum(jnp.where(taken, jnp.exp(cand - top), 0.0), axis=0, keepdims=True)
        cnt_dense = jnp.zeros((PEER_NKEYS, tn), F32)
        for k0 in range(kk):
            cnt_dense = jnp.where(rk[0] == float(k0), cnt[k0:k0 + 1, :], cnt_dense)
        cnt_ref[h] = cnt_dense
        e0_ref[h] = jnp.exp(s[0] - sv[0][0:1, :])
        e1_ref[h] = jnp.exp(s[1] - sv[1][0:1, :]) / z
        r1_ref[h] = rk[1]


def _peer_route(h2, wq, sk, tn):
    n, d = h2.shape
    full = lambda a: pl.BlockSpec(a.shape, lambda i: (0,) * a.ndim)
    out = pl.BlockSpec((PEER_HEADS, PEER_NKEYS, tn), lambda i: (0, 0, i))
    return pl.pallas_call(
        _peer_route_kernel,
        grid=(n // tn,),
        in_specs=[pl.BlockSpec((tn, d), lambda i: (i, 0)), full(wq), full(sk)],
        out_specs=[out] * 4,
        out_shape=[jax.ShapeDtypeStruct((PEER_HEADS, PEER_NKEYS, n), F32)] * 4,
        scratch_shapes=[pltpu.VMEM((tn, PEER_HEADS * PEER_DKEY), F32)],
        compiler_params=_cparams("parallel"),
        name="peer_route",
    )(h2, wq, sk)


def _peer_expert_kernel(h_ref, x1_ref, g2_ref, cnt_ref, e0_ref, e1_ref, r1_ref, u_ref, vt_ref, lg_ref, lb_ref,
                        o_ref, acc_ref, g_ref, *, alpha, i0_per_tile):
    e = pl.program_id(1)
    tn = h_ref.shape[0]

    @pl.when(e == 0)
    def _():
        acc_ref[...] = jnp.zeros(acc_ref.shape, F32)

    h_tok = h_ref[...]

    @pl.loop(0, i0_per_tile)
    def _(il):
        rows = pl.ds(pl.multiple_of(il * PEER_NKEYS, PEER_NKEYS), PEER_NKEYS)
        row = pl.ds(e * i0_per_tile + il, 1)
        act = _gelu(_dot_nt(u_ref[rows, :], h_tok))
        w = jnp.zeros((PEER_NKEYS, tn), F32)
        for h in range(PEER_HEADS):
            cnt = cnt_ref[h, row, :]
            w = w + jnp.where(r1_ref[h] < cnt, e1_ref[h], 0.0) * e0_ref[h, row, :]
        g_ref[rows, :] = (w * act).astype(g_ref.dtype)

    acc_ref[...] += jnp.dot(vt_ref[...], g_ref[...], preferred_element_type=F32)

    @pl.when(e == pl.num_programs(1) - 1)
    def _():
        ff = acc_ref[...].T
        o_ref[...] = _ln(alpha * x1_ref[...] + (1.0 + g2_ref[0]) * ff) * lg_ref[...] + lb_ref[...]


def _peer_expert(h2, x1, g2, cnt, e0, e1, r1, u, vt, lg, lb, tn, te, alpha):
    n, d = h2.shape
    nt = n // tn
    n_exp = u.shape[0]
    mrow = g2.shape[1]
    per_tile = g2.shape[0] == nt and mrow == tn
    tiles_per_seq = nt // g2.shape[0] if not per_tile else 1
    mod_map = (lambda i, e: (i, 0, 0)) if per_tile else (lambda i, e: (i // tiles_per_seq, 0, 0))
    tok = pl.BlockSpec((tn, d), lambda i, e: (i, 0))
    route = pl.BlockSpec((PEER_HEADS, PEER_NKEYS, tn), lambda i, e: (0, 0, i))
    vec = pl.BlockSpec((1, d), lambda i, e: (0, 0))
    return pl.pallas_call(
        functools.partial(_peer_expert_kernel, alpha=alpha, i0_per_tile=te // PEER_NKEYS),
        grid=(nt, n_exp // te),
        in_specs=[tok, tok, pl.BlockSpec((1, mrow, d), mod_map), route, route, route, route,
                  pl.BlockSpec((te, d), lambda i, e: (e, 0)), pl.BlockSpec((d, te), lambda i, e: (0, e)), vec, vec],
        out_specs=tok,
        out_shape=jax.ShapeDtypeStruct((n, d), F32),
        scratch_shapes=[pltpu.VMEM((d, tn), F32), pltpu.VMEM((te, tn), MXU_DTYPE)],
        compiler_params=_cparams("parallel", "arbitrary"),
        name="peer_expert",
    )(h2, x1, g2, cnt, e0, e1, r1, u, vt, lg, lb)


def _block_diag_in(bb_t):
    g, i, p = bb_t.shape
    return jnp.einsum('gip,gh->gihp', bb_t, jnp.eye(g, dtype=bb_t.dtype)).reshape(g * i, g * p)


def _block_diag_out(c):
    g, i, p = c.shape
    return jnp.einsum('gip,gh->gphi', c, jnp.eye(g, dtype=c.dtype)).reshape(g * p, g * i)


def _tile(n, pref):
    return pref if n % pref == 0 else n


def _layer_group(x, mods, w, n_seq, seq, past, layer):
    n, d = x.shape
    bw = BRANCH_WIDTH
    alpha = w['alpha']
    tm = _tile(n, 512)
    tm2 = _tile(n, 256)

    def mod_arr(m, t):
        if seq % t == 0:
            return m.reshape(n_seq, 1, d)
        return jnp.repeat(m, seq, axis=0).reshape(n // t, t, d)

    sh1, sc1, g1, sh2, sc2, g2 = mods
    (h1, gq, gk, gv, gr, mq, mk, mv, fq, fk, fv, su, la, sm) = _inproj(
        x, mod_arr(sh1, tm), mod_arr(sc1, tm), w['wa'], w['ws'], w['gw2'], w['gb'], w['fb'], tm)
    logf = sm[:, GLA_GATE_RANK:GLA_GATE_RANK + N_HEADS]

    s0_t = past['gla'].transpose(0, 1, 3, 2)
    oa, gla_t = _gla(gq, gk, gv, la, gr, w['gn'], s0_t, n_seq, seq)
    gla_state = gla_t.transpose(0, 1, 3, 2)

    if past['paged']:
        r3 = lambda a: a.reshape(n_seq, seq, bw)
        lfn = logf.reshape(n_seq, seq, N_HEADS).transpose(0, 2, 1)
        ob, oc = _attn_sample(past['page_table'], r3(mq), r3(mk), r3(mv), r3(fq), r3(fk), r3(fv), lfn,
                              past['moba_k'], past['moba_v'], past['fox_k'], past['fox_v'], past['fox_logf_t'], layer)
        ob = ob.reshape(n, bw)
        oc = oc.reshape(n, bw)
    else:
        lf_rows = logf.reshape(n_seq, seq, N_HEADS).transpose(0, 2, 1).reshape(n_seq * N_HEADS, seq)
        cum = _fox_cumsum(lf_rows).reshape(n_seq, N_HEADS, seq)
        ob = _attn_prompt(mq, mk, mv, None, n_seq, seq, True)
        oc = _attn_prompt(fq, fk, fv, cum, n_seq, seq, False)

    steps = math.gcd(seq, 128)
    u_tb = su.reshape(n_seq, seq, bw).transpose(1, 0, 2)
    od_tb, s5_re, s5_im = _s5(u_tb, w['bd'], w['cd'], w['a_re'], w['a_im'], w['s5_d'], w['wglu'], w['bglu'],
                              past['s5_re'].reshape(n_seq, S5_WIDTH_STATE), past['s5_im'].reshape(n_seq, S5_WIDTH_STATE),
                              steps)
    od = od_tb.transpose(1, 0, 2).reshape(n, bw)

    x1, h2 = _merge(x, h1, oa, ob, oc, od, mod_arr(g1, tm2), mod_arr(sh2, tm2), mod_arr(sc2, tm2),
                    w['wg'], w['wb'], w['wo'], w['ln1_g'], w['ln1_b'], tm2, alpha)

    cnt, e0, e1, r1 = _peer_route(h2, w['wq'], w['sk'], tm2)
    x2 = _peer_expert(h2, x1, mod_arr(g2, tm2), cnt, e0, e1, r1, w['pu'], w['pvt'], w['ln2_g'], w['ln2_b'],
                      tm2, 2048, alpha)

    hd = lambda a: a.reshape(n_seq, seq, N_HEADS, HEAD_DIM)
    new = {'moba_k': hd(mk), 'moba_v': hd(mv), 'fox_k': hd(fk), 'fox_v': hd(fv),
           'fox_logf': logf.reshape(n_seq, seq, N_HEADS), 'gla': gla_state,
           's5_re': s5_re.reshape(n_seq, S5_GROUPS, S5_STATE), 's5_im': s5_im.reshape(n_seq, S5_GROUPS, S5_STATE)}
    return x2, new


def _prep_weights(l, depth, d, w_in, gla_w_gate2, gla_b_gate, gla_norm, fox_b_f, s5_lambda_re, s5_lambda_im, s5_log_dt,
                  s5_b_re, s5_b_im, s5_c_re, s5_c_im, s5_d, s5_w_glu, s5_b_glu, w_branch, w_out, ln1_g, ln1_b,
                  peer_w_q, peer_sub_keys, peer_u, peer_v, ln2_g, ln2_b):
    bw = BRANCH_WIDTH
    sizes = (bw,) * 4 + (GLA_GATE_RANK,) + (bw,) * 3 + (bw,) * 3 + (N_HEADS, bw, N_BRANCH * d)
    offs = np.concatenate([[0], np.cumsum(sizes)])
    wl = w_in[l]
    col = lambda i: wl[:, offs[i]:offs[i + 1]]
    bf = lambda a: a.astype(MXU_DTYPE)
    pad_small = LANES - GLA_GATE_RANK - N_HEADS
    w = {}
    w['wa'] = bf(jnp.concatenate([col(i) for i in (0, 1, 2, 3, 5, 6, 7, 8, 9, 10, 12)], axis=1))
    w['ws'] = bf(jnp.concatenate([col(4), col(11), jnp.zeros((d, pad_small), F32)], axis=1))
    w['wg'] = bf(col(13))
    w['gw2'] = bf(jnp.concatenate([gla_w_gate2[l], jnp.zeros((LANES - GLA_GATE_RANK, bw), F32)], axis=0))
    w['gb'] = gla_b_gate[l].reshape(1, bw)
    w['gn'] = gla_norm[l].reshape(1, bw)
    w['fb'] = jnp.concatenate([jnp.zeros((GLA_GATE_RANK,), F32), fox_b_f[l], jnp.zeros((pad_small,), F32)]).reshape(1, LANES)
    a_re, a_im, bb_re, bb_im = _s5_params(s5_lambda_re[l], s5_lambda_im[l], s5_log_dt[l], s5_b_re[l], s5_b_im[l])
    w['a_re'] = a_re.reshape(1, S5_WIDTH_STATE)
    w['a_im'] = a_im.reshape(1, S5_WIDTH_STATE)
    w['bd'] = bf(jnp.concatenate([_block_diag_in(bb_re), _block_diag_in(bb_im)], axis=1))
    w['cd'] = bf(jnp.concatenate([_block_diag_out(s5_c_re[l]), -_block_diag_out(s5_c_im[l])], axis=0))
    w['s5_d'] = s5_d[l].reshape(1, bw)
    w['wglu'] = bf(s5_w_glu[l])
    w['bglu'] = s5_b_glu[l].reshape(1, bw)
    w['wb'] = bf(w_branch[l])
    w['wo'] = bf(w_out[l])
    w['ln1_g'] = ln1_g[l].reshape(1, d)
    w['ln1_b'] = ln1_b[l].reshape(1, d)
    w['wq'] = bf(peer_w_q[l])
    w['sk'] = bf(peer_sub_keys[l])
    w['pu'] = bf(peer_u[l])
    w['pvt'] = bf(peer_v[l]).T
    w['ln2_g'] = ln2_g[l].reshape(1, d)
    w['ln2_b'] = ln2_b[l].reshape(1, d)
    w['alpha'] = (2.0 * depth) ** 0.25
    return w


def kernel(x_prompt, x_sample, c_prompt, c_sample, page_table, cache_moba_k, cache_moba_v, cache_fox_k, cache_fox_v, cache_fox_logf, state_gla, state_s5_re, state_s5_im, w_mod, b_mod, w_in, gla_w_gate2, gla_b_gate, gla_norm, fox_b_f, s5_lambda_re, s5_lambda_im, s5_log_dt, s5_b_re, s5_b_im, s5_c_re, s5_c_im, s5_d, s5_w_glu, s5_b_glu, w_branch, w_out, ln1_g, ln1_b, peer_w_q, peer_sub_keys, peer_u, peer_v, ln2_g, ln2_b):
    bp, seq, d = x_prompt.shape
    bs, dseq, _ = x_sample.shape
    depth = w_in.shape[0]
    n_pool, page = cache_moba_k.shape[1], cache_moba_k.shape[2]
    flat_cache = lambda c: c.reshape(depth, n_pool, page, BRANCH_WIDTH)
    paged = {'paged': True, 'page_table': page_table,
             'moba_k': flat_cache(cache_moba_k), 'moba_v': flat_cache(cache_moba_v),
             'fox_k': flat_cache(cache_fox_k), 'fox_v': flat_cache(cache_fox_v),
             'fox_logf_t': cache_fox_logf.transpose(0, 1, 3, 2)}
    c_all = jnp.concatenate([c_prompt, c_sample], axis=0)
    y_p = x_prompt.reshape(bp * seq, d)
    y_s = x_sample.reshape(bs * dseq, d)
    st_p, st_s = [], []
    for l in range(depth):
        w = _prep_weights(l, depth, d, w_in, gla_w_gate2, gla_b_gate, gla_norm, fox_b_f, s5_lambda_re, s5_lambda_im,
                          s5_log_dt, s5_b_re, s5_b_im, s5_c_re, s5_c_im, s5_d, s5_w_glu, s5_b_glu, w_branch, w_out,
                          ln1_g, ln1_b, peer_w_q, peer_sub_keys, peer_u, peer_v, ln2_g, ln2_b)
        mod = _modulation(c_all, w_mod[l], b_mod[l])
        mods = jnp.split(mod, 6, axis=-1)
        past_p = {'paged': False, 'gla': jnp.zeros((bp, N_HEADS, HEAD_DIM, HEAD_DIM), F32),
                  's5_re': jnp.zeros((bp, S5_GROUPS, S5_STATE), F32), 's5_im': jnp.zeros((bp, S5_GROUPS, S5_STATE), F32)}
        past_s = dict(paged, gla=state_gla[l], s5_re=state_s5_re[l], s5_im=state_s5_im[l])
        y_p, new_p = _layer_group(y_p, [m[:bp] for m in mods], w, bp, seq, past_p, l)
        y_s, new_s = _layer_group(y_s, [m[bp:] for m in mods], w, bs, dseq, past_s, l)
        st_p.append(new_p)
        st_s.append(new_s)

    stk = lambda states, name: jnp.stack([s[name] for s in states])
    names = ('moba_k', 'moba_v', 'fox_k', 'fox_v', 'fox_logf', 'gla', 's5_re', 's5_im')
    return ((y_p.reshape(bp, seq, d), y_s.reshape(bs, dseq, d))
            + tuple(stk(st_p, nm) for nm in names) + tuple(stk(st_s, nm) for nm in names))
```

```python
import functools
import math

import numpy as np
import jax
import jax.numpy as jnp
from jax import lax
from jax.experimental import pallas as pl
from jax.experimental.pallas import tpu as pltpu

F32 = jnp.float32
MXU_DTYPE = jnp.bfloat16

HEAD_DIM = 64
N_HEADS = 4
BRANCH_WIDTH = N_HEADS * HEAD_DIM
N_BRANCH = 4
GLA_GATE_RANK = 16
GLA_TAU = 16.0
GLA_CHUNK = 64
GLA_SUBCHUNK = 16
MOBA_BLOCK = 256
MOBA_TOPK = 3
S5_GROUP = 16
S5_GROUPS = BRANCH_WIDTH // S5_GROUP
S5_STATE = 64
S5_WIDTH_STATE = S5_GROUPS * S5_STATE
PEER_HEADS = 8
PEER_DKEY = 256
PEER_NKEYS = 128
PEER_TOPK = 16
LN_EPS = 1e-5
NEG_BIG = -1e30

V7X_VMEM_LIMIT = 56 * 1024 * 1024
LANES = 128


def _cparams(*sem):
    return pltpu.CompilerParams(dimension_semantics=tuple(sem), vmem_limit_bytes=V7X_VMEM_LIMIT)


def _dot(a, b):
    return jnp.dot(a.astype(MXU_DTYPE), b.astype(MXU_DTYPE), preferred_element_type=F32)


def _dot_nt(a, b):
    return lax.dot_general(a.astype(MXU_DTYPE), b.astype(MXU_DTYPE), (((1,), (1,)), ((), ())),
                           preferred_element_type=F32)


def _dot_tn(a, b):
    return lax.dot_general(a.astype(MXU_DTYPE), b.astype(MXU_DTYPE), (((0,), (0,)), ((), ())),
                           preferred_element_type=F32)


def _split3(x):
    hi = x.astype(jnp.bfloat16)
    r1 = x - hi.astype(F32)
    mid = r1.astype(jnp.bfloat16)
    lo = (r1 - mid.astype(F32)).astype(jnp.bfloat16)
    return hi, mid, lo


def _dot_exact01(x, m01):
    hi, mid, lo = _split3(x)
    m = m01.astype(jnp.bfloat16)
    d = lambda a: jnp.dot(a, m, preferred_element_type=F32)
    return d(hi) + d(mid) + d(lo)


def _dot01_exact(m01, x):
    hi, mid, lo = _split3(x)
    m = m01.astype(jnp.bfloat16)
    d = lambda a: jnp.dot(m, a, preferred_element_type=F32)
    return d(hi) + d(mid) + d(lo)


def _ln(x):
    mu = jnp.mean(x, axis=-1, keepdims=True)
    xc = x - mu
    var = jnp.mean(xc * xc, axis=-1, keepdims=True)
    return xc * lax.rsqrt(var + LN_EPS)


def _sigmoid(x):
    return 1.0 / (1.0 + jnp.exp(-x))


def _log_sigmoid(x):
    return jnp.minimum(x, 0.0) - jnp.log1p(jnp.exp(-jnp.abs(x)))


def _gelu(x):
    return 0.5 * x * (1.0 + lax.erf(x * (1.0 / math.sqrt(2.0))))


def _iota(shape, axis):
    return lax.broadcasted_iota(jnp.int32, shape, axis)


def _mod_kernel(c_ref, w_ref, b_ref, o_ref):
    c = c_ref[...]
    o_ref[...] = _dot(c * _sigmoid(c), w_ref[...]) + b_ref[...]


def _modulation(c_all, w_mod, b_mod):
    n, d = c_all.shape
    width = w_mod.shape[1]
    tn = 1536
    return pl.pallas_call(
        _mod_kernel,
        grid=(width // tn,),
        in_specs=[pl.BlockSpec((n, d), lambda j: (0, 0)),
                  pl.BlockSpec((d, tn), lambda j: (0, j)),
                  pl.BlockSpec((1, tn), lambda j: (0, j))],
        out_specs=pl.BlockSpec((n, tn), lambda j: (0, j)),
        out_shape=jax.ShapeDtypeStruct((n, width), F32),
        compiler_params=_cparams("parallel"),
        name="modulation",
    )(c_all, w_mod, b_mod.reshape(1, width))


N_PROJ = 11
PROJ_SEGMENTS = (0, 1, 2, 3, 5, 6, 7, 8, 9, 10, 12)
SEG_GATE_A, SEG_GATE_F, SEG_BRANCH_GATE = 4, 11, 13


def _in_offsets(d):
    bw = BRANCH_WIDTH
    sizes = (bw,) * 4 + (GLA_GATE_RANK,) + (bw,) * 3 + (bw,) * 3 + (N_HEADS, bw, N_BRANCH * d)
    return [int(o) for o in np.concatenate([[0], np.cumsum(sizes)])]


def _repack_kernel(w_ref, wa_ref, ws_ref, wg_ref):
    rows, d_gate = wg_ref.shape
    offs = _in_offsets(d_gate // N_BRANCH)
    bw = BRANCH_WIDTH
    for i, seg in enumerate(PROJ_SEGMENTS):
        wa_ref[:, i * bw:(i + 1) * bw] = w_ref[:, offs[seg]:offs[seg] + bw].astype(wa_ref.dtype)
    small = jnp.concatenate([w_ref[:, offs[SEG_GATE_A]:offs[SEG_GATE_A + 1]],
                             w_ref[:, offs[SEG_GATE_F]:offs[SEG_GATE_F + 1]],
                             jnp.zeros((rows, LANES - GLA_GATE_RANK - N_HEADS), F32)], axis=1)
    ws_ref[...] = small.astype(ws_ref.dtype)
    wg_ref[...] = w_ref[:, offs[SEG_BRANCH_GATE]:offs[SEG_BRANCH_GATE + 1]].astype(wg_ref.dtype)


def _repack_w_in(w_in, layer):
    _, d, width = w_in.shape
    tr = _tile(d, 256)
    out_w = (N_PROJ * BRANCH_WIDTH, LANES, N_BRANCH * d)
    return pl.pallas_call(
        _repack_kernel,
        grid=(d // tr,),
        in_specs=[pl.BlockSpec((None, tr, width), lambda i: (layer, i, 0))],
        out_specs=[pl.BlockSpec((tr, w), lambda i: (i, 0)) for w in out_w],
        out_shape=[jax.ShapeDtypeStruct((d, w), MXU_DTYPE) for w in out_w],
        compiler_params=_cparams("parallel"),
        name="repack_w_in",
    )(w_in)


def _inproj_kernel(x_ref, sh_ref, sc_ref, wa_ref, ws_ref, gw2_ref, gb_ref, fb_ref,
                   h1_ref, gq_ref, gk_ref, gv_ref, gr_ref, mq_ref, mk_ref, mv_ref, fq_ref, fk_ref, fv_ref,
                   su_ref, la_ref, sm_ref):
    h1 = (_ln(x_ref[...]) * (1.0 + sc_ref[0]) + sh_ref[0]).astype(MXU_DTYPE)
    h1_ref[...] = h1
    outs = (gq_ref, gk_ref, gv_ref, gr_ref, mq_ref, mk_ref, mv_ref, fq_ref, fk_ref, fv_ref, su_ref)
    for i, r in enumerate(outs):
        r[...] = jnp.dot(h1, wa_ref[:, i * BRANCH_WIDTH:(i + 1) * BRANCH_WIDTH],
                         preferred_element_type=F32).astype(r.dtype)
    ps = jnp.dot(h1, ws_ref[...], preferred_element_type=F32)
    la_ref[...] = _log_sigmoid(_dot(ps, gw2_ref[...]) + gb_ref[...]) * (1.0 / GLA_TAU)
    sm_ref[...] = _log_sigmoid(ps + fb_ref[...])


def _inproj(x, sh, sc, wa, ws, gw2, gb, fb, tm):
    n, d = x.shape
    nt = n // tm
    mrow = sh.shape[1]
    per_tile = sh.shape[0] == nt and mrow == tm
    tiles_per_seq = nt // sh.shape[0] if not per_tile else 1
    mod_map = (lambda i: (i, 0, 0)) if per_tile else (lambda i: (i // tiles_per_seq, 0, 0))
    tok = lambda w: pl.BlockSpec((tm, w), lambda i: (i, 0))
    full = lambda a: pl.BlockSpec(a.shape, lambda i: (0,) * a.ndim)
    bw = BRANCH_WIDTH
    odt = [MXU_DTYPE] + [F32] * (N_PROJ + 2)
    owd = [d] + [bw] * N_PROJ + [bw, LANES]
    return pl.pallas_call(
        _inproj_kernel,
        grid=(nt,),
        in_specs=[tok(d), pl.BlockSpec((1, mrow, d), mod_map), pl.BlockSpec((1, mrow, d), mod_map),
                  full(wa), full(ws), full(gw2), full(gb), full(fb)],
        out_specs=[tok(w) for w in owd],
        out_shape=[jax.ShapeDtypeStruct((n, w), t) for w, t in zip(owd, odt)],
        compiler_params=_cparams("parallel"),
        name="inproj",
    )(x, sh, sc, wa, ws, gw2, gb, fb)


def _gla_kernel(q_ref, k_ref, v_ref, la_ref, gr_ref, gn_ref, s0_ref, o_ref, sT_ref, *, seq, chunk, sub):
    n_chunks = seq // chunk
    n_sub = chunk // sub
    scale = HEAD_DIM ** -0.5
    tril = (_iota((chunk, chunk), 1) <= _iota((chunk, chunk), 0))
    gn = gn_ref[...]

    def one_chunk(c, state):
        r0 = c * chunk if isinstance(c, int) else pl.multiple_of(c * chunk, chunk)
        rows = pl.ds(r0, chunk)
        g = la_ref[rows, :]
        bcum = _dot01_exact(tril, g)
        q = q_ref[rows, :] * scale
        k = k_ref[rows, :]
        v = v_ref[rows, :]
        b_end = bcum[chunk - 1:chunk, :]
        qe = q * jnp.exp(bcum)
        kd = k * jnp.exp(b_end - bcum)
        qs, ks = [], []
        for i in range(n_sub):
            base = bcum[i * sub - 1:i * sub, :] if i > 0 else jnp.zeros_like(b_end)
            nk = (i + 1) * sub
            qs.append(q[i * sub:nk, :] * jnp.exp(bcum[i * sub:nk, :] - base))
            ks.append(k[:nk, :] * jnp.exp(base - bcum[:nk, :]))
        new_state, outs = [], []
        for h in range(N_HEADS):
            sl = slice(h * HEAD_DIM, (h + 1) * HEAD_DIM)
            s_t = state[h]
            o_h = _dot_nt(qe[:, sl], s_t)
            att_rows = []
            for i in range(n_sub):
                nk = (i + 1) * sub
                a = _dot_nt(qs[i][:, sl], ks[i][:, sl])
                causal = _iota((sub, nk), 1) <= _iota((sub, nk), 0) + i * sub
                a = jnp.where(causal, a, 0.0)
                if nk < chunk:
                    a = jnp.concatenate([a, jnp.zeros((sub, chunk - nk), F32)], axis=1)
                att_rows.append(a)
            att = att_rows[0] if n_sub == 1 else jnp.concatenate(att_rows, axis=0)
            o_h = o_h + _dot(att, v[:, sl])
            new_state.append(s_t * jnp.exp(b_end[:, sl]) + _dot_tn(v[:, sl], kd[:, sl]))
            o_h = o_h * lax.rsqrt(jnp.mean(o_h * o_h, axis=-1, keepdims=True) + LN_EPS)
            outs.append(o_h)
        o = jnp.concatenate(outs, axis=1)
        gr = gr_ref[rows, :]
        o_ref[rows, :] = (o * gn * (gr * _sigmoid(gr))).astype(o_ref.dtype)
        return tuple(new_state)

    state0 = tuple(s0_ref[0, h] for h in range(N_HEADS))
    if n_chunks == 1:
        state = one_chunk(0, state0)
    else:
        state = lax.fori_loop(0, n_chunks, one_chunk, state0)
    for h in range(N_HEADS):
        sT_ref[0, h] = state[h]


def _gla(gq, gk, gv, la, gr, gn, s0_t, n_seq, seq):
    chunk = math.gcd(seq, GLA_CHUNK)
    sub = math.gcd(chunk, GLA_SUBCHUNK)
    tok = pl.BlockSpec((seq, BRANCH_WIDTH), lambda b: (b, 0))
    st = pl.BlockSpec((1, N_HEADS, HEAD_DIM, HEAD_DIM), lambda b: (b, 0, 0, 0))
    return pl.pallas_call(
        functools.partial(_gla_kernel, seq=seq, chunk=chunk, sub=sub),
        grid=(n_seq,),
        in_specs=[tok, tok, tok, tok, tok, pl.BlockSpec((1, BRANCH_WIDTH), lambda b: (0, 0)), st],
        out_specs=[tok, st],
        out_shape=[jax.ShapeDtypeStruct((n_seq * seq, BRANCH_WIDTH), F32),
                   jax.ShapeDtypeStruct((n_seq, N_HEADS, HEAD_DIM, HEAD_DIM), F32)],
        compiler_params=_cparams("parallel"),
        name="gla",
    )(gq, gk, gv, la, gr, gn, s0_t)


def _cum_kernel(lf_ref, o_ref):
    rows, length = lf_ref.shape
    triu = _iota((LANES, LANES), 0) <= _iota((LANES, LANES), 1)
    carry = jnp.zeros((rows, 1), F32)
    for c in range(length // LANES):
        cs = _dot_exact01(lf_ref[:, c * LANES:(c + 1) * LANES], triu) + carry
        o_ref[:, c * LANES:(c + 1) * LANES] = cs
        carry = cs[:, LANES - 1:LANES]


def _fox_cumsum(lf_rows):
    return pl.pallas_call(
        _cum_kernel,
        out_shape=jax.ShapeDtypeStruct(lf_rows.shape, F32),
        compiler_params=pltpu.CompilerParams(vmem_limit_bytes=V7X_VMEM_LIMIT),
        name="fox_cumsum",
    )(lf_rows)


def _alibi_slope(h):
    return 2.0 ** (-8.0 * (h + 1) / N_HEADS)


def _attn_prompt_kernel(q_ref, k_ref, v_ref, *rest, seq, moba):
    if moba:
        o_ref, kh_ref, vh_ref, km_ref, mask_ref = rest
    else:
        cum_ref, o_ref, kh_ref, vh_ref = rest
    blk = MOBA_BLOCK
    nb = seq // blk
    i = pl.program_id(1)

    @pl.when(i == 0)
    def _():
        for h in range(N_HEADS):
            sl = slice(h * HEAD_DIM, (h + 1) * HEAD_DIM)
            kf = k_ref[:, sl]
            kh_ref[h] = kf.astype(MXU_DTYPE)
            vh_ref[h] = v_ref[:, sl].astype(MXU_DTYPE)
            if moba:
                km = jnp.sum(kf.reshape(nb, blk, HEAD_DIM), axis=1) * (1.0 / blk)
                pad = km_ref.shape[1] - nb
                km_ref[h] = jnp.concatenate([km, jnp.zeros((pad, HEAD_DIM), F32)], axis=0) if pad else km

    nbp = km_ref.shape[1] if moba else 0
    q_pos = i * blk + _iota((blk, blk), 0)
    outs = []
    for h in range(N_HEADS):
        q_h = q_ref[:, h * HEAD_DIM:(h + 1) * HEAD_DIM]
        if moba:
            bs = _dot_nt(km_ref[h], q_h)
            jrow = _iota((nbp, blk), 0)
            bs = jnp.where(jrow < i, bs, -jnp.inf)
            rank = jnp.zeros((nbp, blk), F32)
            for j2 in range(nb):
                row = bs[j2:j2 + 1, :]
                beats = (row > bs) | ((row == bs) & (j2 < jrow))
                rank = rank + beats.astype(F32)
            sel = ((jrow < i) & (rank < MOBA_TOPK)) | (jrow == i)
            expand = (_iota((nbp, seq), 1) // blk == _iota((nbp, seq), 0))
            mask_ref[...] = _dot(sel.astype(F32).T, expand.astype(F32))
            slope = _alibi_slope(h)

        def body(j, carry):
            m, l, acc = carry
            c0 = pl.multiple_of(j * blk, blk)
            s = _dot_nt(q_h, kh_ref[h, pl.ds(c0, blk), :]) * (HEAD_DIM ** -0.5)
            dist = (q_pos - (j * blk + _iota((blk, blk), 1))).astype(F32)
            if moba:
                s = s - slope * dist
                ok = (mask_ref[:, pl.ds(c0, blk)] > 0.5) & (dist >= 0)
            else:
                s = s - cum_ref[0, h:h + 1, pl.ds(c0, blk)]
                ok = dist >= 0
            s = jnp.where(ok, s, NEG_BIG)
            m_new = jnp.maximum(m, jnp.max(s, axis=-1, keepdims=True))
            alpha = jnp.exp(m - m_new)
            p = jnp.exp(s - m_new)
            l = alpha * l + jnp.sum(p, axis=-1, keepdims=True)
            acc = alpha * acc + _dot(p, vh_ref[h, pl.ds(c0, blk), :])
            return m_new, l, acc

        init = (jnp.full((blk, 1), NEG_BIG, F32), jnp.zeros((blk, 1), F32), jnp.zeros((blk, HEAD_DIM), F32))
        m, l, acc = lax.fori_loop(0, i + 1, body, init)
        outs.append(acc / l)
    o_ref[...] = jnp.concatenate(outs, axis=1).astype(o_ref.dtype)


def _attn_prompt(q, k, v, cum, n_seq, seq, moba):
    blk = MOBA_BLOCK
    nq = seq // blk
    nbp = -(-nq // 8) * 8
    tile = pl.BlockSpec((blk, BRANCH_WIDTH), lambda b, i: (b * nq + i, 0))
    whole = pl.BlockSpec((seq, BRANCH_WIDTH), lambda b, i: (b, 0))
    in_specs = [tile, whole, whole]
    args = [q, k, v]
    scratch = [pltpu.VMEM((N_HEADS, seq, HEAD_DIM), MXU_DTYPE), pltpu.VMEM((N_HEADS, seq, HEAD_DIM), MXU_DTYPE)]
    if moba:
        scratch += [pltpu.VMEM((N_HEADS, nbp, HEAD_DIM), F32), pltpu.VMEM((blk, seq), F32)]
    else:
        in_specs.append(pl.BlockSpec((1, N_HEADS, seq), lambda b, i: (b, 0, 0)))
        args.append(cum)
    return pl.pallas_call(
        functools.partial(_attn_prompt_kernel, seq=seq, moba=moba),
        grid=(n_seq, nq),
        in_specs=in_specs,
        out_specs=tile,
        out_shape=jax.ShapeDtypeStruct((n_seq * seq, BRANCH_WIDTH), F32),
        scratch_shapes=scratch,
        compiler_params=_cparams("parallel", "arbitrary"),
        name="moba_prompt" if moba else "fox_prompt",
    )(*args)


def _block_diag_queries(q):
    t = q.shape[0]
    col_head = _iota((t, BRANCH_WIDTH), 1) // HEAD_DIM
    return jnp.concatenate([jnp.where(col_head == h, q, jnp.zeros_like(q)) for h in range(N_HEADS)], axis=0)


def _head_diag(acc, t):
    return jnp.concatenate([acc[h * t:(h + 1) * t, h * HEAD_DIM:(h + 1) * HEAD_DIM] for h in range(N_HEADS)], axis=1)


def _attn_sample_kernel(pt_ref, mq_ref, mkn_ref, mvn_ref, fq_ref, fkn_ref, fvn_ref, lfn_ref, *refs,
                        t_new, n_pages, page, past_len):
    del pt_ref
    mk, mv, fk, fv, lfc = (refs[i * n_pages:(i + 1) * n_pages] for i in range(5))
    ob_ref, oc_ref = refs[5 * n_pages:]
    rows = N_HEADS * t_new
    scale = HEAD_DIM ** -0.5
    row_head = _iota((rows, 1), 0) // t_new
    row_t = _iota((rows, 1), 0) % t_new
    lane_n = _iota((rows, t_new), 1)
    causal = lane_n <= row_t
    cast = lambda r: r[...].astype(MXU_DTYPE)

    def softmax_pv(s_past, s_new, v_pages, v_new):
        m = jnp.maximum(jnp.max(s_past, axis=-1, keepdims=True), jnp.max(s_new, axis=-1, keepdims=True))
        p_past = jnp.exp(s_past - m)
        p_new = jnp.exp(s_new - m)
        l = jnp.sum(p_past, axis=-1, keepdims=True) + jnp.sum(p_new, axis=-1, keepdims=True)
        acc = _dot(p_new, v_new)
        for pg in range(n_pages):
            acc = acc + _dot_nt(p_past[:, pg * page:(pg + 1) * page], cast(v_pages[pg]))
        return _head_diag(acc / l, t_new)

    qm = _block_diag_queries(mq_ref[...]).astype(MXU_DTYPE)
    pages_per_block = MOBA_BLOCK // page
    n_blocks = n_pages // pages_per_block
    kmean_cols = []
    for b in range(n_blocks):
        blk = jnp.concatenate([mk[b * pages_per_block + i][...] for i in range(pages_per_block)], axis=1)
        kmean_cols.append(jnp.sum(blk, axis=1, keepdims=True) * (1.0 / MOBA_BLOCK))
    bs = _dot(qm, jnp.concatenate(kmean_cols, axis=1))
    jl = _iota((rows, n_blocks), 1)
    rank = jnp.zeros((rows, n_blocks), F32)
    for j2 in range(n_blocks):
        col = bs[:, j2:j2 + 1]
        rank = rank + ((col > bs) | ((col == bs) & (j2 < jl))).astype(F32)
    sel = (rank < MOBA_TOPK).astype(F32)
    sel_keys = jnp.concatenate(
        [jnp.broadcast_to(sel[:, b:b + 1], (rows, MOBA_BLOCK)) for b in range(n_blocks)], axis=1) > 0.5
    slope = jnp.zeros((rows, 1), F32)
    for h in range(N_HEADS):
        slope = jnp.where(row_head == h, _alibi_slope(h), slope)
    s = jnp.concatenate([jnp.dot(qm, cast(mk[pg]), preferred_element_type=F32) for pg in range(n_pages)], axis=1)
    dist = (past_len + row_t - _iota((rows, past_len), 1)).astype(F32)
    s = jnp.where(sel_keys, s * scale - slope * dist, NEG_BIG)
    s_n = _dot_nt(qm, mkn_ref[...]) * scale - slope * (row_t - lane_n).astype(F32)
    s_n = jnp.where(causal, s_n, NEG_BIG)
    ob_ref[...] = softmax_pv(s, s_n, mv, mvn_ref[...])

    qf = _block_diag_queries(fq_ref[...]).astype(MXU_DTYPE)
    lf_all = jnp.concatenate([lfc[pg][...] for pg in range(n_pages)], axis=0)
    triu = _iota((page, page), 0) <= _iota((page, page), 1)
    local = _dot_exact01(lf_all, triu)
    nr = n_pages * N_HEADS
    ri, ci = _iota((nr, nr), 0), _iota((nr, nr), 1)
    earlier_page = (ci % N_HEADS == ri % N_HEADS) & (ci // N_HEADS < ri // N_HEADS)
    cum = local + _dot01_exact(earlier_page, local[:, page - 1:page])
    bias = jnp.concatenate(
        [jnp.broadcast_to(cum[pg * N_HEADS:(pg + 1) * N_HEADS, None, :], (N_HEADS, t_new, page)).reshape(rows, page)
         for pg in range(n_pages)], axis=1)
    s = jnp.concatenate([jnp.dot(qf, cast(fk[pg]), preferred_element_type=F32) for pg in range(n_pages)], axis=1)
    s = s * scale - bias
    lfn = lfn_ref[...]
    cols, c = [], cum[nr - N_HEADS:nr, page - 1:page]
    for t in range(t_new):
        c = c + lfn[:, t:t + 1]
        cols.append(c)
    cum_n = jnp.concatenate(cols, axis=1)
    bias_n = jnp.broadcast_to(cum_n[:, None, :], (N_HEADS, t_new, t_new)).reshape(rows, t_new)
    s_n = jnp.where(causal, _dot_nt(qf, fkn_ref[...]) * scale - bias_n, NEG_BIG)
    oc_ref[...] = softmax_pv(s, s_n, fv, fvn_ref[...])


def _attn_sample(page_table, mq, mkn, mvn, fq, fkn, fvn, lfn, mkc, mvc, fkc, fvc, lfc, layer):
    n_seq, t_new, _ = mq.shape
    n_pages = page_table.shape[1]
    page = mkc.shape[3]
    past_len = n_pages * page
    assert past_len % MOBA_BLOCK == 0 and MOBA_BLOCK % page == 0 and t_new <= MOBA_BLOCK
    new = pl.BlockSpec((None, t_new, BRANCH_WIDTH), lambda b, pt: (b, 0, 0))

    def pages(width):
        return [pl.BlockSpec((None, None, width, page), lambda b, pt, pg=pg: (layer, pt[b, pg], 0, 0))
                for pg in range(n_pages)]

    grid_spec = pltpu.PrefetchScalarGridSpec(
        num_scalar_prefetch=1,
        grid=(n_seq,),
        in_specs=[new, new, new, new, new, new, pl.BlockSpec((None, N_HEADS, t_new), lambda b, pt: (b, 0, 0))]
                 + 4 * pages(BRANCH_WIDTH) + pages(N_HEADS),
        out_specs=[new, new],
    )
    return pl.pallas_call(
        functools.partial(_attn_sample_kernel, t_new=t_new, n_pages=n_pages, page=page, past_len=past_len),
        grid_spec=grid_spec,
        out_shape=[jax.ShapeDtypeStruct((n_seq, t_new, BRANCH_WIDTH), F32)] * 2,
        compiler_params=_cparams("parallel"),
        name="attn_sample",
    )(page_table, mq, mkn, mvn, fq, fkn, fvn, lfn,
      *([mkc] * n_pages + [mvc] * n_pages + [fkc] * n_pages + [fvc] * n_pages + [lfc] * n_pages))


def _s5_param_kernel(lre_ref, lim_ref, ldt_ref, bre_ref, bim_ref, are_ref, aim_ref, bbre_ref, bbim_ref):
    lam_re = lre_ref[...]
    lam_im = lim_ref[...]
    dt = jnp.exp(ldt_ref[...])
    mag = jnp.exp(lam_re * dt)
    ab_re = mag * jnp.cos(lam_im * dt)
    ab_im = mag * jnp.sin(lam_im * dt)
    den = lam_re * lam_re + lam_im * lam_im
    z_re = ((ab_re - 1.0) * lam_re + ab_im * lam_im) / den
    z_im = (ab_im * lam_re - (ab_re - 1.0) * lam_im) / den
    are_ref[...] = ab_re
    aim_ref[...] = ab_im
    b_re = bre_ref[...]
    b_im = bim_ref[...]
    bbre_ref[...] = z_re[:, None, :] * b_re - z_im[:, None, :] * b_im
    bbim_ref[...] = z_re[:, None, :] * b_im + z_im[:, None, :] * b_re


def _s5_params(lam_re, lam_im, log_dt, b_re, b_im):
    g, p = lam_re.shape
    i = b_re.shape[-1]
    outs = pl.pallas_call(
        _s5_param_kernel,
        out_shape=[jax.ShapeDtypeStruct((g, p), F32)] * 2 + [jax.ShapeDtypeStruct((g, i, p), F32)] * 2,
        name="s5_params",
    )(lam_re, lam_im, log_dt.reshape(g, 1), b_re.transpose(0, 2, 1), b_im.transpose(0, 2, 1))
    return outs


def _s5_kernel(u_ref, bd_ref, cd_ref, are_ref, aim_ref, d_ref, wg_ref, bg_ref, h0r_ref, h0i_ref,
               o_ref, hr_ref, hi_ref, bu_ref, hs_ref, *, steps, nb):
    ns = S5_WIDTH_STATE

    @pl.when(pl.program_id(0) == 0)
    def _():
        hr_ref[...] = h0r_ref[...]
        hi_ref[...] = h0i_ref[...]

    u = u_ref[...].reshape(steps * nb, BRANCH_WIDTH)
    bu_ref[...] = _dot(u, bd_ref[...])
    a_re = jnp.broadcast_to(are_ref[...], (nb, ns))
    a_im = jnp.broadcast_to(aim_ref[...], (nb, ns))

    def step(t, carry):
        h_re, h_im = carry
        rows = pl.ds(pl.multiple_of(t * nb, nb), nb)
        n_re = a_re * h_re - a_im * h_im + bu_ref[rows, 0:ns]
        n_im = a_re * h_im + a_im * h_re + bu_ref[rows, ns:2 * ns]
        hs_ref[rows, 0:ns] = n_re
        hs_ref[rows, ns:2 * ns] = n_im
        return n_re, n_im

    h_re, h_im = lax.fori_loop(0, steps, step, (hr_ref[...], hi_ref[...]))
    hr_ref[...] = h_re
    hi_ref[...] = h_im
    y = _dot(hs_ref[...], cd_ref[...]) + d_ref[...] * u
    z = _gelu(y)
    o = z * _sigmoid(_dot(z, wg_ref[...]) + bg_ref[...])
    o_ref[...] = o.reshape(steps, nb, BRANCH_WIDTH).astype(o_ref.dtype)


def _s5(u_tb, bd, cd, a_re, a_im, d, wg, bg, h0r, h0i, steps):
    t_len, nb, _ = u_tb.shape
    full = lambda a: pl.BlockSpec(a.shape, lambda i: (0,) * a.ndim)
    blk = pl.BlockSpec((steps, nb, BRANCH_WIDTH), lambda i: (i, 0, 0))
    st = pl.BlockSpec((nb, S5_WIDTH_STATE), lambda i: (0, 0))
    return pl.pallas_call(
        functools.partial(_s5_kernel, steps=steps, nb=nb),
        grid=(t_len // steps,),
        in_specs=[blk, full(bd), full(cd), full(a_re), full(a_im), full(d), full(wg), full(bg), st, st],
        out_specs=[blk, st, st],
        out_shape=[jax.ShapeDtypeStruct((t_len, nb, BRANCH_WIDTH), F32),
                   jax.ShapeDtypeStruct((nb, S5_WIDTH_STATE), F32), jax.ShapeDtypeStruct((nb, S5_WIDTH_STATE), F32)],
        scratch_shapes=[pltpu.VMEM((steps * nb, 2 * S5_WIDTH_STATE), F32),
                        pltpu.VMEM((steps * nb, 2 * S5_WIDTH_STATE), F32)],
        compiler_params=_cparams("arbitrary"),
        name="s5",
    )(u_tb, bd, cd, a_re, a_im, d, wg, bg, h0r, h0i)


def _merge_kernel(x_ref, h1_ref, oa_ref, ob_ref, oc_ref, od_ref, g1_ref, sh2_ref, sc2_ref,
                  wg_ref, wb_ref, wo_ref, lg_ref, lb_ref, x1_ref, h2_ref, *, alpha):
    d = x_ref.shape[1]
    h1 = h1_ref[...]
    merged = None
    for n, br in enumerate((oa_ref, ob_ref, oc_ref, od_ref)):
        gate = _sigmoid(jnp.dot(h1, wg_ref[:, n * d:(n + 1) * d], preferred_element_type=F32))
        up = _dot(br[...], wb_ref[n])
        merged = gate * up if merged is None else merged + gate * up
    mix = _dot(merged, wo_ref[...])
    x1 = _ln(alpha * x_ref[...] + (1.0 + g1_ref[0]) * mix) * lg_ref[...] + lb_ref[...]
    x1_ref[...] = x1
    h2_ref[...] = (_ln(x1) * (1.0 + sc2_ref[0]) + sh2_ref[0]).astype(h2_ref.dtype)


def _merge(x, h1, oa, ob, oc, od, g1, sh2, sc2, wg, wb, wo, lg, lb, tm, alpha):
    n, d = x.shape
    nt = n // tm
    mrow = g1.shape[1]
    per_tile = g1.shape[0] == nt and mrow == tm
    tiles_per_seq = nt // g1.shape[0] if not per_tile else 1
    mod_map = (lambda i: (i, 0, 0)) if per_tile else (lambda i: (i // tiles_per_seq, 0, 0))
    mod = pl.BlockSpec((1, mrow, d), mod_map)
    tok = lambda w: pl.BlockSpec((tm, w), lambda i: (i, 0))
    full = lambda a: pl.BlockSpec(a.shape, lambda i: (0,) * a.ndim)
    bw = BRANCH_WIDTH
    return pl.pallas_call(
        functools.partial(_merge_kernel, alpha=alpha),
        grid=(nt,),
        in_specs=[tok(d), tok(d), tok(bw), tok(bw), tok(bw), tok(bw), mod, mod, mod,
                  full(wg), full(wb), full(wo), full(lg), full(lb)],
        out_specs=[tok(d), tok(d)],
        out_shape=[jax.ShapeDtypeStruct((n, d), F32), jax.ShapeDtypeStruct((n, d), MXU_DTYPE)],
        compiler_params=_cparams("parallel"),
        name="merge",
    )(x, h1, oa, ob, oc, od, g1, sh2, sc2, wg, wb, wo, lg, lb)


def _topk_rows(s, k, rid=None):
    if rid is None:
        rid = _iota(s.shape, 0)
    rank = jnp.full(s.shape, float(k), F32)
    vals = []
    for r in range(k):
        m = jnp.max(s, axis=0, keepdims=True)
        idx = jnp.min(jnp.where(s == m, rid, jnp.iinfo(jnp.int32).max), axis=0, keepdims=True)
        hit = rid == idx
        rank = jnp.where(hit, float(r), rank)
        s = jnp.where(hit, -jnp.inf, s)
        vals.append(m)
    return jnp.concatenate(vals, axis=0), rank


def _peer_route_kernel(h_ref, wq_ref, sk_ref, cnt_ref, e0_ref, e1_ref, r1_ref, q_ref):
    kk = PEER_TOPK
    half = PEER_DKEY // 2
    tn = h_ref.shape[0]
    q_ref[...] = jnp.dot(h_ref[...], wq_ref[...], preferred_element_type=F32)

    @pl.loop(0, PEER_HEADS)
    def _(h):
        s, sv, rk = [], [], []
        for p in range(2):
            c0 = pl.multiple_of(h * PEER_DKEY + p * half, half)
            sp = _dot_nt(sk_ref[p], q_ref[:, pl.ds(c0, half)])
            v, r = _topk_rows(sp, kk)
            s.append(sp)
            sv.append(v)
            rk.append(r)
        grid = sv[0][:, None, :] + sv[1][None, :, :]
        sub = 8
        cand = jnp.concatenate([grid[k0, 0:sub, :] for k0 in range(kk)] + [grid[0, sub:kk, :]], axis=0)
        n_cand = cand.shape[0]
        grp, loc = _iota((n_cand, tn), 0) // sub, _iota((n_cand, tn), 0) % sub
        flat_id = jnp.where(grp < kk, grp * kk + loc, sub + loc)
        _, crank = _topk_rows(cand, kk, flat_id)
        taken = crank < float(kk)
        per_grp = jnp.sum(taken.astype(F32).reshape(n_cand // sub, sub, tn), axis=1)
        cnt = per_grp[0:kk, :] + jnp.where(_iota((kk, tn), 0) == 0, per_grp[kk:kk + 1, :], 0.0)
        top = sv[0][0:1, :] + sv[1][0:1, :]
        z = jnp.sum(jnp.where(taken, jnp.exp(cand - top), 0.0), axis=0, keepdims=True)
        cnt_dense = jnp.zeros((PEER_NKEYS, tn), F32)
        for k0 in range(kk):
            cnt_dense = jnp.where(rk[0] == float(k0), cnt[k0:k0 + 1, :], cnt_dense)
        cnt_ref[h] = cnt_dense
        e0_ref[h] = jnp.exp(s[0] - sv[0][0:1, :])
        e1_ref[h] = jnp.exp(s[1] - sv[1][0:1, :]) / z
        r1_ref[h] = rk[1]


def _peer_route(h2, wq, sk, tn):
    n, d = h2.shape
    full = lambda a: pl.BlockSpec(a.shape, lambda i: (0,) * a.ndim)
    out = pl.BlockSpec((PEER_HEADS, PEER_NKEYS, tn), lambda i: (0, 0, i))
    return pl.pallas_call(
        _peer_route_kernel,
        grid=(n // tn,),
        in_specs=[pl.BlockSpec((tn, d), lambda i: (i, 0)), full(wq), full(sk)],
        out_specs=[out] * 4,
        out_shape=[jax.ShapeDtypeStruct((PEER_HEADS, PEER_NKEYS, n), F32)] * 4,
        scratch_shapes=[pltpu.VMEM((tn, PEER_HEADS * PEER_DKEY), F32)],
        compiler_params=_cparams("parallel"),
        name="peer_route",
    )(h2, wq, sk)


def _peer_expert_kernel(h_ref, x1_ref, g2_ref, cnt_ref, e0_ref, e1_ref, r1_ref, u_ref, vt_ref, lg_ref, lb_ref,
                        o_ref, acc_ref, g_ref, a_ref, *, alpha, i0_per_tile):
    e = pl.program_id(1)
    tn = h_ref.shape[0]

    @pl.when(e == 0)
    def _():
        acc_ref[...] = jnp.zeros(acc_ref.shape, F32)

    a_ref[...] = _dot_nt(u_ref[...], h_ref[...])

    @pl.loop(0, i0_per_tile)
    def _(il):
        rows = pl.ds(pl.multiple_of(il * PEER_NKEYS, PEER_NKEYS), PEER_NKEYS)
        row = pl.ds(e * i0_per_tile + il, 1)
        act = _gelu(a_ref[rows, :])
        w = jnp.zeros((PEER_NKEYS, tn), F32)
        for h in range(PEER_HEADS):
            cnt = cnt_ref[h, row, :]
            w = w + jnp.where(r1_ref[h] < cnt, e1_ref[h], 0.0) * e0_ref[h, row, :]
        g_ref[rows, :] = (w * act).astype(g_ref.dtype)

    acc_ref[...] += jnp.dot(vt_ref[...], g_ref[...], preferred_element_type=F32)

    @pl.when(e == pl.num_programs(1) - 1)
    def _():
        ff = acc_ref[...].T
        o_ref[...] = _ln(alpha * x1_ref[...] + (1.0 + g2_ref[0]) * ff) * lg_ref[...] + lb_ref[...]


def _peer_expert(h2, x1, g2, cnt, e0, e1, r1, u, vt, lg, lb, tn, te, alpha):
    n, d = h2.shape
    nt = n // tn
    n_exp = u.shape[0]
    mrow = g2.shape[1]
    per_tile = g2.shape[0] == nt and mrow == tn
    tiles_per_seq = nt // g2.shape[0] if not per_tile else 1
    mod_map = (lambda i, e: (i, 0, 0)) if per_tile else (lambda i, e: (i // tiles_per_seq, 0, 0))
    tok = pl.BlockSpec((tn, d), lambda i, e: (i, 0))
    route = pl.BlockSpec((PEER_HEADS, PEER_NKEYS, tn), lambda i, e: (0, 0, i))
    vec = pl.BlockSpec((1, d), lambda i, e: (0, 0))
    return pl.pallas_call(
        functools.partial(_peer_expert_kernel, alpha=alpha, i0_per_tile=te // PEER_NKEYS),
        grid=(nt, n_exp // te),
        in_specs=[tok, tok, pl.BlockSpec((1, mrow, d), mod_map), route, route, route, route,
                  pl.BlockSpec((te, d), lambda i, e: (e, 0)), pl.BlockSpec((d, te), lambda i, e: (0, e)), vec, vec],
        out_specs=tok,
        out_shape=jax.ShapeDtypeStruct((n, d), F32),
        scratch_shapes=[pltpu.VMEM((d, tn), F32), pltpu.VMEM((te, tn), MXU_DTYPE), pltpu.VMEM((te, tn), F32)],
        compiler_params=_cparams("parallel", "arbitrary"),
        name="peer_expert",
    )(h2, x1, g2, cnt, e0, e1, r1, u, vt, lg, lb)


def _block_diag_in(bb_t):
    g, i, p = bb_t.shape
    return jnp.einsum('gip,gh->gihp', bb_t, jnp.eye(g, dtype=bb_t.dtype)).reshape(g * i, g * p)


def _block_diag_out(c):
    g, i, p = c.shape
    return jnp.einsum('gip,gh->gphi', c, jnp.eye(g, dtype=c.dtype)).reshape(g * p, g * i)


def _tile(n, pref):
    return pref if n % pref == 0 else n


def _layer_group(x, mods, w, n_seq, seq, past, layer):
    n, d = x.shape
    bw = BRANCH_WIDTH
    alpha = w['alpha']
    tm = _tile(n, 512)
    tm2 = _tile(n, 256)

    def mod_arr(m, t):
        if seq % t == 0:
            return m.reshape(n_seq, 1, d)
        return jnp.repeat(m, seq, axis=0).reshape(n // t, t, d)

    sh1, sc1, g1, sh2, sc2, g2 = mods
    (h1, gq, gk, gv, gr, mq, mk, mv, fq, fk, fv, su, la, sm) = _inproj(
        x, mod_arr(sh1, tm), mod_arr(sc1, tm), w['wa'], w['ws'], w['gw2'], w['gb'], w['fb'], tm)
    logf = sm[:, GLA_GATE_RANK:GLA_GATE_RANK + N_HEADS]

    s0_t = past['gla'].transpose(0, 1, 3, 2)
    oa, gla_t = _gla(gq, gk, gv, la, gr, w['gn'], s0_t, n_seq, seq)
    gla_state = gla_t.transpose(0, 1, 3, 2)

    if past['paged']:
        r3 = lambda a: a.reshape(n_seq, seq, bw)
        lfn = logf.reshape(n_seq, seq, N_HEADS).transpose(0, 2, 1)
        ob, oc = _attn_sample(past['page_table'], r3(mq), r3(mk), r3(mv), r3(fq), r3(fk), r3(fv), lfn,
                              past['moba_k'], past['moba_v'], past['fox_k'], past['fox_v'], past['fox_logf_t'], layer)
        ob = ob.reshape(n, bw)
        oc = oc.reshape(n, bw)
    else:
        lf_rows = logf.reshape(n_seq, seq, N_HEADS).transpose(0, 2, 1).reshape(n_seq * N_HEADS, seq)
        cum = _fox_cumsum(lf_rows).reshape(n_seq, N_HEADS, seq)
        ob = _attn_prompt(mq, mk, mv, None, n_seq, seq, True)
        oc = _attn_prompt(fq, fk, fv, cum, n_seq, seq, False)

    steps = math.gcd(seq, 128)
    u_tb = su.reshape(n_seq, seq, bw).transpose(1, 0, 2)
    od_tb, s5_re, s5_im = _s5(u_tb, w['bd'], w['cd'], w['a_re'], w['a_im'], w['s5_d'], w['wglu'], w['bglu'],
                              past['s5_re'].reshape(n_seq, S5_WIDTH_STATE), past['s5_im'].reshape(n_seq, S5_WIDTH_STATE),
                              steps)
    od = od_tb.transpose(1, 0, 2).reshape(n, bw)

    x1, h2 = _merge(x, h1, oa, ob, oc, od, mod_arr(g1, tm2), mod_arr(sh2, tm2), mod_arr(sc2, tm2),
                    w['wg'], w['wb'], w['wo'], w['ln1_g'], w['ln1_b'], tm2, alpha)

    cnt, e0, e1, r1 = _peer_route(h2, w['wq'], w['sk'], tm2)
    x2 = _peer_expert(h2, x1, mod_arr(g2, tm2), cnt, e0, e1, r1, w['pu'], w['pvt'], w['ln2_g'], w['ln2_b'],
                      tm2, 2048, alpha)

    hd = lambda a: a.reshape(n_seq, seq, N_HEADS, HEAD_DIM)
    new = {'moba_k': hd(mk), 'moba_v': hd(mv), 'fox_k': hd(fk), 'fox_v': hd(fv),
           'fox_logf': logf.reshape(n_seq, seq, N_HEADS), 'gla': gla_state,
           's5_re': s5_re.reshape(n_seq, S5_GROUPS, S5_STATE), 's5_im': s5_im.reshape(n_seq, S5_GROUPS, S5_STATE)}
    return x2, new


def _prep_weights(l, depth, d, w_in, gla_w_gate2, gla_b_gate, gla_norm, fox_b_f, s5_lambda_re, s5_lambda_im, s5_log_dt,
                  s5_b_re, s5_b_im, s5_c_re, s5_c_im, s5_d, s5_w_glu, s5_b_glu, w_branch, w_out, ln1_g, ln1_b,
                  peer_w_q, peer_sub_keys, peer_u, peer_v, ln2_g, ln2_b):
    bw = BRANCH_WIDTH
    bf = lambda a: a.astype(MXU_DTYPE)
    pad_small = LANES - GLA_GATE_RANK - N_HEADS
    w = {}
    w['wa'], w['ws'], w['wg'] = _repack_w_in(w_in, l)
    w['gw2'] = bf(jnp.concatenate([gla_w_gate2[l], jnp.zeros((LANES - GLA_GATE_RANK, bw), F32)], axis=0))
    w['gb'] = gla_b_gate[l].reshape(1, bw)
    w['gn'] = gla_norm[l].reshape(1, bw)
    w['fb'] = jnp.concatenate([jnp.zeros((GLA_GATE_RANK,), F32), fox_b_f[l], jnp.zeros((pad_small,), F32)]).reshape(1, LANES)
    a_re, a_im, bb_re, bb_im = _s5_params(s5_lambda_re[l], s5_lambda_im[l], s5_log_dt[l], s5_b_re[l], s5_b_im[l])
    w['a_re'] = a_re.reshape(1, S5_WIDTH_STATE)
    w['a_im'] = a_im.reshape(1, S5_WIDTH_STATE)
    w['bd'] = bf(jnp.concatenate([_block_diag_in(bb_re), _block_diag_in(bb_im)], axis=1))
    w['cd'] = bf(jnp.concatenate([_block_diag_out(s5_c_re[l]), -_block_diag_out(s5_c_im[l])], axis=0))
    w['s5_d'] = s5_d[l].reshape(1, bw)
    w['wglu'] = bf(s5_w_glu[l])
    w['bglu'] = s5_b_glu[l].reshape(1, bw)
    w['wb'] = bf(w_branch[l])
    w['wo'] = bf(w_out[l])
    w['ln1_g'] = ln1_g[l].reshape(1, d)
    w['ln1_b'] = ln1_b[l].reshape(1, d)
    w['wq'] = bf(peer_w_q[l])
    w['sk'] = bf(peer_sub_keys[l])
    w['pu'] = bf(peer_u[l])
    w['pvt'] = bf(peer_v[l]).T
    w['ln2_g'] = ln2_g[l].reshape(1, d)
    w['ln2_b'] = ln2_b[l].reshape(1, d)
    w['alpha'] = (2.0 * depth) ** 0.25
    return w


def kernel(x_prompt, x_sample, c_prompt, c_sample, page_table, cache_moba_k, cache_moba_v, cache_fox_k, cache_fox_v, cache_fox_logf, state_gla, state_s5_re, state_s5_im, w_mod, b_mod, w_in, gla_w_gate2, gla_b_gate, gla_norm, fox_b_f, s5_lambda_re, s5_lambda_im, s5_log_dt, s5_b_re, s5_b_im, s5_c_re, s5_c_im, s5_d, s5_w_glu, s5_b_glu, w_branch, w_out, ln1_g, ln1_b, peer_w_q, peer_sub_keys, peer_u, peer_v, ln2_g, ln2_b):
    bp, seq, d = x_prompt.shape
    bs, dseq, _ = x_sample.shape
    depth = w_in.shape[0]
    n_pool, page = cache_moba_k.shape[1], cache_moba_k.shape[2]
    flat_cache = lambda c: c.transpose(0, 1, 3, 4, 2).reshape(depth, n_pool, BRANCH_WIDTH, page)
    paged = {'paged': True, 'page_table': page_table,
             'moba_k': flat_cache(cache_moba_k), 'moba_v': flat_cache(cache_moba_v),
             'fox_k': flat_cache(cache_fox_k), 'fox_v': flat_cache(cache_fox_v),
             'fox_logf_t': cache_fox_logf.transpose(0, 1, 3, 2)}
    c_all = jnp.concatenate([c_prompt, c_sample], axis=0)
    y_p = x_prompt.reshape(bp * seq, d)
    y_s = x_sample.reshape(bs * dseq, d)
    st_p, st_s = [], []
    for l in range(depth):
        w = _prep_weights(l, depth, d, w_in, gla_w_gate2, gla_b_gate, gla_norm, fox_b_f, s5_lambda_re, s5_lambda_im,
                          s5_log_dt, s5_b_re, s5_b_im, s5_c_re, s5_c_im, s5_d, s5_w_glu, s5_b_glu, w_branch, w_out,
                          ln1_g, ln1_b, peer_w_q, peer_sub_keys, peer_u, peer_v, ln2_g, ln2_b)
        mod = _modulation(c_all, w_mod[l], b_mod[l])
        mods = jnp.split(mod, 6, axis=-1)
        past_p = {'paged': False, 'gla': jnp.zeros((bp, N_HEADS, HEAD_DIM, HEAD_DIM), F32),
                  's5_re': jnp.zeros((bp, S5_GROUPS, S5_STATE), F32), 's5_im': jnp.zeros((bp, S5_GROUPS, S5_STATE), F32)}
        past_s = dict(paged, gla=state_gla[l], s5_re=state_s5_re[l], s5_im=state_s5_im[l])
        y_p, new_p = _layer_group(y_p, [m[:bp] for m in mods], w, bp, seq, past_p, l)
        y_s, new_s = _layer_group(y_s, [m[bp:] for m in mods], w, bs, dseq, past_s, l)
        st_p.append(new_p)
        st_s.append(new_s)

    stk = lambda states, name: jnp.stack([s[name] for s in states])
    names = ('moba_k', 'moba_v', 'fox_k', 'fox_v', 'fox_logf', 'gla', 's5_re', 's5_im')
    return ((y_p.reshape(bp, seq, d), y_s.reshape(bs, dseq, d))
            + tuple(stk(st_p, nm) for nm in names) + tuple(stk(st_s, nm) for nm in names))
```

```python
import functools
import math

import numpy as np
import jax
import jax.numpy as jnp
from jax import lax
from jax.experimental import pallas as pl
from jax.experimental.pallas import tpu as pltpu

F32 = jnp.float32
MXU_DTYPE = jnp.bfloat16
PEER_WEIGHT_DTYPE = jnp.bfloat16

HEAD_DIM = 64
N_HEADS = 4
BRANCH_WIDTH = N_HEADS * HEAD_DIM
N_BRANCH = 4
GLA_GATE_RANK = 16
GLA_TAU = 16.0
GLA_CHUNK = 64
GLA_SUBCHUNK = 16
MOBA_BLOCK = 256
MOBA_TOPK = 3
S5_GROUP = 16
S5_GROUPS = BRANCH_WIDTH // S5_GROUP
S5_STATE = 64
S5_WIDTH_STATE = S5_GROUPS * S5_STATE
PEER_HEADS = 8
PEER_DKEY = 256
PEER_NKEYS = 128
PEER_TOPK = 16
LN_EPS = 1e-5
NEG_BIG = -1e30

V7X_VMEM_LIMIT = 56 * 1024 * 1024
LANES = 128


def _cparams(*sem):
    return pltpu.CompilerParams(dimension_semantics=tuple(sem), vmem_limit_bytes=V7X_VMEM_LIMIT)


def _dot(a, b):
    return jnp.dot(a.astype(MXU_DTYPE), b.astype(MXU_DTYPE), preferred_element_type=F32)


def _dot_nt(a, b):
    return lax.dot_general(a.astype(MXU_DTYPE), b.astype(MXU_DTYPE), (((1,), (1,)), ((), ())),
                           preferred_element_type=F32)


def _dot_tn(a, b):
    return lax.dot_general(a.astype(MXU_DTYPE), b.astype(MXU_DTYPE), (((0,), (0,)), ((), ())),
                           preferred_element_type=F32)


def _split3(x):
    hi = x.astype(jnp.bfloat16)
    r1 = x - hi.astype(F32)
    mid = r1.astype(jnp.bfloat16)
    lo = (r1 - mid.astype(F32)).astype(jnp.bfloat16)
    return hi, mid, lo


def _dot_exact01(x, m01):
    hi, mid, lo = _split3(x)
    m = m01.astype(jnp.bfloat16)
    d = lambda a: jnp.dot(a, m, preferred_element_type=F32)
    return d(hi) + d(mid) + d(lo)


def _dot01_exact(m01, x):
    hi, mid, lo = _split3(x)
    m = m01.astype(jnp.bfloat16)
    d = lambda a: jnp.dot(m, a, preferred_element_type=F32)
    return d(hi) + d(mid) + d(lo)


def _ln(x):
    mu = jnp.mean(x, axis=-1, keepdims=True)
    xc = x - mu
    var = jnp.mean(xc * xc, axis=-1, keepdims=True)
    return xc * lax.rsqrt(var + LN_EPS)


def _sigmoid(x):
    return 1.0 / (1.0 + jnp.exp(-x))


def _log_sigmoid(x):
    return jnp.minimum(x, 0.0) - jnp.log1p(jnp.exp(-jnp.abs(x)))


def _gelu(x):
    return 0.5 * x * (1.0 + lax.erf(x * (1.0 / math.sqrt(2.0))))


def _iota(shape, axis):
    return lax.broadcasted_iota(jnp.int32, shape, axis)


def _mod_kernel(c_ref, w_ref, b_ref, o_ref):
    c = c_ref[...]
    o_ref[...] = _dot(c * _sigmoid(c), w_ref[...]) + b_ref[...]


def _modulation(c_all, w_mod, b_mod):
    n, d = c_all.shape
    width = w_mod.shape[1]
    tn = 1536
    return pl.pallas_call(
        _mod_kernel,
        grid=(width // tn,),
        in_specs=[pl.BlockSpec((n, d), lambda j: (0, 0)),
                  pl.BlockSpec((d, tn), lambda j: (0, j)),
                  pl.BlockSpec((1, tn), lambda j: (0, j))],
        out_specs=pl.BlockSpec((n, tn), lambda j: (0, j)),
        out_shape=jax.ShapeDtypeStruct((n, width), F32),
        compiler_params=_cparams("parallel"),
        name="modulation",
    )(c_all, w_mod, b_mod.reshape(1, width))


N_PROJ = 11
PROJ_SEGMENTS = (0, 1, 2, 3, 5, 6, 7, 8, 9, 10, 12)
SEG_GATE_A, SEG_GATE_F, SEG_BRANCH_GATE = 4, 11, 13


def _in_offsets(d):
    bw = BRANCH_WIDTH
    sizes = (bw,) * 4 + (GLA_GATE_RANK,) + (bw,) * 3 + (bw,) * 3 + (N_HEADS, bw, N_BRANCH * d)
    return [int(o) for o in np.concatenate([[0], np.cumsum(sizes)])]


def _repack_kernel(w_ref, wa_ref, ws_ref, wg_ref):
    rows, d_gate = wg_ref.shape
    offs = _in_offsets(d_gate // N_BRANCH)
    bw = BRANCH_WIDTH
    for i, seg in enumerate(PROJ_SEGMENTS):
        wa_ref[:, i * bw:(i + 1) * bw] = w_ref[:, offs[seg]:offs[seg] + bw].astype(wa_ref.dtype)
    small = jnp.concatenate([w_ref[:, offs[SEG_GATE_A]:offs[SEG_GATE_A + 1]],
                             w_ref[:, offs[SEG_GATE_F]:offs[SEG_GATE_F + 1]],
                             jnp.zeros((rows, LANES - GLA_GATE_RANK - N_HEADS), F32)], axis=1)
    ws_ref[...] = small.astype(ws_ref.dtype)
    wg_ref[...] = w_ref[:, offs[SEG_BRANCH_GATE]:offs[SEG_BRANCH_GATE + 1]].astype(wg_ref.dtype)


def _repack_w_in(w_in, layer):
    _, d, width = w_in.shape
    tr = _tile(d, 256)
    out_w = (N_PROJ * BRANCH_WIDTH, LANES, N_BRANCH * d)
    return pl.pallas_call(
        _repack_kernel,
        grid=(d // tr,),
        in_specs=[pl.BlockSpec((None, tr, width), lambda i: (layer, i, 0))],
        out_specs=[pl.BlockSpec((tr, w), lambda i: (i, 0)) for w in out_w],
        out_shape=[jax.ShapeDtypeStruct((d, w), MXU_DTYPE) for w in out_w],
        compiler_params=_cparams("parallel"),
        name="repack_w_in",
    )(w_in)


def _inproj_kernel(x_ref, sh_ref, sc_ref, wa_ref, ws_ref, gw2_ref, gb_ref, fb_ref,
                   h1_ref, gq_ref, gk_ref, gv_ref, gr_ref, mq_ref, mk_ref, mv_ref, fq_ref, fk_ref, fv_ref,
                   su_ref, la_ref, sm_ref):
    h1 = (_ln(x_ref[...]) * (1.0 + sc_ref[0]) + sh_ref[0]).astype(MXU_DTYPE)
    h1_ref[...] = h1
    outs = (gq_ref, gk_ref, gv_ref, gr_ref, mq_ref, mk_ref, mv_ref, fq_ref, fk_ref, fv_ref, su_ref)
    for i, r in enumerate(outs):
        r[...] = jnp.dot(h1, wa_ref[:, i * BRANCH_WIDTH:(i + 1) * BRANCH_WIDTH],
                         preferred_element_type=F32).astype(r.dtype)
    ps = jnp.dot(h1, ws_ref[...], preferred_element_type=F32)
    la_ref[...] = _log_sigmoid(_dot(ps, gw2_ref[...]) + gb_ref[...]) * (1.0 / GLA_TAU)
    sm_ref[...] = _log_sigmoid(ps + fb_ref[...])


def _inproj(x, sh, sc, wa, ws, gw2, gb, fb, tm):
    n, d = x.shape
    nt = n // tm
    mrow = sh.shape[1]
    per_tile = sh.shape[0] == nt and mrow == tm
    tiles_per_seq = nt // sh.shape[0] if not per_tile else 1
    mod_map = (lambda i: (i, 0, 0)) if per_tile else (lambda i: (i // tiles_per_seq, 0, 0))
    tok = lambda w: pl.BlockSpec((tm, w), lambda i: (i, 0))
    full = lambda a: pl.BlockSpec(a.shape, lambda i: (0,) * a.ndim)
    bw = BRANCH_WIDTH
    odt = [MXU_DTYPE] + [F32] * (N_PROJ + 2)
    owd = [d] + [bw] * N_PROJ + [bw, LANES]
    return pl.pallas_call(
        _inproj_kernel,
        grid=(nt,),
        in_specs=[tok(d), pl.BlockSpec((1, mrow, d), mod_map), pl.BlockSpec((1, mrow, d), mod_map),
                  full(wa), full(ws), full(gw2), full(gb), full(fb)],
        out_specs=[tok(w) for w in owd],
        out_shape=[jax.ShapeDtypeStruct((n, w), t) for w, t in zip(owd, odt)],
        compiler_params=_cparams("parallel"),
        name="inproj",
    )(x, sh, sc, wa, ws, gw2, gb, fb)


def _gla_kernel(q_ref, k_ref, v_ref, la_ref, gr_ref, gn_ref, s0_ref, o_ref, sT_ref, *, seq, chunk, sub):
    n_chunks = seq // chunk
    n_sub = chunk // sub
    scale = HEAD_DIM ** -0.5
    tril = (_iota((chunk, chunk), 1) <= _iota((chunk, chunk), 0))
    gn = gn_ref[...]

    def one_chunk(c, state):
        r0 = c * chunk if isinstance(c, int) else pl.multiple_of(c * chunk, chunk)
        rows = pl.ds(r0, chunk)
        g = la_ref[rows, :]
        bcum = _dot01_exact(tril, g)
        q = q_ref[rows, :] * scale
        k = k_ref[rows, :]
        v = v_ref[rows, :]
        b_end = bcum[chunk - 1:chunk, :]
        qe = q * jnp.exp(bcum)
        kd = k * jnp.exp(b_end - bcum)
        qs, ks = [], []
        for i in range(n_sub):
            base = bcum[i * sub - 1:i * sub, :] if i > 0 else jnp.zeros_like(b_end)
            nk = (i + 1) * sub
            qs.append(q[i * sub:nk, :] * jnp.exp(bcum[i * sub:nk, :] - base))
            ks.append(k[:nk, :] * jnp.exp(base - bcum[:nk, :]))
        new_state, outs = [], []
        for h in range(N_HEADS):
            sl = slice(h * HEAD_DIM, (h + 1) * HEAD_DIM)
            s_t = state[h]
            o_h = _dot_nt(qe[:, sl], s_t)
            att_rows = []
            for i in range(n_sub):
                nk = (i + 1) * sub
                a = _dot_nt(qs[i][:, sl], ks[i][:, sl])
                causal = _iota((sub, nk), 1) <= _iota((sub, nk), 0) + i * sub
                a = jnp.where(causal, a, 0.0)
                if nk < chunk:
                    a = jnp.concatenate([a, jnp.zeros((sub, chunk - nk), F32)], axis=1)
                att_rows.append(a)
            att = att_rows[0] if n_sub == 1 else jnp.concatenate(att_rows, axis=0)
            o_h = o_h + _dot(att, v[:, sl])
            new_state.append(s_t * jnp.exp(b_end[:, sl]) + _dot_tn(v[:, sl], kd[:, sl]))
            o_h = o_h * lax.rsqrt(jnp.mean(o_h * o_h, axis=-1, keepdims=True) + LN_EPS)
            outs.append(o_h)
        o = jnp.concatenate(outs, axis=1)
        gr = gr_ref[rows, :]
        o_ref[rows, :] = (o * gn * (gr * _sigmoid(gr))).astype(o_ref.dtype)
        return tuple(new_state)

    state0 = tuple(s0_ref[0, h] for h in range(N_HEADS))
    if n_chunks == 1:
        state = one_chunk(0, state0)
    else:
        state = lax.fori_loop(0, n_chunks, one_chunk, state0)
    for h in range(N_HEADS):
        sT_ref[0, h] = state[h]


def _gla(gq, gk, gv, la, gr, gn, s0_t, n_seq, seq):
    chunk = math.gcd(seq, GLA_CHUNK)
    sub = math.gcd(chunk, GLA_SUBCHUNK)
    tok = pl.BlockSpec((seq, BRANCH_WIDTH), lambda b: (b, 0))
    st = pl.BlockSpec((1, N_HEADS, HEAD_DIM, HEAD_DIM), lambda b: (b, 0, 0, 0))
    return pl.pallas_call(
        functools.partial(_gla_kernel, seq=seq, chunk=chunk, sub=sub),
        grid=(n_seq,),
        in_specs=[tok, tok, tok, tok, tok, pl.BlockSpec((1, BRANCH_WIDTH), lambda b: (0, 0)), st],
        out_specs=[tok, st],
        out_shape=[jax.ShapeDtypeStruct((n_seq * seq, BRANCH_WIDTH), F32),
                   jax.ShapeDtypeStruct((n_seq, N_HEADS, HEAD_DIM, HEAD_DIM), F32)],
        compiler_params=_cparams("parallel"),
        name="gla",
    )(gq, gk, gv, la, gr, gn, s0_t)


def _cum_kernel(lf_ref, o_ref):
    rows, length = lf_ref.shape
    triu = _iota((LANES, LANES), 0) <= _iota((LANES, LANES), 1)
    carry = jnp.zeros((rows, 1), F32)
    for c in range(length // LANES):
        cs = _dot_exact01(lf_ref[:, c * LANES:(c + 1) * LANES], triu) + carry
        o_ref[:, c * LANES:(c + 1) * LANES] = cs
        carry = cs[:, LANES - 1:LANES]


def _fox_cumsum(lf_rows):
    return pl.pallas_call(
        _cum_kernel,
        out_shape=jax.ShapeDtypeStruct(lf_rows.shape, F32),
        compiler_params=pltpu.CompilerParams(vmem_limit_bytes=V7X_VMEM_LIMIT),
        name="fox_cumsum",
    )(lf_rows)


def _alibi_slope(h):
    return 2.0 ** (-8.0 * (h + 1) / N_HEADS)


def _attn_prompt_kernel(q_ref, k_ref, v_ref, *rest, seq, moba):
    if moba:
        o_ref, kh_ref, vh_ref, km_ref = rest
    else:
        cum_ref, o_ref, kh_ref, vh_ref = rest
    blk = MOBA_BLOCK
    nb = seq // blk
    i = pl.program_id(1)

    @pl.when(i == 0)
    def _():
        for h in range(N_HEADS):
            sl = slice(h * HEAD_DIM, (h + 1) * HEAD_DIM)
            kf = k_ref[:, sl]
            kh_ref[h] = kf.astype(MXU_DTYPE)
            vh_ref[h] = v_ref[:, sl].astype(MXU_DTYPE)
            if moba:
                km = jnp.sum(kf.reshape(nb, blk, HEAD_DIM), axis=1) * (1.0 / blk)
                pad = km_ref.shape[1] - nb
                km_ref[h] = jnp.concatenate([km, jnp.zeros((pad, HEAD_DIM), F32)], axis=0) if pad else km

    nbp = km_ref.shape[1] if moba else 0
    heads = range(N_HEADS)
    cols = [slice(h * HEAD_DIM, (h + 1) * HEAD_DIM) for h in heads]
    r_minus_c = (_iota((blk, blk), 0) - _iota((blk, blk), 1)).astype(F32)
    causal = r_minus_c >= 0
    q_heads = [(q_ref[:, cols[h]] * (HEAD_DIM ** -0.5)).astype(MXU_DTYPE) for h in heads]

    if moba:
        sel_bias = []
        for h in heads:
            bs = _dot_nt(km_ref[h], q_ref[:, cols[h]])
            jrow = _iota((nbp, blk), 0)
            bs = jnp.where(jrow < i, bs, -jnp.inf)
            rank = jnp.zeros((nbp, blk), F32)
            for j2 in range(nb):
                row = bs[j2:j2 + 1, :]
                beats = (row > bs) | ((row == bs) & (j2 < jrow))
                rank = rank + beats.astype(F32)
            sel = (jrow < i) & (rank < MOBA_TOPK)
            sel_bias.append(jnp.where(sel, 0.0, NEG_BIG).T.astype(MXU_DTYPE))

    def tile_update(state, s, v_blk):
        m, l, acc = state
        m_new = jnp.maximum(m, jnp.max(s, axis=-1, keepdims=True))
        alpha = jnp.exp(m - m_new)
        p = jnp.exp(s - m_new)
        return m_new, alpha * l + jnp.sum(p, axis=-1, keepdims=True), alpha * acc + _dot(p, v_blk)

    def logits(h, c0):
        return _dot_nt(q_heads[h], kh_ref[h, pl.ds(c0, blk), :])

    c_own = pl.multiple_of(i * blk, blk)
    states = []
    for h in heads:
        s = logits(h, c_own)
        s = s - _alibi_slope(h) * r_minus_c if moba else s - cum_ref[0, h:h + 1, pl.ds(c_own, blk)]
        s = jnp.where(causal, s, NEG_BIG)
        m = jnp.max(s, axis=-1, keepdims=True)
        p = jnp.exp(s - m)
        states.append((m, jnp.sum(p, axis=-1, keepdims=True), _dot(p, vh_ref[h, pl.ds(c_own, blk), :])))

    def body(j, states):
        c0 = pl.multiple_of(j * blk, blk)
        new = []
        for h in heads:
            s = logits(h, c0)
            if moba:
                dist = r_minus_c + ((i - j) * blk).astype(F32)
                pick = (_iota((nbp, blk), 0) == j).astype(MXU_DTYPE)
                s = s - _alibi_slope(h) * dist + jnp.dot(sel_bias[h], pick, preferred_element_type=F32)
            else:
                s = s - cum_ref[0, h:h + 1, pl.ds(c0, blk)]
            new.append(tile_update(states[h], s, vh_ref[h, pl.ds(c0, blk), :]))
        return tuple(new)

    states = lax.fori_loop(0, i, body, tuple(states))
    o_ref[...] = jnp.concatenate([acc / l for _, l, acc in states], axis=1).astype(o_ref.dtype)


def _attn_prompt(q, k, v, cum, n_seq, seq, moba):
    blk = MOBA_BLOCK
    nq = seq // blk
    nbp = -(-nq // 8) * 8
    tile = pl.BlockSpec((blk, BRANCH_WIDTH), lambda b, i: (b * nq + i, 0))
    whole = pl.BlockSpec((seq, BRANCH_WIDTH), lambda b, i: (b, 0))
    in_specs = [tile, whole, whole]
    args = [q, k, v]
    scratch = [pltpu.VMEM((N_HEADS, seq, HEAD_DIM), MXU_DTYPE), pltpu.VMEM((N_HEADS, seq, HEAD_DIM), MXU_DTYPE)]
    if moba:
        scratch += [pltpu.VMEM((N_HEADS, nbp, HEAD_DIM), F32)]
    else:
        in_specs.append(pl.BlockSpec((1, N_HEADS, seq), lambda b, i: (b, 0, 0)))
        args.append(cum)
    return pl.pallas_call(
        functools.partial(_attn_prompt_kernel, seq=seq, moba=moba),
        grid=(n_seq, nq),
        in_specs=in_specs,
        out_specs=tile,
        out_shape=jax.ShapeDtypeStruct((n_seq * seq, BRANCH_WIDTH), F32),
        scratch_shapes=scratch,
        compiler_params=_cparams("parallel", "arbitrary"),
        name="moba_prompt" if moba else "fox_prompt",
    )(*args)


def _block_diag_queries(q):
    t = q.shape[0]
    col_head = _iota((t, BRANCH_WIDTH), 1) // HEAD_DIM
    return jnp.concatenate([jnp.where(col_head == h, q, jnp.zeros_like(q)) for h in range(N_HEADS)], axis=0)


def _head_diag(acc, t):
    return jnp.concatenate([acc[h * t:(h + 1) * t, h * HEAD_DIM:(h + 1) * HEAD_DIM] for h in range(N_HEADS)], axis=1)


def _attn_sample_kernel(pt_ref, mq_ref, mkn_ref, mvn_ref, fq_ref, fkn_ref, fvn_ref, lfn_ref, *refs,
                        t_new, n_pages, page, past_len):
    del pt_ref
    mk, mv, fk, fv, lfc = (refs[i * n_pages:(i + 1) * n_pages] for i in range(5))
    ob_ref, oc_ref = refs[5 * n_pages:]
    rows = N_HEADS * t_new
    scale = HEAD_DIM ** -0.5
    row_head = _iota((rows, 1), 0) // t_new
    row_t = _iota((rows, 1), 0) % t_new
    lane_n = _iota((rows, t_new), 1)
    causal = lane_n <= row_t
    cast = lambda r: r[...].astype(MXU_DTYPE)

    def softmax_pv(s_past, s_new, v_pages, v_new):
        m = jnp.maximum(jnp.max(s_past, axis=-1, keepdims=True), jnp.max(s_new, axis=-1, keepdims=True))
        p_past = jnp.exp(s_past - m)
        p_new = jnp.exp(s_new - m)
        l = jnp.sum(p_past, axis=-1, keepdims=True) + jnp.sum(p_new, axis=-1, keepdims=True)
        acc = _dot(p_new, v_new)
        for pg in range(n_pages):
            acc = acc + _dot_nt(p_past[:, pg * page:(pg + 1) * page], cast(v_pages[pg]))
        return _head_diag(acc / l, t_new)

    qm = _block_diag_queries(mq_ref[...]).astype(MXU_DTYPE)
    pages_per_block = MOBA_BLOCK // page
    n_blocks = n_pages // pages_per_block
    kmean_cols = []
    for b in range(n_blocks):
        blk = jnp.concatenate([mk[b * pages_per_block + i][...] for i in range(pages_per_block)], axis=1)
        kmean_cols.append(jnp.sum(blk, axis=1, keepdims=True) * (1.0 / MOBA_BLOCK))
    bs = _dot(qm, jnp.concatenate(kmean_cols, axis=1))
    jl = _iota((rows, n_blocks), 1)
    rank = jnp.zeros((rows, n_blocks), F32)
    for j2 in range(n_blocks):
        col = bs[:, j2:j2 + 1]
        rank = rank + ((col > bs) | ((col == bs) & (j2 < jl))).astype(F32)
    sel = (rank < MOBA_TOPK).astype(F32)
    sel_keys = jnp.concatenate(
        [jnp.broadcast_to(sel[:, b:b + 1], (rows, MOBA_BLOCK)) for b in range(n_blocks)], axis=1) > 0.5
    slope = jnp.zeros((rows, 1), F32)
    for h in range(N_HEADS):
        slope = jnp.where(row_head == h, _alibi_slope(h), slope)
    s = jnp.concatenate([jnp.dot(qm, cast(mk[pg]), preferred_element_type=F32) for pg in range(n_pages)], axis=1)
    dist = (past_len + row_t - _iota((rows, past_len), 1)).astype(F32)
    s = jnp.where(sel_keys, s * scale - slope * dist, NEG_BIG)
    s_n = _dot_nt(qm, mkn_ref[...]) * scale - slope * (row_t - lane_n).astype(F32)
    s_n = jnp.where(causal, s_n, NEG_BIG)
    ob_ref[...] = softmax_pv(s, s_n, mv, mvn_ref[...])

    qf = _block_diag_queries(fq_ref[...]).astype(MXU_DTYPE)
    lf_all = jnp.concatenate([lfc[pg][...] for pg in range(n_pages)], axis=0)
    triu = _iota((page, page), 0) <= _iota((page, page), 1)
    local = _dot_exact01(lf_all, triu)
    nr = n_pages * N_HEADS
    ri, ci = _iota((nr, nr), 0), _iota((nr, nr), 1)
    earlier_page = (ci % N_HEADS == ri % N_HEADS) & (ci // N_HEADS < ri // N_HEADS)
    cum = local + _dot01_exact(earlier_page, local[:, page - 1:page])
    bias = jnp.concatenate(
        [jnp.broadcast_to(cum[pg * N_HEADS:(pg + 1) * N_HEADS, None, :], (N_HEADS, t_new, page)).reshape(rows, page)
         for pg in range(n_pages)], axis=1)
    s = jnp.concatenate([jnp.dot(qf, cast(fk[pg]), preferred_element_type=F32) for pg in range(n_pages)], axis=1)
    s = s * scale - bias
    lfn = lfn_ref[...]
    cols, c = [], cum[nr - N_HEADS:nr, page - 1:page]
    for t in range(t_new):
        c = c + lfn[:, t:t + 1]
        cols.append(c)
    cum_n = jnp.concatenate(cols, axis=1)
    bias_n = jnp.broadcast_to(cum_n[:, None, :], (N_HEADS, t_new, t_new)).reshape(rows, t_new)
    s_n = jnp.where(causal, _dot_nt(qf, fkn_ref[...]) * scale - bias_n, NEG_BIG)
    oc_ref[...] = softmax_pv(s, s_n, fv, fvn_ref[...])


def _attn_sample(page_table, mq, mkn, mvn, fq, fkn, fvn, lfn, mkc, mvc, fkc, fvc, lfc, layer):
    n_seq, t_new, _ = mq.shape
    n_pages = page_table.shape[1]
    page = mkc.shape[3]
    past_len = n_pages * page
    assert past_len % MOBA_BLOCK == 0 and MOBA_BLOCK % page == 0 and t_new <= MOBA_BLOCK
    new = pl.BlockSpec((None, t_new, BRANCH_WIDTH), lambda b, pt: (b, 0, 0))

    def pages(width):
        return [pl.BlockSpec((None, None, width, page), lambda b, pt, pg=pg: (layer, pt[b, pg], 0, 0))
                for pg in range(n_pages)]

    grid_spec = pltpu.PrefetchScalarGridSpec(
        num_scalar_prefetch=1,
        grid=(n_seq,),
        in_specs=[new, new, new, new, new, new, pl.BlockSpec((None, N_HEADS, t_new), lambda b, pt: (b, 0, 0))]
                 + 4 * pages(BRANCH_WIDTH) + pages(N_HEADS),
        out_specs=[new, new],
    )
    return pl.pallas_call(
        functools.partial(_attn_sample_kernel, t_new=t_new, n_pages=n_pages, page=page, past_len=past_len),
        grid_spec=grid_spec,
        out_shape=[jax.ShapeDtypeStruct((n_seq, t_new, BRANCH_WIDTH), F32)] * 2,
        compiler_params=_cparams("parallel"),
        name="attn_sample",
    )(page_table, mq, mkn, mvn, fq, fkn, fvn, lfn,
      *([mkc] * n_pages + [mvc] * n_pages + [fkc] * n_pages + [fvc] * n_pages + [lfc] * n_pages))


def _s5_param_kernel(lre_ref, lim_ref, ldt_ref, bre_ref, bim_ref, are_ref, aim_ref, bbre_ref, bbim_ref):
    lam_re = lre_ref[...]
    lam_im = lim_ref[...]
    dt = jnp.exp(ldt_ref[...])
    mag = jnp.exp(lam_re * dt)
    ab_re = mag * jnp.cos(lam_im * dt)
    ab_im = mag * jnp.sin(lam_im * dt)
    den = lam_re * lam_re + lam_im * lam_im
    z_re = ((ab_re - 1.0) * lam_re + ab_im * lam_im) / den
    z_im = (ab_im * lam_re - (ab_re - 1.0) * lam_im) / den
    are_ref[...] = ab_re
    aim_ref[...] = ab_im
    b_re = bre_ref[...]
    b_im = bim_ref[...]
    bbre_ref[...] = z_re[:, None, :] * b_re - z_im[:, None, :] * b_im
    bbim_ref[...] = z_re[:, None, :] * b_im + z_im[:, None, :] * b_re


def _s5_params(lam_re, lam_im, log_dt, b_re, b_im):
    g, p = lam_re.shape
    i = b_re.shape[-1]
    outs = pl.pallas_call(
        _s5_param_kernel,
        out_shape=[jax.ShapeDtypeStruct((g, p), F32)] * 2 + [jax.ShapeDtypeStruct((g, i, p), F32)] * 2,
        name="s5_params",
    )(lam_re, lam_im, log_dt.reshape(g, 1), b_re.transpose(0, 2, 1), b_im.transpose(0, 2, 1))
    return outs


def _s5_kernel(u_ref, bd_ref, cd_ref, are_ref, aim_ref, d_ref, wg_ref, bg_ref, h0r_ref, h0i_ref,
               o_ref, hr_ref, hi_ref, bu_ref, hs_ref, *, steps, nb):
    ns = S5_WIDTH_STATE

    @pl.when(pl.program_id(0) == 0)
    def _():
        hr_ref[...] = h0r_ref[...]
        hi_ref[...] = h0i_ref[...]

    u = u_ref[...].reshape(steps * nb, BRANCH_WIDTH)
    bu_ref[...] = _dot(u, bd_ref[...])
    a_re = jnp.broadcast_to(are_ref[...], (nb, ns))
    a_im = jnp.broadcast_to(aim_ref[...], (nb, ns))

    def step(t, carry):
        h_re, h_im = carry
        rows = pl.ds(pl.multiple_of(t * nb, nb), nb)
        n_re = a_re * h_re - a_im * h_im + bu_ref[rows, 0:ns]
        n_im = a_re * h_im + a_im * h_re + bu_ref[rows, ns:2 * ns]
        hs_ref[rows, 0:ns] = n_re
        hs_ref[rows, ns:2 * ns] = n_im
        return n_re, n_im

    h_re, h_im = lax.fori_loop(0, steps, step, (hr_ref[...], hi_ref[...]))
    hr_ref[...] = h_re
    hi_ref[...] = h_im
    y = _dot(hs_ref[...], cd_ref[...]) + d_ref[...] * u
    z = _gelu(y)
    o = z * _sigmoid(_dot(z, wg_ref[...]) + bg_ref[...])
    o_ref[...] = o.reshape(steps, nb, BRANCH_WIDTH).astype(o_ref.dtype)


def _s5(u_tb, bd, cd, a_re, a_im, d, wg, bg, h0r, h0i, steps):
    t_len, nb, _ = u_tb.shape
    full = lambda a: pl.BlockSpec(a.shape, lambda i: (0,) * a.ndim)
    blk = pl.BlockSpec((steps, nb, BRANCH_WIDTH), lambda i: (i, 0, 0))
    st = pl.BlockSpec((nb, S5_WIDTH_STATE), lambda i: (0, 0))
    return pl.pallas_call(
        functools.partial(_s5_kernel, steps=steps, nb=nb),
        grid=(t_len // steps,),
        in_specs=[blk, full(bd), full(cd), full(a_re), full(a_im), full(d), full(wg), full(bg), st, st],
        out_specs=[blk, st, st],
        out_shape=[jax.ShapeDtypeStruct((t_len, nb, BRANCH_WIDTH), F32),
                   jax.ShapeDtypeStruct((nb, S5_WIDTH_STATE), F32), jax.ShapeDtypeStruct((nb, S5_WIDTH_STATE), F32)],
        scratch_shapes=[pltpu.VMEM((steps * nb, 2 * S5_WIDTH_STATE), F32),
                        pltpu.VMEM((steps * nb, 2 * S5_WIDTH_STATE), F32)],
        compiler_params=_cparams("arbitrary"),
        name="s5",
    )(u_tb, bd, cd, a_re, a_im, d, wg, bg, h0r, h0i)


def _merge_kernel(x_ref, h1_ref, oa_ref, ob_ref, oc_ref, od_ref, g1_ref, sh2_ref, sc2_ref,
                  wg_ref, wb_ref, wo_ref, lg_ref, lb_ref, x1_ref, h2_ref, h2t_ref, *, alpha):
    d = x_ref.shape[1]
    h1 = h1_ref[...]
    merged = None
    for n, br in enumerate((oa_ref, ob_ref, oc_ref, od_ref)):
        gate = _sigmoid(jnp.dot(h1, wg_ref[:, n * d:(n + 1) * d], preferred_element_type=F32))
        up = _dot(br[...], wb_ref[n])
        merged = gate * up if merged is None else merged + gate * up
    mix = _dot(merged, wo_ref[...])
    x1 = _ln(alpha * x_ref[...] + (1.0 + g1_ref[0]) * mix) * lg_ref[...] + lb_ref[...]
    x1_ref[...] = x1
    h2 = _ln(x1) * (1.0 + sc2_ref[0]) + sh2_ref[0]
    h2_ref[...] = h2.astype(h2_ref.dtype)
    h2t_ref[...] = h2.T.astype(h2t_ref.dtype)


def _merge(x, h1, oa, ob, oc, od, g1, sh2, sc2, wg, wb, wo, lg, lb, tm, alpha):
    n, d = x.shape
    nt = n // tm
    mrow = g1.shape[1]
    per_tile = g1.shape[0] == nt and mrow == tm
    tiles_per_seq = nt // g1.shape[0] if not per_tile else 1
    mod_map = (lambda i: (i, 0, 0)) if per_tile else (lambda i: (i // tiles_per_seq, 0, 0))
    mod = pl.BlockSpec((1, mrow, d), mod_map)
    tok = lambda w: pl.BlockSpec((tm, w), lambda i: (i, 0))
    full = lambda a: pl.BlockSpec(a.shape, lambda i: (0,) * a.ndim)
    bw = BRANCH_WIDTH
    return pl.pallas_call(
        functools.partial(_merge_kernel, alpha=alpha),
        grid=(nt,),
        in_specs=[tok(d), tok(d), tok(bw), tok(bw), tok(bw), tok(bw), mod, mod, mod,
                  full(wg), full(wb), full(wo), full(lg), full(lb)],
        out_specs=[tok(d), tok(d), pl.BlockSpec((d, tm), lambda i: (0, i))],
        out_shape=[jax.ShapeDtypeStruct((n, d), F32), jax.ShapeDtypeStruct((n, d), MXU_DTYPE),
                   jax.ShapeDtypeStruct((d, n), MXU_DTYPE)],
        compiler_params=_cparams("parallel"),
        name="merge",
    )(x, h1, oa, ob, oc, od, g1, sh2, sc2, wg, wb, wo, lg, lb)


def _topk_rows(s, k, rid=None):
    if rid is None:
        rid = _iota(s.shape, 0)
    rank = jnp.full(s.shape, float(k), F32)
    vals = []
    for r in range(k):
        m = jnp.max(s, axis=0, keepdims=True)
        idx = jnp.min(jnp.where(s == m, rid, jnp.iinfo(jnp.int32).max), axis=0, keepdims=True)
        hit = rid == idx
        rank = jnp.where(hit, float(r), rank)
        s = jnp.where(hit, -jnp.inf, s)
        vals.append(m)
    return jnp.concatenate(vals, axis=0), rank


def _peer_route_kernel(h_ref, wq_ref, sk_ref, cnt_ref, e0_ref, e1_ref, r1_ref, q_ref):
    kk = PEER_TOPK
    half = PEER_DKEY // 2
    q_ref[...] = jnp.dot(h_ref[...], wq_ref[...], preferred_element_type=F32)
    tn = min(PEER_ROUTE_PASS, h_ref.shape[0])
    n_pass = h_ref.shape[0] // tn

    @pl.loop(0, PEER_HEADS * n_pass)
    def _(it):
        h = it // n_pass
        toks = pl.ds(pl.multiple_of((it % n_pass) * tn, tn), tn)
        s, sv, rk = [], [], []
        for p in range(2):
            c0 = pl.multiple_of(h * PEER_DKEY + p * half, half)
            sp = _dot_nt(sk_ref[p], q_ref[toks, pl.ds(c0, half)])
            v, r = _topk_rows(sp, kk)
            s.append(sp)
            sv.append(v)
            rk.append(r)
        grid = sv[0][:, None, :] + sv[1][None, :, :]
        sub = 8
        cand = jnp.concatenate([grid[k0, 0:sub, :] for k0 in range(kk)] + [grid[0, sub:kk, :]], axis=0)
        n_cand = cand.shape[0]
        grp, loc = _iota((n_cand, tn), 0) // sub, _iota((n_cand, tn), 0) % sub
        flat_id = jnp.where(grp < kk, grp * kk + loc, sub + loc)
        _, crank = _topk_rows(cand, kk, flat_id)
        taken = crank < float(kk)
        per_grp = jnp.sum(taken.astype(F32).reshape(n_cand // sub, sub, tn), axis=1)
        cnt = per_grp[0:kk, :] + jnp.where(_iota((kk, tn), 0) == 0, per_grp[kk:kk + 1, :], 0.0)
        top = sv[0][0:1, :] + sv[1][0:1, :]
        z = jnp.sum(jnp.where(taken, jnp.exp(cand - top), 0.0), axis=0, keepdims=True)
        cnt_dense = jnp.zeros((PEER_NKEYS, tn), F32)
        for k0 in range(kk):
            cnt_dense = jnp.where(rk[0] == float(k0), cnt[k0:k0 + 1, :], cnt_dense)
        cnt_ref[h, :, toks] = cnt_dense
        e0_ref[h, :, toks] = jnp.exp(s[0] - sv[0][0:1, :])
        e1_ref[h, :, toks] = (jnp.exp(s[1] - sv[1][0:1, :]) / z).astype(e1_ref.dtype)
        r1_ref[h, :, toks] = rk[1].astype(r1_ref.dtype)


def _peer_route(h2, wq, sk, tn):
    n, d = h2.shape
    full = lambda a: pl.BlockSpec(a.shape, lambda i: (0,) * a.ndim)
    out = pl.BlockSpec((PEER_HEADS, PEER_NKEYS, tn), lambda i: (0, 0, i))
    return pl.pallas_call(
        _peer_route_kernel,
        grid=(n // tn,),
        in_specs=[pl.BlockSpec((tn, d), lambda i: (i, 0)), full(wq), full(sk)],
        out_specs=[out] * 4,
        out_shape=[jax.ShapeDtypeStruct((PEER_HEADS, PEER_NKEYS, n), t)
                   for t in (F32, F32, PEER_WEIGHT_DTYPE, PEER_WEIGHT_DTYPE)],
        scratch_shapes=[pltpu.VMEM((tn, PEER_HEADS * PEER_DKEY), F32)],
        compiler_params=_cparams("parallel"),
        name="peer_route",
    )(h2, wq, sk)


PEER_ROUTE_PASS = 256
PEER_STAGE_ROWS = 256
PEER_LANE_CHUNK = 256


def _peer_expert_kernel(ht_ref, x1_ref, g2_ref, cnt_ref, e0_ref, e1_ref, r1_ref, u_ref, vt_ref, lg_ref, lb_ref,
                        o_ref, acc_ref, g_ref, a_ref, *, alpha):
    e = pl.program_id(1)
    te, tn = a_ref.shape
    stage = min(PEER_STAGE_ROWS, te)
    n_stage = te // stage
    lane_chunk = min(PEER_LANE_CHUNK, tn)

    @pl.when(e == 0)
    def _():
        acc_ref[...] = jnp.zeros(acc_ref.shape, F32)

    a_ref[...] = jnp.dot(u_ref[...], ht_ref[...], preferred_element_type=F32)
    wdt = e1_ref.dtype

    def weighted_activation(s):
        for c in range(stage // PEER_NKEYS):
            i0_local = s * (stage // PEER_NKEYS) + c
            rows = slice(i0_local * PEER_NKEYS, (i0_local + 1) * PEER_NKEYS)
            row = pl.ds(e * (te // PEER_NKEYS) + i0_local, 1)
            for t0 in range(0, tn, lane_chunk):
                lanes = slice(t0, t0 + lane_chunk)
                w = jnp.zeros((PEER_NKEYS, lane_chunk), wdt)
                for h in range(PEER_HEADS):
                    cnt = cnt_ref[h, row, lanes].astype(wdt)
                    e0 = e0_ref[h, row, lanes].astype(wdt)
                    w = w + jnp.where(r1_ref[h, :, lanes] < cnt, e1_ref[h, :, lanes], jnp.zeros((), wdt)) * e0
                g_ref[rows, lanes] = (w.astype(F32) * _gelu(a_ref[rows, lanes])).astype(g_ref.dtype)

    def project(s):
        rows = slice(s * stage, (s + 1) * stage)
        acc_ref[...] += jnp.dot(vt_ref[:, rows], g_ref[rows, :], preferred_element_type=F32)

    for s in range(n_stage + 1):
        if s < n_stage:
            weighted_activation(s)
        if s >= 1:
            project(s - 1)

    @pl.when(e == pl.num_programs(1) - 1)
    def _():
        ff = acc_ref[...].T
        o_ref[...] = _ln(alpha * x1_ref[...] + (1.0 + g2_ref[0]) * ff) * lg_ref[...] + lb_ref[...]


def _peer_expert(h2t, x1, g2, cnt, e0, e1, r1, u, vt, lg, lb, tn, te, alpha):
    d, n = h2t.shape
    nt = n // tn
    n_exp = u.shape[0]
    mrow = g2.shape[1]
    per_tile = g2.shape[0] == nt and mrow == tn
    tiles_per_seq = nt // g2.shape[0] if not per_tile else 1
    mod_map = (lambda i, e: (i, 0, 0)) if per_tile else (lambda i, e: (i // tiles_per_seq, 0, 0))
    tok = pl.BlockSpec((tn, d), lambda i, e: (i, 0))
    route = pl.BlockSpec((PEER_HEADS, PEER_NKEYS, tn), lambda i, e: (0, 0, i))
    vec = pl.BlockSpec((1, d), lambda i, e: (0, 0))
    return pl.pallas_call(
        functools.partial(_peer_expert_kernel, alpha=alpha),
        grid=(nt, n_exp // te),
        in_specs=[pl.BlockSpec((d, tn), lambda i, e: (0, i)), tok, pl.BlockSpec((1, mrow, d), mod_map),
                  route, route, route, route,
                  pl.BlockSpec((te, d), lambda i, e: (e, 0)), pl.BlockSpec((d, te), lambda i, e: (0, e)), vec, vec],
        out_specs=tok,
        out_shape=jax.ShapeDtypeStruct((n, d), F32),
        scratch_shapes=[pltpu.VMEM((d, tn), F32), pltpu.VMEM((te, tn), MXU_DTYPE), pltpu.VMEM((te, tn), F32)],
        compiler_params=_cparams("parallel", "arbitrary"),
        name="peer_expert",
    )(h2t, x1, g2, cnt, e0, e1, r1, u, vt, lg, lb)


def _block_diag_in(bb_t):
    g, i, p = bb_t.shape
    return jnp.einsum('gip,gh->gihp', bb_t, jnp.eye(g, dtype=bb_t.dtype)).reshape(g * i, g * p)


def _block_diag_out(c):
    g, i, p = c.shape
    return jnp.einsum('gip,gh->gphi', c, jnp.eye(g, dtype=c.dtype)).reshape(g * p, g * i)


def _tile(n, pref):
    return pref if n % pref == 0 else n


def _layer_group(x, mods, w, n_seq, seq, past, layer):
    n, d = x.shape
    bw = BRANCH_WIDTH
    alpha = w['alpha']
    tm = _tile(n, 512)
    tm2 = _tile(n, 256)

    def mod_arr(m, t):
        if seq % t == 0:
            return m.reshape(n_seq, 1, d)
        return jnp.repeat(m, seq, axis=0).reshape(n // t, t, d)

    sh1, sc1, g1, sh2, sc2, g2 = mods
    (h1, gq, gk, gv, gr, mq, mk, mv, fq, fk, fv, su, la, sm) = _inproj(
        x, mod_arr(sh1, tm), mod_arr(sc1, tm), w['wa'], w['ws'], w['gw2'], w['gb'], w['fb'], tm)
    logf = sm[:, GLA_GATE_RANK:GLA_GATE_RANK + N_HEADS]

    s0_t = past['gla'].transpose(0, 1, 3, 2)
    oa, gla_t = _gla(gq, gk, gv, la, gr, w['gn'], s0_t, n_seq, seq)
    gla_state = gla_t.transpose(0, 1, 3, 2)

    if past['paged']:
        r3 = lambda a: a.reshape(n_seq, seq, bw)
        lfn = logf.reshape(n_seq, seq, N_HEADS).transpose(0, 2, 1)
        ob, oc = _attn_sample(past['page_table'], r3(mq), r3(mk), r3(mv), r3(fq), r3(fk), r3(fv), lfn,
                              past['moba_k'], past['moba_v'], past['fox_k'], past['fox_v'], past['fox_logf_t'], layer)
        ob = ob.reshape(n, bw)
        oc = oc.reshape(n, bw)
    else:
        lf_rows = logf.reshape(n_seq, seq, N_HEADS).transpose(0, 2, 1).reshape(n_seq * N_HEADS, seq)
        cum = _fox_cumsum(lf_rows).reshape(n_seq, N_HEADS, seq)
        ob = _attn_prompt(mq, mk, mv, None, n_seq, seq, True)
        oc = _attn_prompt(fq, fk, fv, cum, n_seq, seq, False)

    steps = math.gcd(seq, 128)
    u_tb = su.reshape(n_seq, seq, bw).transpose(1, 0, 2)
    od_tb, s5_re, s5_im = _s5(u_tb, w['bd'], w['cd'], w['a_re'], w['a_im'], w['s5_d'], w['wglu'], w['bglu'],
                              past['s5_re'].reshape(n_seq, S5_WIDTH_STATE), past['s5_im'].reshape(n_seq, S5_WIDTH_STATE),
                              steps)
    od = od_tb.transpose(1, 0, 2).reshape(n, bw)

    x1, h2, h2t = _merge(x, h1, oa, ob, oc, od, mod_arr(g1, tm2), mod_arr(sh2, tm2), mod_arr(sc2, tm2),
                         w['wg'], w['wb'], w['wo'], w['ln1_g'], w['ln1_b'], tm2, alpha)

    cnt, e0, e1, r1 = _peer_route(h2, w['wq'], w['sk'], tm2)
    x2 = _peer_expert(h2t, x1, mod_arr(g2, tm), cnt, e0, e1, r1, w['pu'], w['pvt'], w['ln2_g'], w['ln2_b'],
                      tm, 1024, alpha)

    hd = lambda a: a.reshape(n_seq, seq, N_HEADS, HEAD_DIM)
    new = {'moba_k': hd(mk), 'moba_v': hd(mv), 'fox_k': hd(fk), 'fox_v': hd(fv),
           'fox_logf': logf.reshape(n_seq, seq, N_HEADS), 'gla': gla_state,
           's5_re': s5_re.reshape(n_seq, S5_GROUPS, S5_STATE), 's5_im': s5_im.reshape(n_seq, S5_GROUPS, S5_STATE)}
    return x2, new


def _prep_weights(l, depth, d, w_in, gla_w_gate2, gla_b_gate, gla_norm, fox_b_f, s5_lambda_re, s5_lambda_im, s5_log_dt,
                  s5_b_re, s5_b_im, s5_c_re, s5_c_im, s5_d, s5_w_glu, s5_b_glu, w_branch, w_out, ln1_g, ln1_b,
                  peer_w_q, peer_sub_keys, peer_u, peer_v, ln2_g, ln2_b):
    bw = BRANCH_WIDTH
    bf = lambda a: a.astype(MXU_DTYPE)
    pad_small = LANES - GLA_GATE_RANK - N_HEADS
    w = {}
    w['wa'], w['ws'], w['wg'] = _repack_w_in(w_in, l)
    w['gw2'] = bf(jnp.concatenate([gla_w_gate2[l], jnp.zeros((LANES - GLA_GATE_RANK, bw), F32)], axis=0))
    w['gb'] = gla_b_gate[l].reshape(1, bw)
    w['gn'] = gla_norm[l].reshape(1, bw)
    w['fb'] = jnp.concatenate([jnp.zeros((GLA_GATE_RANK,), F32), fox_b_f[l], jnp.zeros((pad_small,), F32)]).reshape(1, LANES)
    a_re, a_im, bb_re, bb_im = _s5_params(s5_lambda_re[l], s5_lambda_im[l], s5_log_dt[l], s5_b_re[l], s5_b_im[l])
    w['a_re'] = a_re.reshape(1, S5_WIDTH_STATE)
    w['a_im'] = a_im.reshape(1, S5_WIDTH_STATE)
    w['bd'] = bf(jnp.concatenate([_block_diag_in(bb_re), _block_diag_in(bb_im)], axis=1))
    w['cd'] = bf(jnp.concatenate([_block_diag_out(s5_c_re[l]), -_block_diag_out(s5_c_im[l])], axis=0))
    w['s5_d'] = s5_d[l].reshape(1, bw)
    w['wglu'] = bf(s5_w_glu[l])
    w['bglu'] = s5_b_glu[l].reshape(1, bw)
    w['wb'] = bf(w_branch[l])
    w['wo'] = bf(w_out[l])
    w['ln1_g'] = ln1_g[l].reshape(1, d)
    w['ln1_b'] = ln1_b[l].reshape(1, d)
    w['wq'] = bf(peer_w_q[l])
    w['sk'] = bf(peer_sub_keys[l])
    w['pu'] = bf(peer_u[l])
    w['pvt'] = bf(peer_v[l]).T
    w['ln2_g'] = ln2_g[l].reshape(1, d)
    w['ln2_b'] = ln2_b[l].reshape(1, d)
    w['alpha'] = (2.0 * depth) ** 0.25
    return w


def kernel(x_prompt, x_sample, c_prompt, c_sample, page_table, cache_moba_k, cache_moba_v, cache_fox_k, cache_fox_v, cache_fox_logf, state_gla, state_s5_re, state_s5_im, w_mod, b_mod, w_in, gla_w_gate2, gla_b_gate, gla_norm, fox_b_f, s5_lambda_re, s5_lambda_im, s5_log_dt, s5_b_re, s5_b_im, s5_c_re, s5_c_im, s5_d, s5_w_glu, s5_b_glu, w_branch, w_out, ln1_g, ln1_b, peer_w_q, peer_sub_keys, peer_u, peer_v, ln2_g, ln2_b):
    bp, seq, d = x_prompt.shape
    bs, dseq, _ = x_sample.shape
    depth = w_in.shape[0]
    n_pool, page = cache_moba_k.shape[1], cache_moba_k.shape[2]
    flat_cache = lambda c: c.transpose(0, 1, 3, 4, 2).reshape(depth, n_pool, BRANCH_WIDTH, page)
    paged = {'paged': True, 'page_table': page_table,
             'moba_k': flat_cache(cache_moba_k), 'moba_v': flat_cache(cache_moba_v),
             'fox_k': flat_cache(cache_fox_k), 'fox_v': flat_cache(cache_fox_v),
             'fox_logf_t': cache_fox_logf.transpose(0, 1, 3, 2)}
    c_all = jnp.concatenate([c_prompt, c_sample], axis=0)
    y_p = x_prompt.reshape(bp * seq, d)
    y_s = x_sample.reshape(bs * dseq, d)
    st_p, st_s = [], []
    for l in range(depth):
        w = _prep_weights(l, depth, d, w_in, gla_w_gate2, gla_b_gate, gla_norm, fox_b_f, s5_lambda_re, s5_lambda_im,
                          s5_log_dt, s5_b_re, s5_b_im, s5_c_re, s5_c_im, s5_d, s5_w_glu, s5_b_glu, w_branch, w_out,
                          ln1_g, ln1_b, peer_w_q, peer_sub_keys, peer_u, peer_v, ln2_g, ln2_b)
        mod = _modulation(c_all, w_mod[l], b_mod[l])
        mods = jnp.split(mod, 6, axis=-1)
        past_p = {'paged': False, 'gla': jnp.zeros((bp, N_HEADS, HEAD_DIM, HEAD_DIM), F32),
                  's5_re': jnp.zeros((bp, S5_GROUPS, S5_STATE), F32), 's5_im': jnp.zeros((bp, S5_GROUPS, S5_STATE), F32)}
        past_s = dict(paged, gla=state_gla[l], s5_re=state_s5_re[l], s5_im=state_s5_im[l])
        y_p, new_p = _layer_group(y_p, [m[:bp] for m in mods], w, bp, seq, past_p, l)
        y_s, new_s = _layer_group(y_s, [m[bp:] for m in mods], w, bs, dseq, past_s, l)
        st_p.append(new_p)
        st_s.append(new_s)

    stk = lambda states, name: jnp.stack([s[name] for s in states])
    names = ('moba_k', 'moba_v', 'fox_k', 'fox_v', 'fox_logf', 'gla', 's5_re', 's5_im')
    return ((y_p.reshape(bp, seq, d), y_s.reshape(bs, dseq, d))
            + tuple(stk(st_p, nm) for nm in names) + tuple(stk(st_s, nm) for nm in names))
```

```python
import functools
import math

import numpy as np
import jax
import jax.numpy as jnp
from jax import lax
from jax.experimental import pallas as pl
from jax.experimental.pallas import tpu as pltpu

F32 = jnp.float32
MXU_DTYPE = jnp.bfloat16
PEER_WEIGHT_DTYPE = jnp.bfloat16

HEAD_DIM = 64
N_HEADS = 4
BRANCH_WIDTH = N_HEADS * HEAD_DIM
N_BRANCH = 4
GLA_GATE_RANK = 16
GLA_TAU = 16.0
GLA_CHUNK = 64
GLA_SUBCHUNK = 16
MOBA_BLOCK = 256
MOBA_TOPK = 3
S5_GROUP = 16
S5_GROUPS = BRANCH_WIDTH // S5_GROUP
S5_STATE = 64
S5_WIDTH_STATE = S5_GROUPS * S5_STATE
PEER_HEADS = 8
PEER_DKEY = 256
PEER_NKEYS = 128
PEER_TOPK = 16
LN_EPS = 1e-5
NEG_BIG = -1e30

V7X_VMEM_LIMIT = 56 * 1024 * 1024
LANES = 128


def _cparams(*sem):
    return pltpu.CompilerParams(dimension_semantics=tuple(sem), vmem_limit_bytes=V7X_VMEM_LIMIT)


def _dot(a, b):
    return jnp.dot(a.astype(MXU_DTYPE), b.astype(MXU_DTYPE), preferred_element_type=F32)


def _dot_nt(a, b):
    return lax.dot_general(a.astype(MXU_DTYPE), b.astype(MXU_DTYPE), (((1,), (1,)), ((), ())),
                           preferred_element_type=F32)


def _dot_tn(a, b):
    return lax.dot_general(a.astype(MXU_DTYPE), b.astype(MXU_DTYPE), (((0,), (0,)), ((), ())),
                           preferred_element_type=F32)


def _split3(x):
    hi = x.astype(jnp.bfloat16)
    r1 = x - hi.astype(F32)
    mid = r1.astype(jnp.bfloat16)
    lo = (r1 - mid.astype(F32)).astype(jnp.bfloat16)
    return hi, mid, lo


def _dot_exact01(x, m01):
    hi, mid, lo = _split3(x)
    m = m01.astype(jnp.bfloat16)
    d = lambda a: jnp.dot(a, m, preferred_element_type=F32)
    return d(hi) + d(mid) + d(lo)


def _dot01_exact(m01, x):
    hi, mid, lo = _split3(x)
    m = m01.astype(jnp.bfloat16)
    d = lambda a: jnp.dot(m, a, preferred_element_type=F32)
    return d(hi) + d(mid) + d(lo)


def _ln(x):
    mu = jnp.mean(x, axis=-1, keepdims=True)
    xc = x - mu
    var = jnp.mean(xc * xc, axis=-1, keepdims=True)
    return xc * lax.rsqrt(var + LN_EPS)


def _sigmoid(x):
    return 1.0 / (1.0 + jnp.exp(-x))


def _log_sigmoid(x):
    return jnp.minimum(x, 0.0) - jnp.log1p(jnp.exp(-jnp.abs(x)))


def _gelu(x):
    return 0.5 * x * (1.0 + lax.erf(x * (1.0 / math.sqrt(2.0))))


def _iota(shape, axis):
    return lax.broadcasted_iota(jnp.int32, shape, axis)


def _mod_kernel(c_ref, w_ref, b_ref, o_ref):
    c = c_ref[...]
    o_ref[...] = _dot(c * _sigmoid(c), w_ref[...]) + b_ref[...]


def _modulation(c_all, w_mod, b_mod, layer):
    n, d = c_all.shape
    depth, _, width = w_mod.shape
    tn = _tile(width, 1536)
    return pl.pallas_call(
        _mod_kernel,
        grid=(width // tn,),
        in_specs=[pl.BlockSpec((n, d), lambda j: (0, 0)),
                  pl.BlockSpec((None, d, tn), lambda j: (layer, 0, j)),
                  pl.BlockSpec((None, 1, tn), lambda j: (layer, 0, j))],
        out_specs=pl.BlockSpec((n, tn), lambda j: (0, j)),
        out_shape=jax.ShapeDtypeStruct((n, width), F32),
        compiler_params=_cparams("parallel"),
        name="modulation",
    )(c_all, w_mod, b_mod.reshape(depth, 1, width))


N_PROJ = 11
PROJ_SEGMENTS = (0, 1, 2, 3, 5, 6, 7, 8, 9, 10, 12)
SEG_GATE_A, SEG_GATE_F, SEG_BRANCH_GATE = 4, 11, 13


def _in_offsets(d):
    bw = BRANCH_WIDTH
    sizes = (bw,) * 4 + (GLA_GATE_RANK,) + (bw,) * 3 + (bw,) * 3 + (N_HEADS, bw, N_BRANCH * d)
    return [int(o) for o in np.concatenate([[0], np.cumsum(sizes)])]


def _repack_kernel(w_ref, wa_ref, ws_ref, wg_ref):
    rows, d_gate = wg_ref.shape
    offs = _in_offsets(d_gate // N_BRANCH)
    bw = BRANCH_WIDTH
    for i, seg in enumerate(PROJ_SEGMENTS):
        wa_ref[:, i * bw:(i + 1) * bw] = w_ref[:, offs[seg]:offs[seg] + bw].astype(wa_ref.dtype)
    small = jnp.concatenate([w_ref[:, offs[SEG_GATE_A]:offs[SEG_GATE_A + 1]],
                             w_ref[:, offs[SEG_GATE_F]:offs[SEG_GATE_F + 1]],
                             jnp.zeros((rows, LANES - GLA_GATE_RANK - N_HEADS), F32)], axis=1)
    ws_ref[...] = small.astype(ws_ref.dtype)
    wg_ref[...] = w_ref[:, offs[SEG_BRANCH_GATE]:offs[SEG_BRANCH_GATE + 1]].astype(wg_ref.dtype)


def _repack_w_in(w_in, layer):
    _, d, width = w_in.shape
    tr = _tile(d, 256)
    out_w = (N_PROJ * BRANCH_WIDTH, LANES, N_BRANCH * d)
    return pl.pallas_call(
        _repack_kernel,
        grid=(d // tr,),
        in_specs=[pl.BlockSpec((None, tr, width), lambda i: (layer, i, 0))],
        out_specs=[pl.BlockSpec((tr, w), lambda i: (i, 0)) for w in out_w],
        out_shape=[jax.ShapeDtypeStruct((d, w), MXU_DTYPE) for w in out_w],
        compiler_params=_cparams("parallel"),
        name="repack_w_in",
    )(w_in)


def _inproj_kernel(x_ref, sh_ref, sc_ref, wa_ref, ws_ref, gw2_ref, gb_ref, fb_ref,
                   h1_ref, gq_ref, gk_ref, gv_ref, gr_ref, mq_ref, mk_ref, mv_ref, fq_ref, fk_ref, fv_ref,
                   su_ref, la_ref, sm_ref):
    h1 = (_ln(x_ref[...]) * (1.0 + sc_ref[0]) + sh_ref[0]).astype(MXU_DTYPE)
    h1_ref[...] = h1
    outs = (gq_ref, gk_ref, gv_ref, gr_ref, mq_ref, mk_ref, mv_ref, fq_ref, fk_ref, fv_ref, su_ref)
    for i, r in enumerate(outs):
        r[...] = jnp.dot(h1, wa_ref[:, i * BRANCH_WIDTH:(i + 1) * BRANCH_WIDTH],
                         preferred_element_type=F32).astype(r.dtype)
    ps = jnp.dot(h1, ws_ref[...], preferred_element_type=F32)
    la_ref[...] = _log_sigmoid(_dot(ps, gw2_ref[...]) + gb_ref[...]) * (1.0 / GLA_TAU)
    sm_ref[...] = _log_sigmoid(ps + fb_ref[...])


def _inproj(x, sh, sc, wa, ws, gw2, gb, fb, tm):
    n, d = x.shape
    nt = n // tm
    mrow = sh.shape[1]
    per_tile = sh.shape[0] == nt and mrow == tm
    tiles_per_seq = nt // sh.shape[0] if not per_tile else 1
    mod_map = (lambda i: (i, 0, 0)) if per_tile else (lambda i: (i // tiles_per_seq, 0, 0))
    tok = lambda w: pl.BlockSpec((tm, w), lambda i: (i, 0))
    full = lambda a: pl.BlockSpec(a.shape, lambda i: (0,) * a.ndim)
    bw = BRANCH_WIDTH
    odt = [MXU_DTYPE] + [F32] * (N_PROJ + 2)
    owd = [d] + [bw] * N_PROJ + [bw, LANES]
    return pl.pallas_call(
        _inproj_kernel,
        grid=(nt,),
        in_specs=[tok(d), pl.BlockSpec((1, mrow, d), mod_map), pl.BlockSpec((1, mrow, d), mod_map),
                  full(wa), full(ws), full(gw2), full(gb), full(fb)],
        out_specs=[tok(w) for w in owd],
        out_shape=[jax.ShapeDtypeStruct((n, w), t) for w, t in zip(owd, odt)],
        compiler_params=_cparams("parallel"),
        name="inproj",
    )(x, sh, sc, wa, ws, gw2, gb, fb)


def _gla_kernel(q_ref, k_ref, v_ref, la_ref, gr_ref, gn_ref, s0_ref, o_ref, sT_ref, *, seq, chunk, sub):
    n_chunks = seq // chunk
    n_sub = chunk // sub
    scale = HEAD_DIM ** -0.5
    tril = (_iota((chunk, chunk), 1) <= _iota((chunk, chunk), 0))
    gn = gn_ref[...]

    def one_chunk(c, state):
        r0 = c * chunk if isinstance(c, int) else pl.multiple_of(c * chunk, chunk)
        rows = pl.ds(r0, chunk)
        g = la_ref[rows, :]
        bcum = _dot01_exact(tril, g)
        q = q_ref[rows, :] * scale
        k = k_ref[rows, :]
        v = v_ref[rows, :]
        b_end = bcum[chunk - 1:chunk, :]
        qe = q * jnp.exp(bcum)
        kd = k * jnp.exp(b_end - bcum)
        qs, ks = [], []
        for i in range(n_sub):
            base = bcum[i * sub - 1:i * sub, :] if i > 0 else jnp.zeros_like(b_end)
            nk = (i + 1) * sub
            qs.append(q[i * sub:nk, :] * jnp.exp(bcum[i * sub:nk, :] - base))
            ks.append(k[:nk, :] * jnp.exp(base - bcum[:nk, :]))
        new_state, outs = [], []
        for h in range(N_HEADS):
            sl = slice(h * HEAD_DIM, (h + 1) * HEAD_DIM)
            s_t = state[h]
            o_h = _dot_nt(qe[:, sl], s_t)
            att_rows = []
            for i in range(n_sub):
                nk = (i + 1) * sub
                a = _dot_nt(qs[i][:, sl], ks[i][:, sl])
                causal = _iota((sub, nk), 1) <= _iota((sub, nk), 0) + i * sub
                a = jnp.where(causal, a, 0.0)
                if nk < chunk:
                    a = jnp.concatenate([a, jnp.zeros((sub, chunk - nk), F32)], axis=1)
                att_rows.append(a)
            att = att_rows[0] if n_sub == 1 else jnp.concatenate(att_rows, axis=0)
            o_h = o_h + _dot(att, v[:, sl])
            new_state.append(s_t * jnp.exp(b_end[:, sl]) + _dot_tn(v[:, sl], kd[:, sl]))
            o_h = o_h * lax.rsqrt(jnp.mean(o_h * o_h, axis=-1, keepdims=True) + LN_EPS)
            outs.append(o_h)
        o = jnp.concatenate(outs, axis=1)
        gr = gr_ref[rows, :]
        o_ref[rows, :] = (o * gn * (gr * _sigmoid(gr))).astype(o_ref.dtype)
        return tuple(new_state)

    state0 = tuple(s0_ref[0, h] for h in range(N_HEADS))
    if n_chunks == 1:
        state = one_chunk(0, state0)
    else:
        state = lax.fori_loop(0, n_chunks, one_chunk, state0)
    for h in range(N_HEADS):
        sT_ref[0, h] = state[h]


def _gla(gq, gk, gv, la, gr, gn, s0_t, n_seq, seq):
    chunk = math.gcd(seq, GLA_CHUNK)
    sub = math.gcd(chunk, GLA_SUBCHUNK)
    tok = pl.BlockSpec((seq, BRANCH_WIDTH), lambda b: (b, 0))
    st = pl.BlockSpec((1, N_HEADS, HEAD_DIM, HEAD_DIM), lambda b: (b, 0, 0, 0))
    return pl.pallas_call(
        functools.partial(_gla_kernel, seq=seq, chunk=chunk, sub=sub),
        grid=(n_seq,),
        in_specs=[tok, tok, tok, tok, tok, pl.BlockSpec((1, BRANCH_WIDTH), lambda b: (0, 0)), st],
        out_specs=[tok, st],
        out_shape=[jax.ShapeDtypeStruct((n_seq * seq, BRANCH_WIDTH), F32),
                   jax.ShapeDtypeStruct((n_seq, N_HEADS, HEAD_DIM, HEAD_DIM), F32)],
        compiler_params=_cparams("parallel"),
        name="gla",
    )(gq, gk, gv, la, gr, gn, s0_t)


def _cum_kernel(lf_ref, o_ref):
    rows, length = lf_ref.shape
    triu = _iota((LANES, LANES), 0) <= _iota((LANES, LANES), 1)
    carry = jnp.zeros((rows, 1), F32)
    for c in range(length // LANES):
        cs = _dot_exact01(lf_ref[:, c * LANES:(c + 1) * LANES], triu) + carry
        o_ref[:, c * LANES:(c + 1) * LANES] = cs
        carry = cs[:, LANES - 1:LANES]


def _fox_cumsum(lf_rows):
    return pl.pallas_call(
        _cum_kernel,
        out_shape=jax.ShapeDtypeStruct(lf_rows.shape, F32),
        compiler_params=pltpu.CompilerParams(vmem_limit_bytes=V7X_VMEM_LIMIT),
        name="fox_cumsum",
    )(lf_rows)


def _alibi_slope(h):
    return 2.0 ** (-8.0 * (h + 1) / N_HEADS)


def _attn_prompt_kernel(q_ref, k_ref, v_ref, *rest, seq, moba):
    if moba:
        o_ref, kh_ref, vh_ref, km_ref = rest
    else:
        cum_ref, o_ref, kh_ref, vh_ref = rest
    blk = MOBA_BLOCK
    nb = seq // blk
    i = pl.program_id(1)

    @pl.when(i == 0)
    def _():
        for h in range(N_HEADS):
            sl = slice(h * HEAD_DIM, (h + 1) * HEAD_DIM)
            kf = k_ref[:, sl]
            kh_ref[h] = kf.astype(MXU_DTYPE)
            vh_ref[h] = v_ref[:, sl].astype(MXU_DTYPE)
            if moba:
                km = jnp.sum(kf.reshape(nb, blk, HEAD_DIM), axis=1) * (1.0 / blk)
                pad = km_ref.shape[1] - nb
                km_ref[h] = jnp.concatenate([km, jnp.zeros((pad, HEAD_DIM), F32)], axis=0) if pad else km

    nbp = km_ref.shape[1] if moba else 0
    heads = range(N_HEADS)
    cols = [slice(h * HEAD_DIM, (h + 1) * HEAD_DIM) for h in heads]
    r_minus_c = (_iota((blk, blk), 0) - _iota((blk, blk), 1)).astype(F32)
    causal = r_minus_c >= 0
    q_heads = [(q_ref[:, cols[h]] * (HEAD_DIM ** -0.5)).astype(MXU_DTYPE) for h in heads]

    if moba:
        sel_bias = []
        for h in heads:
            bs = _dot_nt(km_ref[h], q_ref[:, cols[h]])
            jrow = _iota((nbp, blk), 0)
            bs = jnp.where(jrow < i, bs, -jnp.inf)
            rank = jnp.zeros((nbp, blk), F32)
            for j2 in range(nb):
                row = bs[j2:j2 + 1, :]
                beats = (row > bs) | ((row == bs) & (j2 < jrow))
                rank = rank + beats.astype(F32)
            sel = (jrow < i) & (rank < MOBA_TOPK)
            sel_bias.append(jnp.where(sel, 0.0, NEG_BIG).T.astype(MXU_DTYPE))

    def tile_update(state, s, v_blk):
        m, l, acc = state
        m_new = jnp.maximum(m, jnp.max(s, axis=-1, keepdims=True))
        alpha = jnp.exp(m - m_new)
        p = jnp.exp(s - m_new)
        return m_new, alpha * l + jnp.sum(p, axis=-1, keepdims=True), alpha * acc + _dot(p, v_blk)

    def logits(h, c0):
        return _dot_nt(q_heads[h], kh_ref[h, pl.ds(c0, blk), :])

    c_own = pl.multiple_of(i * blk, blk)
    states = []
    for h in heads:
        s = logits(h, c_own)
        s = s - _alibi_slope(h) * r_minus_c if moba else s - cum_ref[0, h:h + 1, pl.ds(c_own, blk)]
        s = jnp.where(causal, s, NEG_BIG)
        m = jnp.max(s, axis=-1, keepdims=True)
        p = jnp.exp(s - m)
        states.append((m, jnp.sum(p, axis=-1, keepdims=True), _dot(p, vh_ref[h, pl.ds(c_own, blk), :])))

    def body(j, states):
        c0 = pl.multiple_of(j * blk, blk)
        new = []
        for h in heads:
            s = logits(h, c0)
            if moba:
                dist = r_minus_c + ((i - j) * blk).astype(F32)
                pick = (_iota((nbp, blk), 0) == j).astype(MXU_DTYPE)
                s = s - _alibi_slope(h) * dist + jnp.dot(sel_bias[h], pick, preferred_element_type=F32)
            else:
                s = s - cum_ref[0, h:h + 1, pl.ds(c0, blk)]
            new.append(tile_update(states[h], s, vh_ref[h, pl.ds(c0, blk), :]))
        return tuple(new)

    states = lax.fori_loop(0, i, body, tuple(states))
    o_ref[...] = jnp.concatenate([acc / l for _, l, acc in states], axis=1).astype(o_ref.dtype)


def _attn_prompt(q, k, v, cum, n_seq, seq, moba):
    blk = MOBA_BLOCK
    nq = seq // blk
    nbp = -(-nq // 8) * 8
    tile = pl.BlockSpec((blk, BRANCH_WIDTH), lambda b, i: (b * nq + i, 0))
    whole = pl.BlockSpec((seq, BRANCH_WIDTH), lambda b, i: (b, 0))
    in_specs = [tile, whole, whole]
    args = [q, k, v]
    scratch = [pltpu.VMEM((N_HEADS, seq, HEAD_DIM), MXU_DTYPE), pltpu.VMEM((N_HEADS, seq, HEAD_DIM), MXU_DTYPE)]
    if moba:
        scratch += [pltpu.VMEM((N_HEADS, nbp, HEAD_DIM), F32)]
    else:
        in_specs.append(pl.BlockSpec((1, N_HEADS, seq), lambda b, i: (b, 0, 0)))
        args.append(cum)
    return pl.pallas_call(
        functools.partial(_attn_prompt_kernel, seq=seq, moba=moba),
        grid=(n_seq, nq),
        in_specs=in_specs,
        out_specs=tile,
        out_shape=jax.ShapeDtypeStruct((n_seq * seq, BRANCH_WIDTH), F32),
        scratch_shapes=scratch,
        compiler_params=_cparams("parallel", "arbitrary"),
        name="moba_prompt" if moba else "fox_prompt",
    )(*args)


def _block_diag_queries(q):
    t = q.shape[0]
    col_head = _iota((t, BRANCH_WIDTH), 1) // HEAD_DIM
    return jnp.concatenate([jnp.where(col_head == h, q, jnp.zeros_like(q)) for h in range(N_HEADS)], axis=0)


def _head_diag(acc, t):
    return jnp.concatenate([acc[h * t:(h + 1) * t, h * HEAD_DIM:(h + 1) * HEAD_DIM] for h in range(N_HEADS)], axis=1)


def _attn_sample_kernel(pt_ref, mq_ref, mkn_ref, mvn_ref, fq_ref, fkn_ref, fvn_ref, lfn_ref, *refs,
                        t_new, n_pages, page, past_len):
    del pt_ref
    mk, mv, fk, fv, lfc = (refs[i * n_pages:(i + 1) * n_pages] for i in range(5))
    ob_ref, oc_ref = refs[5 * n_pages:]
    rows = N_HEADS * t_new
    scale = HEAD_DIM ** -0.5
    row_head = _iota((rows, 1), 0) // t_new
    row_t = _iota((rows, 1), 0) % t_new
    lane_n = _iota((rows, t_new), 1)
    causal = lane_n <= row_t
    cast = lambda r: r[...].astype(MXU_DTYPE)

    def softmax_pv(s_past, s_new, v_pages, v_new):
        m = jnp.maximum(jnp.max(s_past, axis=-1, keepdims=True), jnp.max(s_new, axis=-1, keepdims=True))
        p_past = jnp.exp(s_past - m)
        p_new = jnp.exp(s_new - m)
        l = jnp.sum(p_past, axis=-1, keepdims=True) + jnp.sum(p_new, axis=-1, keepdims=True)
        acc = _dot(p_new, v_new)
        for pg in range(n_pages):
            acc = acc + _dot_nt(p_past[:, pg * page:(pg + 1) * page], cast(v_pages[pg]))
        return _head_diag(acc / l, t_new)

    qm = _block_diag_queries(mq_ref[...]).astype(MXU_DTYPE)
    pages_per_block = MOBA_BLOCK // page
    n_blocks = n_pages // pages_per_block
    kmean_cols = []
    for b in range(n_blocks):
        blk = jnp.concatenate([mk[b * pages_per_block + i][...] for i in range(pages_per_block)], axis=1)
        kmean_cols.append(jnp.sum(blk, axis=1, keepdims=True) * (1.0 / MOBA_BLOCK))
    bs = _dot(qm, jnp.concatenate(kmean_cols, axis=1))
    jl = _iota((rows, n_blocks), 1)
    rank = jnp.zeros((rows, n_blocks), F32)
    for j2 in range(n_blocks):
        col = bs[:, j2:j2 + 1]
        rank = rank + ((col > bs) | ((col == bs) & (j2 < jl))).astype(F32)
    sel = (rank < MOBA_TOPK).astype(F32)
    sel_keys = jnp.concatenate(
        [jnp.broadcast_to(sel[:, b:b + 1], (rows, MOBA_BLOCK)) for b in range(n_blocks)], axis=1) > 0.5
    slope = jnp.zeros((rows, 1), F32)
    for h in range(N_HEADS):
        slope = jnp.where(row_head == h, _alibi_slope(h), slope)
    s = jnp.concatenate([jnp.dot(qm, cast(mk[pg]), preferred_element_type=F32) for pg in range(n_pages)], axis=1)
    dist = (past_len + row_t - _iota((rows, past_len), 1)).astype(F32)
    s = jnp.where(sel_keys, s * scale - slope * dist, NEG_BIG)
    s_n = _dot_nt(qm, mkn_ref[...]) * scale - slope * (row_t - lane_n).astype(F32)
    s_n = jnp.where(causal, s_n, NEG_BIG)
    ob_ref[...] = softmax_pv(s, s_n, mv, mvn_ref[...])

    qf = _block_diag_queries(fq_ref[...]).astype(MXU_DTYPE)
    lf_all = jnp.concatenate([lfc[pg][...] for pg in range(n_pages)], axis=0)
    triu = _iota((page, page), 0) <= _iota((page, page), 1)
    local = _dot_exact01(lf_all, triu)
    nr = n_pages * N_HEADS
    ri, ci = _iota((nr, nr), 0), _iota((nr, nr), 1)
    earlier_page = (ci % N_HEADS == ri % N_HEADS) & (ci // N_HEADS < ri // N_HEADS)
    cum = local + _dot01_exact(earlier_page, local[:, page - 1:page])
    bias = jnp.concatenate(
        [jnp.broadcast_to(cum[pg * N_HEADS:(pg + 1) * N_HEADS, None, :], (N_HEADS, t_new, page)).reshape(rows, page)
         for pg in range(n_pages)], axis=1)
    s = jnp.concatenate([jnp.dot(qf, cast(fk[pg]), preferred_element_type=F32) for pg in range(n_pages)], axis=1)
    s = s * scale - bias
    lfn = lfn_ref[...]
    cols, c = [], cum[nr - N_HEADS:nr, page - 1:page]
    for t in range(t_new):
        c = c + lfn[:, t:t + 1]
        cols.append(c)
    cum_n = jnp.concatenate(cols, axis=1)
    bias_n = jnp.broadcast_to(cum_n[:, None, :], (N_HEADS, t_new, t_new)).reshape(rows, t_new)
    s_n = jnp.where(causal, _dot_nt(qf, fkn_ref[...]) * scale - bias_n, NEG_BIG)
    oc_ref[...] = softmax_pv(s, s_n, fv, fvn_ref[...])


def _attn_sample(page_table, mq, mkn, mvn, fq, fkn, fvn, lfn, mkc, mvc, fkc, fvc, lfc, layer):
    n_seq, t_new, _ = mq.shape
    n_pages = page_table.shape[1]
    page = mkc.shape[3]
    past_len = n_pages * page
    assert past_len % MOBA_BLOCK == 0 and MOBA_BLOCK % page == 0 and t_new <= MOBA_BLOCK
    new = pl.BlockSpec((None, t_new, BRANCH_WIDTH), lambda b, pt: (b, 0, 0))

    def pages(width):
        return [pl.BlockSpec((None, None, width, page), lambda b, pt, pg=pg: (layer, pt[b, pg], 0, 0))
                for pg in range(n_pages)]

    grid_spec = pltpu.PrefetchScalarGridSpec(
        num_scalar_prefetch=1,
        grid=(n_seq,),
        in_specs=[new, new, new, new, new, new, pl.BlockSpec((None, N_HEADS, t_new), lambda b, pt: (b, 0, 0))]
                 + 4 * pages(BRANCH_WIDTH) + pages(N_HEADS),
        out_specs=[new, new],
    )
    return pl.pallas_call(
        functools.partial(_attn_sample_kernel, t_new=t_new, n_pages=n_pages, page=page, past_len=past_len),
        grid_spec=grid_spec,
        out_shape=[jax.ShapeDtypeStruct((n_seq, t_new, BRANCH_WIDTH), F32)] * 2,
        compiler_params=_cparams("parallel"),
        name="attn_sample",
    )(page_table, mq, mkn, mvn, fq, fkn, fvn, lfn,
      *([mkc] * n_pages + [mvc] * n_pages + [fkc] * n_pages + [fvc] * n_pages + [lfc] * n_pages))


def _s5_param_kernel(lre_ref, lim_ref, ldt_ref, bre_ref, bim_ref, are_ref, aim_ref, bbre_ref, bbim_ref):
    lam_re = lre_ref[...]
    lam_im = lim_ref[...]
    dt = jnp.exp(ldt_ref[...])
    mag = jnp.exp(lam_re * dt)
    ab_re = mag * jnp.cos(lam_im * dt)
    ab_im = mag * jnp.sin(lam_im * dt)
    den = lam_re * lam_re + lam_im * lam_im
    z_re = ((ab_re - 1.0) * lam_re + ab_im * lam_im) / den
    z_im = (ab_im * lam_re - (ab_re - 1.0) * lam_im) / den
    are_ref[...] = ab_re
    aim_ref[...] = ab_im
    b_re = bre_ref[...]
    b_im = bim_ref[...]
    bbre_ref[...] = z_re[:, None, :] * b_re - z_im[:, None, :] * b_im
    bbim_ref[...] = z_re[:, None, :] * b_im + z_im[:, None, :] * b_re


def _s5_params(lam_re, lam_im, log_dt, b_re, b_im):
    g, p = lam_re.shape
    i = b_re.shape[-1]
    outs = pl.pallas_call(
        _s5_param_kernel,
        out_shape=[jax.ShapeDtypeStruct((g, p), F32)] * 2 + [jax.ShapeDtypeStruct((g, i, p), F32)] * 2,
        name="s5_params",
    )(lam_re, lam_im, log_dt.reshape(g, 1), b_re.transpose(0, 2, 1), b_im.transpose(0, 2, 1))
    return outs


def _s5_kernel(u_ref, bd_ref, cd_ref, are_ref, aim_ref, d_ref, wg_ref, bg_ref, h0r_ref, h0i_ref,
               o_ref, hr_ref, hi_ref, bu_ref, hs_ref, *, steps, nb):
    ns = S5_WIDTH_STATE

    @pl.when(pl.program_id(0) == 0)
    def _():
        hr_ref[...] = h0r_ref[...]
        hi_ref[...] = h0i_ref[...]

    u = u_ref[...].reshape(steps * nb, BRANCH_WIDTH)
    bu_ref[...] = _dot(u, bd_ref[...])
    a_re = jnp.broadcast_to(are_ref[...], (nb, ns))
    a_im = jnp.broadcast_to(aim_ref[...], (nb, ns))

    def step(t, carry):
        h_re, h_im = carry
        rows = pl.ds(pl.multiple_of(t * nb, nb), nb)
        n_re = a_re * h_re - a_im * h_im + bu_ref[rows, 0:ns]
        n_im = a_re * h_im + a_im * h_re + bu_ref[rows, ns:2 * ns]
        hs_ref[rows, 0:ns] = n_re
        hs_ref[rows, ns:2 * ns] = n_im
        return n_re, n_im

    h_re, h_im = lax.fori_loop(0, steps, step, (hr_ref[...], hi_ref[...]))
    hr_ref[...] = h_re
    hi_ref[...] = h_im
    y = _dot(hs_ref[...], cd_ref[...]) + d_ref[...] * u
    z = _gelu(y)
    o = z * _sigmoid(_dot(z, wg_ref[...]) + bg_ref[...])
    o_ref[...] = o.reshape(steps, nb, BRANCH_WIDTH).astype(o_ref.dtype)


def _s5(u_tb, bd, cd, a_re, a_im, d, wg, bg, h0r, h0i, steps):
    t_len, nb, _ = u_tb.shape
    full = lambda a: pl.BlockSpec(a.shape, lambda i: (0,) * a.ndim)
    blk = pl.BlockSpec((steps, nb, BRANCH_WIDTH), lambda i: (i, 0, 0))
    st = pl.BlockSpec((nb, S5_WIDTH_STATE), lambda i: (0, 0))
    return pl.pallas_call(
        functools.partial(_s5_kernel, steps=steps, nb=nb),
        grid=(t_len // steps,),
        in_specs=[blk, full(bd), full(cd), full(a_re), full(a_im), full(d), full(wg), full(bg), st, st],
        out_specs=[blk, st, st],
        out_shape=[jax.ShapeDtypeStruct((t_len, nb, BRANCH_WIDTH), F32),
                   jax.ShapeDtypeStruct((nb, S5_WIDTH_STATE), F32), jax.ShapeDtypeStruct((nb, S5_WIDTH_STATE), F32)],
        scratch_shapes=[pltpu.VMEM((steps * nb, 2 * S5_WIDTH_STATE), F32),
                        pltpu.VMEM((steps * nb, 2 * S5_WIDTH_STATE), F32)],
        compiler_params=_cparams("arbitrary"),
        name="s5",
    )(u_tb, bd, cd, a_re, a_im, d, wg, bg, h0r, h0i)


def _merge_kernel(x_ref, h1_ref, oa_ref, ob_ref, oc_ref, od_ref, g1_ref, sh2_ref, sc2_ref,
                  wg_ref, wb_ref, wo_ref, lg_ref, lb_ref, x1_ref, h2_ref, h2t_ref, *, alpha):
    d = x_ref.shape[1]
    h1 = h1_ref[...]
    merged = None
    for n, br in enumerate((oa_ref, ob_ref, oc_ref, od_ref)):
        gate = _sigmoid(jnp.dot(h1, wg_ref[:, n * d:(n + 1) * d], preferred_element_type=F32))
        up = _dot(br[...], wb_ref[n])
        merged = gate * up if merged is None else merged + gate * up
    mix = _dot(merged, wo_ref[...])
    x1 = _ln(alpha * x_ref[...] + (1.0 + g1_ref[0]) * mix) * lg_ref[...] + lb_ref[...]
    x1_ref[...] = x1
    h2 = _ln(x1) * (1.0 + sc2_ref[0]) + sh2_ref[0]
    h2_ref[...] = h2.astype(h2_ref.dtype)
    h2t_ref[...] = h2.T.astype(h2t_ref.dtype)


def _merge(x, h1, oa, ob, oc, od, g1, sh2, sc2, wg, wb, wo, lg, lb, tm, alpha):
    n, d = x.shape
    nt = n // tm
    mrow = g1.shape[1]
    per_tile = g1.shape[0] == nt and mrow == tm
    tiles_per_seq = nt // g1.shape[0] if not per_tile else 1
    mod_map = (lambda i: (i, 0, 0)) if per_tile else (lambda i: (i // tiles_per_seq, 0, 0))
    mod = pl.BlockSpec((1, mrow, d), mod_map)
    tok = lambda w: pl.BlockSpec((tm, w), lambda i: (i, 0))
    full = lambda a: pl.BlockSpec(a.shape, lambda i: (0,) * a.ndim)
    bw = BRANCH_WIDTH
    return pl.pallas_call(
        functools.partial(_merge_kernel, alpha=alpha),
        grid=(nt,),
        in_specs=[tok(d), tok(d), tok(bw), tok(bw), tok(bw), tok(bw), mod, mod, mod,
                  full(wg), full(wb), full(wo), full(lg), full(lb)],
        out_specs=[tok(d), tok(d), pl.BlockSpec((d, tm), lambda i: (0, i))],
        out_shape=[jax.ShapeDtypeStruct((n, d), F32), jax.ShapeDtypeStruct((n, d), MXU_DTYPE),
                   jax.ShapeDtypeStruct((d, n), MXU_DTYPE)],
        compiler_params=_cparams("parallel"),
        name="merge",
    )(x, h1, oa, ob, oc, od, g1, sh2, sc2, wg, wb, wo, lg, lb)


def _topk_rows(s, k, rid=None, exact=True):
    if exact and rid is None:
        rid = _iota(s.shape, 0)
    rank = jnp.full(s.shape, float(k), F32)
    vals = []
    for r in range(k):
        m = jnp.max(s, axis=0, keepdims=True)
        if exact:
            idx = jnp.min(jnp.where(s == m, rid, jnp.iinfo(jnp.int32).max), axis=0, keepdims=True)
            hit = rid == idx
        else:
            hit = s == m
        rank = jnp.where(hit, float(r), rank)
        s = jnp.where(hit, -jnp.inf, s)
        vals.append(m)
    taken = jnp.sum((rank < float(k)).astype(F32), axis=0, keepdims=True)
    return jnp.concatenate(vals, axis=0), rank, taken


def _peer_route_kernel(h_ref, wq_ref, sk_ref, cnt_ref, e0_ref, e1_ref, r1_ref, q_ref):
    kk = PEER_TOPK
    half = PEER_DKEY // 2
    q_ref[...] = jnp.dot(h_ref[...], wq_ref[...], preferred_element_type=F32)
    tn = min(PEER_ROUTE_PASS, h_ref.shape[0])
    n_pass = h_ref.shape[0] // tn

    def route(h, toks, exact):
        s, sv, rk = [], [], []
        surplus = jnp.zeros((1, tn), F32)
        for p in range(2):
            c0 = pl.multiple_of(h * PEER_DKEY + p * half, half)
            sp = _dot_nt(sk_ref[p], q_ref[toks, pl.ds(c0, half)])
            v, r, n_taken = _topk_rows(sp, kk, exact=exact)
            surplus = surplus + (n_taken - float(kk))
            s.append(sp)
            sv.append(v)
            rk.append(r)
        grid = sv[0][:, None, :] + sv[1][None, :, :]
        sub = 8
        cand = jnp.concatenate([grid[k0, 0:sub, :] for k0 in range(kk)] + [grid[0, sub:kk, :]], axis=0)
        n_cand = cand.shape[0]
        grp, loc = _iota((n_cand, tn), 0) // sub, _iota((n_cand, tn), 0) % sub
        flat_id = jnp.where(grp < kk, grp * kk + loc, sub + loc)
        _, crank, n_taken = _topk_rows(cand, kk, flat_id, exact=exact)
        surplus = surplus + (n_taken - float(kk))
        taken = crank < float(kk)
        per_grp = jnp.sum(taken.astype(F32).reshape(n_cand // sub, sub, tn), axis=1)
        cnt = per_grp[0:kk, :] + jnp.where(_iota((kk, tn), 0) == 0, per_grp[kk:kk + 1, :], 0.0)
        top = sv[0][0:1, :] + sv[1][0:1, :]
        z = jnp.sum(jnp.where(taken, jnp.exp(cand - top), 0.0), axis=0, keepdims=True)
        cnt_dense = jnp.zeros((PEER_NKEYS, tn), F32)
        for k0 in range(kk):
            cnt_dense = jnp.where(rk[0] == float(k0), cnt[k0:k0 + 1, :], cnt_dense)
        cnt_ref[h, :, toks] = cnt_dense
        e0_ref[h, :, toks] = jnp.exp(s[0] - sv[0][0:1, :])
        e1_ref[h, :, toks] = (jnp.exp(s[1] - sv[1][0:1, :]) / z).astype(e1_ref.dtype)
        r1_ref[h, :, toks] = rk[1].astype(r1_ref.dtype)
        return jnp.max(surplus) > 0.5

    @pl.loop(0, PEER_HEADS * n_pass)
    def _(it):
        h = it // n_pass
        toks = pl.ds(pl.multiple_of((it % n_pass) * tn, tn), tn)
        tie = route(h, toks, exact=False)

        @pl.when(tie)
        def _():
            route(h, toks, exact=True)


def _peer_route(h2, wq, sk, tn):
    n, d = h2.shape
    full = lambda a: pl.BlockSpec(a.shape, lambda i: (0,) * a.ndim)
    out = pl.BlockSpec((PEER_HEADS, PEER_NKEYS, tn), lambda i: (0, 0, i))
    return pl.pallas_call(
        _peer_route_kernel,
        grid=(n // tn,),
        in_specs=[pl.BlockSpec((tn, d), lambda i: (i, 0)), full(wq), full(sk)],
        out_specs=[out] * 4,
        out_shape=[jax.ShapeDtypeStruct((PEER_HEADS, PEER_NKEYS, n), t)
                   for t in (F32, F32, PEER_WEIGHT_DTYPE, PEER_WEIGHT_DTYPE)],
        scratch_shapes=[pltpu.VMEM((tn, PEER_HEADS * PEER_DKEY), F32)],
        compiler_params=_cparams("parallel"),
        name="peer_route",
    )(h2, wq, sk)


PEER_ROUTE_PASS = 256
PEER_STAGE_ROWS = 256
PEER_LANE_CHUNK = 256


def _peer_expert_kernel(ht_ref, x1_ref, g2_ref, cnt_ref, e0_ref, e1_ref, r1_ref, u_ref, vt_ref, lg_ref, lb_ref,
                        o_ref, acc_ref, g_ref, a_ref, *, alpha):
    e = pl.program_id(1)
    te, tn = a_ref.shape
    stage = min(PEER_STAGE_ROWS, te)
    n_stage = te // stage
    lane_chunk = min(PEER_LANE_CHUNK, tn)

    @pl.when(e == 0)
    def _():
        acc_ref[...] = jnp.zeros(acc_ref.shape, F32)

    a_ref[...] = jnp.dot(u_ref[...], ht_ref[...], preferred_element_type=F32)
    wdt = e1_ref.dtype

    def weighted_activation(s):
        for c in range(stage // PEER_NKEYS):
            i0_local = s * (stage // PEER_NKEYS) + c
            rows = slice(i0_local * PEER_NKEYS, (i0_local + 1) * PEER_NKEYS)
            row = pl.ds(e * (te // PEER_NKEYS) + i0_local, 1)
            cnt_rows = [cnt_ref[h, row, :].astype(wdt) for h in range(PEER_HEADS)]
            e0_rows = [e0_ref[h, row, :].astype(wdt) for h in range(PEER_HEADS)]
            for t0 in range(0, tn, lane_chunk):
                lanes = slice(t0, t0 + lane_chunk)
                w = jnp.zeros((PEER_NKEYS, lane_chunk), wdt)
                for h in range(PEER_HEADS):
                    hit = r1_ref[h, :, lanes] < cnt_rows[h][:, lanes]
                    w = w + jnp.where(hit, e1_ref[h, :, lanes], jnp.zeros((), wdt)) * e0_rows[h][:, lanes]
                g_ref[rows, lanes] = (w.astype(F32) * _gelu(a_ref[rows, lanes])).astype(g_ref.dtype)

    def project(s):
        rows = slice(s * stage, (s + 1) * stage)
        acc_ref[...] += jnp.dot(vt_ref[:, rows], g_ref[rows, :], preferred_element_type=F32)

    for s in range(n_stage + 1):
        if s < n_stage:
            weighted_activation(s)
        if s >= 1:
            project(s - 1)

    @pl.when(e == pl.num_programs(1) - 1)
    def _():
        ff = acc_ref[...].T
        o_ref[...] = _ln(alpha * x1_ref[...] + (1.0 + g2_ref[0]) * ff) * lg_ref[...] + lb_ref[...]


def _peer_expert(h2t, x1, g2, cnt, e0, e1, r1, u, vt, lg, lb, tn, te, alpha, layer):
    d, n = h2t.shape
    nt = n // tn
    n_exp = u.shape[1]
    mrow = g2.shape[1]
    per_tile = g2.shape[0] == nt and mrow == tn
    tiles_per_seq = nt // g2.shape[0] if not per_tile else 1
    mod_map = (lambda i, e: (i, 0, 0)) if per_tile else (lambda i, e: (i // tiles_per_seq, 0, 0))
    tok = pl.BlockSpec((tn, d), lambda i, e: (i, 0))
    route = pl.BlockSpec((PEER_HEADS, PEER_NKEYS, tn), lambda i, e: (0, 0, i))
    vec = pl.BlockSpec((1, d), lambda i, e: (0, 0))
    return pl.pallas_call(
        functools.partial(_peer_expert_kernel, alpha=alpha),
        grid=(nt, n_exp // te),
        in_specs=[pl.BlockSpec((d, tn), lambda i, e: (0, i)), tok, pl.BlockSpec((1, mrow, d), mod_map),
                  route, route, route, route,
                  pl.BlockSpec((None, te, d), lambda i, e: (layer, e, 0)),
                  pl.BlockSpec((None, d, te), lambda i, e: (layer, 0, e)), vec, vec],
        out_specs=tok,
        out_shape=jax.ShapeDtypeStruct((n, d), F32),
        scratch_shapes=[pltpu.VMEM((d, tn), F32), pltpu.VMEM((te, tn), MXU_DTYPE), pltpu.VMEM((te, tn), F32)],
        compiler_params=_cparams("parallel", "arbitrary"),
        name="peer_expert",
    )(h2t, x1, g2, cnt, e0, e1, r1, u, vt, lg, lb)


def _block_diag_in(bb_t):
    g, i, p = bb_t.shape
    return jnp.einsum('gip,gh->gihp', bb_t, jnp.eye(g, dtype=bb_t.dtype)).reshape(g * i, g * p)


def _block_diag_out(c):
    g, i, p = c.shape
    return jnp.einsum('gip,gh->gphi', c, jnp.eye(g, dtype=c.dtype)).reshape(g * p, g * i)


def _tile(n, pref):
    return pref if n % pref == 0 else n


def _layer_group(x, mods, w, n_seq, seq, past, layer):
    n, d = x.shape
    bw = BRANCH_WIDTH
    alpha = w['alpha']
    tm = _tile(n, 512)
    tm2 = _tile(n, 256)

    def mod_arr(m, t):
        if seq % t == 0:
            return m.reshape(n_seq, 1, d)
        return jnp.repeat(m, seq, axis=0).reshape(n // t, t, d)

    sh1, sc1, g1, sh2, sc2, g2 = mods
    (h1, gq, gk, gv, gr, mq, mk, mv, fq, fk, fv, su, la, sm) = _inproj(
        x, mod_arr(sh1, tm), mod_arr(sc1, tm), w['wa'], w['ws'], w['gw2'], w['gb'], w['fb'], tm)
    logf = sm[:, GLA_GATE_RANK:GLA_GATE_RANK + N_HEADS]

    s0_t = past['gla'].transpose(0, 1, 3, 2)
    oa, gla_t = _gla(gq, gk, gv, la, gr, w['gn'], s0_t, n_seq, seq)
    gla_state = gla_t.transpose(0, 1, 3, 2)

    if past['paged']:
        r3 = lambda a: a.reshape(n_seq, seq, bw)
        lfn = logf.reshape(n_seq, seq, N_HEADS).transpose(0, 2, 1)
        ob, oc = _attn_sample(past['page_table'], r3(mq), r3(mk), r3(mv), r3(fq), r3(fk), r3(fv), lfn,
                              past['moba_k'], past['moba_v'], past['fox_k'], past['fox_v'], past['fox_logf_t'], layer)
        ob = ob.reshape(n, bw)
        oc = oc.reshape(n, bw)
    else:
        lf_rows = logf.reshape(n_seq, seq, N_HEADS).transpose(0, 2, 1).reshape(n_seq * N_HEADS, seq)
        cum = _fox_cumsum(lf_rows).reshape(n_seq, N_HEADS, seq)
        ob = _attn_prompt(mq, mk, mv, None, n_seq, seq, True)
        oc = _attn_prompt(fq, fk, fv, cum, n_seq, seq, False)

    steps = math.gcd(seq, 128)
    u_tb = su.reshape(n_seq, seq, bw).transpose(1, 0, 2)
    od_tb, s5_re, s5_im = _s5(u_tb, w['bd'], w['cd'], w['a_re'], w['a_im'], w['s5_d'], w['wglu'], w['bglu'],
                              past['s5_re'].reshape(n_seq, S5_WIDTH_STATE), past['s5_im'].reshape(n_seq, S5_WIDTH_STATE),
                              steps)
    od = od_tb.transpose(1, 0, 2).reshape(n, bw)

    x1, h2, h2t = _merge(x, h1, oa, ob, oc, od, mod_arr(g1, tm2), mod_arr(sh2, tm2), mod_arr(sc2, tm2),
                         w['wg'], w['wb'], w['wo'], w['ln1_g'], w['ln1_b'], tm2, alpha)

    cnt, e0, e1, r1 = _peer_route(h2, w['wq'], w['sk'], tm2)
    x2 = _peer_expert(h2t, x1, mod_arr(g2, tm), cnt, e0, e1, r1, w['pu'], w['pvt'], w['ln2_g'], w['ln2_b'],
                      tm, 1024, alpha, layer)

    hd = lambda a: a.reshape(n_seq, seq, N_HEADS, HEAD_DIM)
    new = {'moba_k': hd(mk), 'moba_v': hd(mv), 'fox_k': hd(fk), 'fox_v': hd(fv),
           'fox_logf': logf.reshape(n_seq, seq, N_HEADS), 'gla': gla_state,
           's5_re': s5_re.reshape(n_seq, S5_GROUPS, S5_STATE), 's5_im': s5_im.reshape(n_seq, S5_GROUPS, S5_STATE)}
    return x2, new


def _prep_weights(l, depth, d, w_in, gla_w_gate2, gla_b_gate, gla_norm, fox_b_f, s5_lambda_re, s5_lambda_im, s5_log_dt,
                  s5_b_re, s5_b_im, s5_c_re, s5_c_im, s5_d, s5_w_glu, s5_b_glu, w_branch, w_out, ln1_g, ln1_b,
                  peer_w_q, peer_sub_keys, peer_u, peer_v, ln2_g, ln2_b):
    bw = BRANCH_WIDTH
    bf = lambda a: a.astype(MXU_DTYPE)
    pad_small = LANES - GLA_GATE_RANK - N_HEADS
    w = {}
    w['wa'], w['ws'], w['wg'] = _repack_w_in(w_in, l)
    w['gw2'] = bf(jnp.concatenate([gla_w_gate2[l], jnp.zeros((LANES - GLA_GATE_RANK, bw), F32)], axis=0))
    w['gb'] = gla_b_gate[l].reshape(1, bw)
    w['gn'] = gla_norm[l].reshape(1, bw)
    w['fb'] = jnp.concatenate([jnp.zeros((GLA_GATE_RANK,), F32), fox_b_f[l], jnp.zeros((pad_small,), F32)]).reshape(1, LANES)
    a_re, a_im, bb_re, bb_im = _s5_params(s5_lambda_re[l], s5_lambda_im[l], s5_log_dt[l], s5_b_re[l], s5_b_im[l])
    w['a_re'] = a_re.reshape(1, S5_WIDTH_STATE)
    w['a_im'] = a_im.reshape(1, S5_WIDTH_STATE)
    w['bd'] = bf(jnp.concatenate([_block_diag_in(bb_re), _block_diag_in(bb_im)], axis=1))
    w['cd'] = bf(jnp.concatenate([_block_diag_out(s5_c_re[l]), -_block_diag_out(s5_c_im[l])], axis=0))
    w['s5_d'] = s5_d[l].reshape(1, bw)
    w['wglu'] = bf(s5_w_glu[l])
    w['bglu'] = s5_b_glu[l].reshape(1, bw)
    w['wb'] = bf(w_branch[l])
    w['wo'] = bf(w_out[l])
    w['ln1_g'] = ln1_g[l].reshape(1, d)
    w['ln1_b'] = ln1_b[l].reshape(1, d)
    w['wq'] = bf(peer_w_q[l])
    w['sk'] = bf(peer_sub_keys[l])
    w['pu'] = peer_u
    w['pvt'] = peer_v
    w['ln2_g'] = ln2_g[l].reshape(1, d)
    w['ln2_b'] = ln2_b[l].reshape(1, d)
    w['alpha'] = (2.0 * depth) ** 0.25
    return w


def kernel(x_prompt, x_sample, c_prompt, c_sample, page_table, cache_moba_k, cache_moba_v, cache_fox_k, cache_fox_v, cache_fox_logf, state_gla, state_s5_re, state_s5_im, w_mod, b_mod, w_in, gla_w_gate2, gla_b_gate, gla_norm, fox_b_f, s5_lambda_re, s5_lambda_im, s5_log_dt, s5_b_re, s5_b_im, s5_c_re, s5_c_im, s5_d, s5_w_glu, s5_b_glu, w_branch, w_out, ln1_g, ln1_b, peer_w_q, peer_sub_keys, peer_u, peer_v, ln2_g, ln2_b):
    bp, seq, d = x_prompt.shape
    bs, dseq, _ = x_sample.shape
    depth = w_in.shape[0]
    n_pool, page = cache_moba_k.shape[1], cache_moba_k.shape[2]
    flat_cache = lambda c: c.transpose(0, 1, 3, 4, 2).reshape(depth, n_pool, BRANCH_WIDTH, page)
    paged = {'paged': True, 'page_table': page_table,
             'moba_k': flat_cache(cache_moba_k), 'moba_v': flat_cache(cache_moba_v),
             'fox_k': flat_cache(cache_fox_k), 'fox_v': flat_cache(cache_fox_v),
             'fox_logf_t': cache_fox_logf.transpose(0, 1, 3, 2)}
    c_all = jnp.concatenate([c_prompt, c_sample], axis=0)
    y_p = x_prompt.reshape(bp * seq, d)
    y_s = x_sample.reshape(bs * dseq, d)
    st_p, st_s = [], []
    pu_all = peer_u.astype(MXU_DTYPE)
    pvt_all = peer_v.astype(MXU_DTYPE).transpose(0, 2, 1)
    for l in range(depth):
        w = _prep_weights(l, depth, d, w_in, gla_w_gate2, gla_b_gate, gla_norm, fox_b_f, s5_lambda_re, s5_lambda_im,
                          s5_log_dt, s5_b_re, s5_b_im, s5_c_re, s5_c_im, s5_d, s5_w_glu, s5_b_glu, w_branch, w_out,
                          ln1_g, ln1_b, peer_w_q, peer_sub_keys, pu_all, pvt_all, ln2_g, ln2_b)
        mod = _modulation(c_all, w_mod, b_mod, l)
        mods = jnp.split(mod, 6, axis=-1)
        past_p = {'paged': False, 'gla': jnp.zeros((bp, N_HEADS, HEAD_DIM, HEAD_DIM), F32),
                  's5_re': jnp.zeros((bp, S5_GROUPS, S5_STATE), F32), 's5_im': jnp.zeros((bp, S5_GROUPS, S5_STATE), F32)}
        past_s = dict(paged, gla=state_gla[l], s5_re=state_s5_re[l], s5_im=state_s5_im[l])
        y_p, new_p = _layer_group(y_p, [m[:bp] for m in mods], w, bp, seq, past_p, l)
        y_s, new_s = _layer_group(y_s, [m[bp:] for m in mods], w, bs, dseq, past_s, l)
        st_p.append(new_p)
        st_s.append(new_s)

    stk = lambda states, name: jnp.stack([s[name] for s in states])
    names = ('moba_k', 'moba_v', 'fox_k', 'fox_v', 'fox_logf', 'gla', 's5_re', 's5_im')
    return ((y_p.reshape(bp, seq, d), y_s.reshape(bs, dseq, d))
            + tuple(stk(st_p, nm) for nm in names) + tuple(stk(st_s, nm) for nm in names))
```

```python
import functools
import math

import numpy as np
import jax
import jax.numpy as jnp
from jax import lax
from jax.experimental import pallas as pl
from jax.experimental.pallas import tpu as pltpu

F32 = jnp.float32
MXU_DTYPE = jnp.bfloat16
PEER_WEIGHT_DTYPE = jnp.bfloat16

HEAD_DIM = 64
N_HEADS = 4
BRANCH_WIDTH = N_HEADS * HEAD_DIM
N_BRANCH = 4
GLA_GATE_RANK = 16
GLA_TAU = 16.0
GLA_CHUNK = 64
GLA_SUBCHUNK = 16
GLA_UNROLL = 4
MOBA_BLOCK = 256
MOBA_TOPK = 3
S5_GROUP = 16
S5_GROUPS = BRANCH_WIDTH // S5_GROUP
S5_STATE = 64
S5_WIDTH_STATE = S5_GROUPS * S5_STATE
PEER_HEADS = 8
PEER_DKEY = 256
PEER_NKEYS = 128
PEER_TOPK = 16
LN_EPS = 1e-5
NEG_BIG = -1e30

V7X_VMEM_LIMIT = 56 * 1024 * 1024
LANES = 128


def _cparams(*sem):
    return pltpu.CompilerParams(dimension_semantics=tuple(sem), vmem_limit_bytes=V7X_VMEM_LIMIT)


def _dot(a, b):
    return jnp.dot(a.astype(MXU_DTYPE), b.astype(MXU_DTYPE), preferred_element_type=F32)


def _dot_nt(a, b):
    return lax.dot_general(a.astype(MXU_DTYPE), b.astype(MXU_DTYPE), (((1,), (1,)), ((), ())),
                           preferred_element_type=F32)


def _dot_tn(a, b):
    return lax.dot_general(a.astype(MXU_DTYPE), b.astype(MXU_DTYPE), (((0,), (0,)), ((), ())),
                           preferred_element_type=F32)


def _split3(x):
    hi = x.astype(jnp.bfloat16)
    r1 = x - hi.astype(F32)
    mid = r1.astype(jnp.bfloat16)
    lo = (r1 - mid.astype(F32)).astype(jnp.bfloat16)
    return hi, mid, lo


def _dot_exact01(x, m01):
    hi, mid, lo = _split3(x)
    m = m01.astype(jnp.bfloat16)
    d = lambda a: jnp.dot(a, m, preferred_element_type=F32)
    return d(hi) + d(mid) + d(lo)


def _dot01_exact(m01, x):
    hi, mid, lo = _split3(x)
    m = m01.astype(jnp.bfloat16)
    d = lambda a: jnp.dot(m, a, preferred_element_type=F32)
    return d(hi) + d(mid) + d(lo)


def _ln(x):
    mu = jnp.mean(x, axis=-1, keepdims=True)
    xc = x - mu
    var = jnp.mean(xc * xc, axis=-1, keepdims=True)
    return xc * lax.rsqrt(var + LN_EPS)


def _sigmoid(x):
    return 1.0 / (1.0 + jnp.exp(-x))


def _log_sigmoid(x):
    return jnp.minimum(x, 0.0) - jnp.log1p(jnp.exp(-jnp.abs(x)))


def _gelu(x):
    return 0.5 * x * (1.0 + lax.erf(x * (1.0 / math.sqrt(2.0))))


def _iota(shape, axis):
    return lax.broadcasted_iota(jnp.int32, shape, axis)


def _mod_kernel(c_ref, w_ref, b_ref, o_ref):
    c = c_ref[...]
    o_ref[...] = _dot(c * _sigmoid(c), w_ref[...]) + b_ref[...]


def _modulation(c_all, w_mod, b_mod, layer):
    n, d = c_all.shape
    depth, _, width = w_mod.shape
    tn = _tile(width, 1536)
    return pl.pallas_call(
        _mod_kernel,
        grid=(width // tn,),
        in_specs=[pl.BlockSpec((n, d), lambda j: (0, 0)),
                  pl.BlockSpec((None, d, tn), lambda j: (layer, 0, j)),
                  pl.BlockSpec((None, 1, tn), lambda j: (layer, 0, j))],
        out_specs=pl.BlockSpec((n, tn), lambda j: (0, j)),
        out_shape=jax.ShapeDtypeStruct((n, width), F32),
        compiler_params=_cparams("parallel"),
        name="modulation",
    )(c_all, w_mod, b_mod.reshape(depth, 1, width))


N_PROJ = 11
PROJ_SEGMENTS = (0, 1, 2, 3, 5, 6, 7, 8, 9, 10, 12)
SEG_GATE_A, SEG_GATE_F, SEG_BRANCH_GATE = 4, 11, 13


def _in_offsets(d):
    bw = BRANCH_WIDTH
    sizes = (bw,) * 4 + (GLA_GATE_RANK,) + (bw,) * 3 + (bw,) * 3 + (N_HEADS, bw, N_BRANCH * d)
    return [int(o) for o in np.concatenate([[0], np.cumsum(sizes)])]


def _repack_kernel(w_ref, wa_ref, ws_ref, wg_ref):
    rows, d_gate = wg_ref.shape
    offs = _in_offsets(d_gate // N_BRANCH)
    bw = BRANCH_WIDTH
    for i, seg in enumerate(PROJ_SEGMENTS):
        wa_ref[:, i * bw:(i + 1) * bw] = w_ref[:, offs[seg]:offs[seg] + bw].astype(wa_ref.dtype)
    small = jnp.concatenate([w_ref[:, offs[SEG_GATE_A]:offs[SEG_GATE_A + 1]],
                             w_ref[:, offs[SEG_GATE_F]:offs[SEG_GATE_F + 1]],
                             jnp.zeros((rows, LANES - GLA_GATE_RANK - N_HEADS), F32)], axis=1)
    ws_ref[...] = small.astype(ws_ref.dtype)
    wg_ref[...] = w_ref[:, offs[SEG_BRANCH_GATE]:offs[SEG_BRANCH_GATE + 1]].astype(wg_ref.dtype)


def _repack_w_in(w_in, layer):
    _, d, width = w_in.shape
    tr = _tile(d, 256)
    out_w = (N_PROJ * BRANCH_WIDTH, LANES, N_BRANCH * d)
    return pl.pallas_call(
        _repack_kernel,
        grid=(d // tr,),
        in_specs=[pl.BlockSpec((None, tr, width), lambda i: (layer, i, 0))],
        out_specs=[pl.BlockSpec((tr, w), lambda i: (i, 0)) for w in out_w],
        out_shape=[jax.ShapeDtypeStruct((d, w), MXU_DTYPE) for w in out_w],
        compiler_params=_cparams("parallel"),
        name="repack_w_in",
    )(w_in)


def _inproj_kernel(x_ref, sh_ref, sc_ref, wa_ref, ws_ref, gw2_ref, gb_ref, fb_ref,
                   h1_ref, gq_ref, gk_ref, gv_ref, gr_ref, mq_ref, mk_ref, mv_ref, fq_ref, fk_ref, fv_ref,
                   su_ref, la_ref, sm_ref):
    h1 = (_ln(x_ref[...]) * (1.0 + sc_ref[0]) + sh_ref[0]).astype(MXU_DTYPE)
    h1_ref[...] = h1
    outs = (gq_ref, gk_ref, gv_ref, gr_ref, mq_ref, mk_ref, mv_ref, fq_ref, fk_ref, fv_ref, su_ref)
    for i, r in enumerate(outs):
        r[...] = jnp.dot(h1, wa_ref[:, i * BRANCH_WIDTH:(i + 1) * BRANCH_WIDTH],
                         preferred_element_type=F32).astype(r.dtype)
    ps = jnp.dot(h1, ws_ref[...], preferred_element_type=F32)
    la_ref[...] = _log_sigmoid(_dot(ps, gw2_ref[...]) + gb_ref[...]) * (1.0 / GLA_TAU)
    sm_ref[...] = _log_sigmoid(ps + fb_ref[...])


def _inproj(x, sh, sc, wa, ws, gw2, gb, fb, tm):
    n, d = x.shape
    nt = n // tm
    mrow = sh.shape[1]
    per_tile = sh.shape[0] == nt and mrow == tm
    tiles_per_seq = nt // sh.shape[0] if not per_tile else 1
    mod_map = (lambda i: (i, 0, 0)) if per_tile else (lambda i: (i // tiles_per_seq, 0, 0))
    tok = lambda w: pl.BlockSpec((tm, w), lambda i: (i, 0))
    full = lambda a: pl.BlockSpec(a.shape, lambda i: (0,) * a.ndim)
    bw = BRANCH_WIDTH
    odt = [MXU_DTYPE] + [F32] * (N_PROJ + 2)
    owd = [d] + [bw] * N_PROJ + [bw, LANES]
    return pl.pallas_call(
        _inproj_kernel,
        grid=(nt,),
        in_specs=[tok(d), pl.BlockSpec((1, mrow, d), mod_map), pl.BlockSpec((1, mrow, d), mod_map),
                  full(wa), full(ws), full(gw2), full(gb), full(fb)],
        out_specs=[tok(w) for w in owd],
        out_shape=[jax.ShapeDtypeStruct((n, w), t) for w, t in zip(owd, odt)],
        compiler_params=_cparams("parallel"),
        name="inproj",
    )(x, sh, sc, wa, ws, gw2, gb, fb)


def _gla_kernel(q_ref, k_ref, v_ref, la_ref, gr_ref, gn_ref, s0_ref, o_ref, sT_ref, *, seq, chunk, sub):
    n_chunks = seq // chunk
    n_sub = chunk // sub
    scale = HEAD_DIM ** -0.5
    tril = (_iota((chunk, chunk), 1) <= _iota((chunk, chunk), 0))
    gn = gn_ref[...]

    def one_chunk(c, state):
        r0 = c * chunk if isinstance(c, int) else pl.multiple_of(c * chunk, chunk)
        rows = pl.ds(r0, chunk)
        g = la_ref[rows, :]
        bcum = _dot01_exact(tril, g)
        q = q_ref[rows, :] * scale
        k = k_ref[rows, :]
        v = v_ref[rows, :]
        b_end = bcum[chunk - 1:chunk, :]
        qe = q * jnp.exp(bcum)
        kd = k * jnp.exp(b_end - bcum)
        qs, ks = [], []
        for i in range(n_sub):
            base = bcum[i * sub - 1:i * sub, :] if i > 0 else jnp.zeros_like(b_end)
            nk = (i + 1) * sub
            qs.append(q[i * sub:nk, :] * jnp.exp(bcum[i * sub:nk, :] - base))
            ks.append(k[:nk, :] * jnp.exp(base - bcum[:nk, :]))
        new_state, outs = [], []
        for h in range(N_HEADS):
            sl = slice(h * HEAD_DIM, (h + 1) * HEAD_DIM)
            s_t = state[h]
            o_h = _dot_nt(qe[:, sl], s_t)
            att_rows = []
            for i in range(n_sub):
                nk = (i + 1) * sub
                a = _dot_nt(qs[i][:, sl], ks[i][:, sl])
                causal = _iota((sub, nk), 1) <= _iota((sub, nk), 0) + i * sub
                a = jnp.where(causal, a, 0.0)
                if nk < chunk:
                    a = jnp.concatenate([a, jnp.zeros((sub, chunk - nk), F32)], axis=1)
                att_rows.append(a)
            att = att_rows[0] if n_sub == 1 else jnp.concatenate(att_rows, axis=0)
            o_h = o_h + _dot(att, v[:, sl])
            new_state.append(s_t * jnp.exp(b_end[:, sl]) + _dot_tn(v[:, sl], kd[:, sl]))
            o_h = o_h * lax.rsqrt(jnp.mean(o_h * o_h, axis=-1, keepdims=True) + LN_EPS)
            outs.append(o_h)
        o = jnp.concatenate(outs, axis=1)
        gr = gr_ref[rows, :]
        o_ref[rows, :] = (o * gn * (gr * _sigmoid(gr))).astype(o_ref.dtype)
        return tuple(new_state)

    state0 = tuple(s0_ref[0, h] for h in range(N_HEADS))
    if n_chunks == 1:
        state = one_chunk(0, state0)
    else:
        state = lax.fori_loop(0, n_chunks, one_chunk, state0, unroll=math.gcd(n_chunks, GLA_UNROLL))
    for h in range(N_HEADS):
        sT_ref[0, h] = state[h]


def _gla(gq, gk, gv, la, gr, gn, s0_t, n_seq, seq):
    chunk = math.gcd(seq, GLA_CHUNK)
    sub = math.gcd(chunk, GLA_SUBCHUNK)
    tok = pl.BlockSpec((seq, BRANCH_WIDTH), lambda b: (b, 0))
    st = pl.BlockSpec((1, N_HEADS, HEAD_DIM, HEAD_DIM), lambda b: (b, 0, 0, 0))
    return pl.pallas_call(
        functools.partial(_gla_kernel, seq=seq, chunk=chunk, sub=sub),
        grid=(n_seq,),
        in_specs=[tok, tok, tok, tok, tok, pl.BlockSpec((1, BRANCH_WIDTH), lambda b: (0, 0)), st],
        out_specs=[tok, st],
        out_shape=[jax.ShapeDtypeStruct((n_seq * seq, BRANCH_WIDTH), F32),
                   jax.ShapeDtypeStruct((n_seq, N_HEADS, HEAD_DIM, HEAD_DIM), F32)],
        compiler_params=_cparams("parallel"),
        name="gla",
    )(gq, gk, gv, la, gr, gn, s0_t)


def _cum_kernel(lf_ref, o_ref):
    rows, length = lf_ref.shape
    triu = _iota((LANES, LANES), 0) <= _iota((LANES, LANES), 1)
    carry = jnp.zeros((rows, 1), F32)
    for c in range(length // LANES):
        cs = _dot_exact01(lf_ref[:, c * LANES:(c + 1) * LANES], triu) + carry
        o_ref[:, c * LANES:(c + 1) * LANES] = cs
        carry = cs[:, LANES - 1:LANES]


def _fox_cumsum(lf_rows):
    return pl.pallas_call(
        _cum_kernel,
        out_shape=jax.ShapeDtypeStruct(lf_rows.shape, F32),
        compiler_params=pltpu.CompilerParams(vmem_limit_bytes=V7X_VMEM_LIMIT),
        name="fox_cumsum",
    )(lf_rows)


def _alibi_slope(h):
    return 2.0 ** (-8.0 * (h + 1) / N_HEADS)


def _attn_prompt_kernel(q_ref, k_ref, v_ref, *rest, seq, moba):
    if moba:
        o_ref, kx_ref, vt_ref, km_ref = rest
    else:
        cum_ref, o_ref, kx_ref, vt_ref = rest
    blk = MOBA_BLOCK
    nb = seq // blk
    i = pl.program_id(1)
    nbp = km_ref.shape[1] if moba else 0
    heads = range(N_HEADS)
    cols = [slice(h * HEAD_DIM, (h + 1) * HEAD_DIM) for h in heads]
    n_feat = kx_ref.shape[2] - HEAD_DIM

    @pl.when(i == 0)
    def _():
        pos = _iota((seq, n_feat), 0)
        slot = _iota((seq, n_feat), 1)
        for h in heads:
            kf = k_ref[:, cols[h]]
            if moba:
                km = jnp.sum(kf.reshape(nb, blk, HEAD_DIM), axis=1) * (1.0 / blk)
                km_ref[h] = jnp.concatenate([km, jnp.zeros((nbp - nb, HEAD_DIM), F32)], axis=0) if nbp > nb else km
                feat = jnp.where(slot < nbp, (slot == pos // blk).astype(F32),
                                 jnp.where(slot == nbp, (pos // blk).astype(F32),
                                           jnp.where(slot == nbp + 1, (pos % blk).astype(F32), 0.0)))
            else:
                hi, mid, lo = (p.astype(F32) for p in _split3(cum_ref[0, h:h + 1, :]))
                part = _iota((n_feat, seq), 0)
                feat = jnp.where(part == 0, hi, jnp.where(part == 1, mid, jnp.where(part == 2, lo, 0.0))).T
            kx_ref[h, :, 0:HEAD_DIM] = kf.astype(MXU_DTYPE)
            kx_ref[h, :, HEAD_DIM:HEAD_DIM + n_feat] = feat.astype(MXU_DTYPE)
            vt_ref[h] = v_ref[:, cols[h]].T.astype(MXU_DTYPE)

    q_t = q_ref[...].T
    qx = []
    for h in heads:
        q_h = q_t[h * HEAD_DIM:(h + 1) * HEAD_DIM, :]
        if moba:
            bs = _dot(km_ref[h], q_h)
            jrow = _iota((nbp, blk), 0)
            bs = jnp.where(jrow < i, bs, -jnp.inf)
            rank = jnp.zeros((nbp, blk), F32)
            for j2 in range(nb):
                row = bs[j2:j2 + 1, :]
                beats = (row > bs) | ((row == bs) & (j2 < jrow))
                rank = rank + beats.astype(F32)
            sel = ((jrow < i) & (rank < MOBA_TOPK)) | (jrow == i)
            frow = _iota((n_feat - nbp, blk), 0)
            slope = _alibi_slope(h)
            extra = jnp.concatenate([jnp.where(sel, 0.0, NEG_BIG),
                                     jnp.where(frow == 0, slope * blk, jnp.where(frow == 1, slope, 0.0))], axis=0)
        else:
            extra = jnp.where(_iota((n_feat, blk), 0) < 3, -1.0, 0.0)
        qx.append(jnp.concatenate([q_h * (HEAD_DIM ** -0.5), extra], axis=0).astype(MXU_DTYPE))

    def scores(h, c0):
        return jnp.dot(kx_ref[h, pl.ds(c0, blk), :], qx[h], preferred_element_type=F32)

    def block_update(states, c0, mask):
        s_all = [scores(h, c0) for h in heads]
        if mask is not None:
            s_all = [jnp.where(mask, s_t, NEG_BIG) for s_t in s_all]
        parts = []
        for h in heads:
            m, l, acc = states[h]
            m_new = jnp.maximum(m, jnp.max(s_all[h], axis=0, keepdims=True))
            alpha = jnp.exp(m - m_new)
            p = jnp.exp(s_all[h] - m_new)
            parts.append((m_new, alpha, alpha * l + jnp.sum(p, axis=0, keepdims=True), p.astype(MXU_DTYPE)))
        return tuple((m_new, l_new, alpha * states[h][2] + _dot(vt_ref[h, :, pl.ds(c0, blk)], p))
                     for h, (m_new, alpha, l_new, p) in zip(heads, parts))

    causal_t = _iota((blk, blk), 0) <= _iota((blk, blk), 1)
    init = tuple((jnp.full((1, blk), NEG_BIG, F32), jnp.zeros((1, blk), F32), jnp.zeros((HEAD_DIM, blk), F32))
                 for _ in heads)
    states = block_update(init, pl.multiple_of(i * blk, blk), causal_t)
    states = lax.fori_loop(0, i, lambda j, st: block_update(st, pl.multiple_of(j * blk, blk), None), states)
    o_ref[...] = jnp.concatenate([acc / l for _, l, acc in states], axis=0).T.astype(o_ref.dtype)


def _attn_prompt(q, k, v, cum, n_seq, seq, moba):
    blk = MOBA_BLOCK
    nq = seq // blk
    nbp = -(-nq // 8) * 8
    tile = pl.BlockSpec((blk, BRANCH_WIDTH), lambda b, i: (b * nq + i, 0))
    whole = pl.BlockSpec((seq, BRANCH_WIDTH), lambda b, i: (b, 0))
    in_specs = [tile, whole, whole]
    args = [q, k, v]
    scratch = [pltpu.VMEM((N_HEADS, seq, 2 * HEAD_DIM), MXU_DTYPE), pltpu.VMEM((N_HEADS, HEAD_DIM, seq), MXU_DTYPE)]
    if moba:
        scratch += [pltpu.VMEM((N_HEADS, nbp, HEAD_DIM), F32)]
    else:
        in_specs.append(pl.BlockSpec((1, N_HEADS, seq), lambda b, i: (b, 0, 0)))
        args.append(cum)
    return pl.pallas_call(
        functools.partial(_attn_prompt_kernel, seq=seq, moba=moba),
        grid=(n_seq, nq),
        in_specs=in_specs,
        out_specs=tile,
        out_shape=jax.ShapeDtypeStruct((n_seq * seq, BRANCH_WIDTH), F32),
        scratch_shapes=scratch,
        compiler_params=_cparams("parallel", "arbitrary"),
        name="moba_prompt" if moba else "fox_prompt",
    )(*args)


def _block_diag_queries(q):
    t = q.shape[0]
    col_head = _iota((t, BRANCH_WIDTH), 1) // HEAD_DIM
    return jnp.concatenate([jnp.where(col_head == h, q, jnp.zeros_like(q)) for h in range(N_HEADS)], axis=0)


def _head_diag(acc, t):
    return jnp.concatenate([acc[h * t:(h + 1) * t, h * HEAD_DIM:(h + 1) * HEAD_DIM] for h in range(N_HEADS)], axis=1)


def _attn_sample_kernel(pt_ref, mq_ref, mkn_ref, mvn_ref, fq_ref, fkn_ref, fvn_ref, lfn_ref, *refs,
                        t_new, n_pages, page, past_len):
    del pt_ref
    mk, mv, fk, fv, lfc = (refs[i * n_pages:(i + 1) * n_pages] for i in range(5))
    ob_ref, oc_ref = refs[5 * n_pages:]
    rows = N_HEADS * t_new
    scale = HEAD_DIM ** -0.5
    row_head = _iota((rows, 1), 0) // t_new
    row_t = _iota((rows, 1), 0) % t_new
    lane_n = _iota((rows, t_new), 1)
    causal = lane_n <= row_t
    cast = lambda r: r[...].astype(MXU_DTYPE)

    def softmax_pv(s_past, s_new, v_pages, v_new):
        m = jnp.maximum(jnp.max(s_past, axis=-1, keepdims=True), jnp.max(s_new, axis=-1, keepdims=True))
        p_past = jnp.exp(s_past - m)
        p_new = jnp.exp(s_new - m)
        l = jnp.sum(p_past, axis=-1, keepdims=True) + jnp.sum(p_new, axis=-1, keepdims=True)
        acc = _dot(p_new, v_new)
        for pg in range(n_pages):
            acc = acc + _dot_nt(p_past[:, pg * page:(pg + 1) * page], cast(v_pages[pg]))
        return _head_diag(acc / l, t_new)

    qm = _block_diag_queries(mq_ref[...]).astype(MXU_DTYPE)
    pages_per_block = MOBA_BLOCK // page
    n_blocks = n_pages // pages_per_block
    kmean_cols = []
    for b in range(n_blocks):
        blk = jnp.concatenate([mk[b * pages_per_block + i][...] for i in range(pages_per_block)], axis=1)
        kmean_cols.append(jnp.sum(blk, axis=1, keepdims=True) * (1.0 / MOBA_BLOCK))
    bs = _dot(qm, jnp.concatenate(kmean_cols, axis=1))
    jl = _iota((rows, n_blocks), 1)
    rank = jnp.zeros((rows, n_blocks), F32)
    for j2 in range(n_blocks):
        col = bs[:, j2:j2 + 1]
        rank = rank + ((col > bs) | ((col == bs) & (j2 < jl))).astype(F32)
    sel = (rank < MOBA_TOPK).astype(F32)
    sel_keys = jnp.concatenate(
        [jnp.broadcast_to(sel[:, b:b + 1], (rows, MOBA_BLOCK)) for b in range(n_blocks)], axis=1) > 0.5
    slope = jnp.zeros((rows, 1), F32)
    for h in range(N_HEADS):
        slope = jnp.where(row_head == h, _alibi_slope(h), slope)
    s = jnp.concatenate([jnp.dot(qm, cast(mk[pg]), preferred_element_type=F32) for pg in range(n_pages)], axis=1)
    dist = (past_len + row_t - _iota((rows, past_len), 1)).astype(F32)
    s = jnp.where(sel_keys, s * scale - slope * dist, NEG_BIG)
    s_n = _dot_nt(qm, mkn_ref[...]) * scale - slope * (row_t - lane_n).astype(F32)
    s_n = jnp.where(causal, s_n, NEG_BIG)
    ob_ref[...] = softmax_pv(s, s_n, mv, mvn_ref[...])

    qf = _block_diag_queries(fq_ref[...]).astype(MXU_DTYPE)
    lf_all = jnp.concatenate([lfc[pg][...] for pg in range(n_pages)], axis=0)
    triu = _iota((page, page), 0) <= _iota((page, page), 1)
    local = _dot_exact01(lf_all, triu)
    nr = n_pages * N_HEADS
    ri, ci = _iota((nr, nr), 0), _iota((nr, nr), 1)
    earlier_page = (ci % N_HEADS == ri % N_HEADS) & (ci // N_HEADS < ri // N_HEADS)
    cum = local + _dot01_exact(earlier_page, local[:, page - 1:page])
    bias = jnp.concatenate(
        [jnp.broadcast_to(cum[pg * N_HEADS:(pg + 1) * N_HEADS, None, :], (N_HEADS, t_new, page)).reshape(rows, page)
         for pg in range(n_pages)], axis=1)
    s = jnp.concatenate([jnp.dot(qf, cast(fk[pg]), preferred_element_type=F32) for pg in range(n_pages)], axis=1)
    s = s * scale - bias
    lfn = lfn_ref[...]
    cols, c = [], cum[nr - N_HEADS:nr, page - 1:page]
    for t in range(t_new):
        c = c + lfn[:, t:t + 1]
        cols.append(c)
    cum_n = jnp.concatenate(cols, axis=1)
    bias_n = jnp.broadcast_to(cum_n[:, None, :], (N_HEADS, t_new, t_new)).reshape(rows, t_new)
    s_n = jnp.where(causal, _dot_nt(qf, fkn_ref[...]) * scale - bias_n, NEG_BIG)
    oc_ref[...] = softmax_pv(s, s_n, fv, fvn_ref[...])


def _attn_sample(page_table, mq, mkn, mvn, fq, fkn, fvn, lfn, mkc, mvc, fkc, fvc, lfc, layer):
    n_seq, t_new, _ = mq.shape
    n_pages = page_table.shape[1]
    page = mkc.shape[3]
    past_len = n_pages * page
    assert past_len % MOBA_BLOCK == 0 and MOBA_BLOCK % page == 0 and t_new <= MOBA_BLOCK
    new = pl.BlockSpec((None, t_new, BRANCH_WIDTH), lambda b, pt: (b, 0, 0))

    def pages(width):
        return [pl.BlockSpec((None, None, width, page), lambda b, pt, pg=pg: (layer, pt[b, pg], 0, 0))
                for pg in range(n_pages)]

    grid_spec = pltpu.PrefetchScalarGridSpec(
        num_scalar_prefetch=1,
        grid=(n_seq,),
        in_specs=[new, new, new, new, new, new, pl.BlockSpec((None, N_HEADS, t_new), lambda b, pt: (b, 0, 0))]
                 + 4 * pages(BRANCH_WIDTH) + pages(N_HEADS),
        out_specs=[new, new],
    )
    return pl.pallas_call(
        functools.partial(_attn_sample_kernel, t_new=t_new, n_pages=n_pages, page=page, past_len=past_len),
        grid_spec=grid_spec,
        out_shape=[jax.ShapeDtypeStruct((n_seq, t_new, BRANCH_WIDTH), F32)] * 2,
        compiler_params=_cparams("parallel"),
        name="attn_sample",
    )(page_table, mq, mkn, mvn, fq, fkn, fvn, lfn,
      *([mkc] * n_pages + [mvc] * n_pages + [fkc] * n_pages + [fvc] * n_pages + [lfc] * n_pages))


def _s5_param_kernel(lre_ref, lim_ref, ldt_ref, bre_ref, bim_ref, are_ref, aim_ref, bbre_ref, bbim_ref):
    lam_re = lre_ref[...]
    lam_im = lim_ref[...]
    dt = jnp.exp(ldt_ref[...])
    mag = jnp.exp(lam_re * dt)
    ab_re = mag * jnp.cos(lam_im * dt)
    ab_im = mag * jnp.sin(lam_im * dt)
    den = lam_re * lam_re + lam_im * lam_im
    z_re = ((ab_re - 1.0) * lam_re + ab_im * lam_im) / den
    z_im = (ab_im * lam_re - (ab_re - 1.0) * lam_im) / den
    are_ref[...] = ab_re
    aim_ref[...] = ab_im
    b_re = bre_ref[...]
    b_im = bim_ref[...]
    bbre_ref[...] = z_re[:, None, :] * b_re - z_im[:, None, :] * b_im
    bbim_ref[...] = z_re[:, None, :] * b_im + z_im[:, None, :] * b_re


def _s5_params(lam_re, lam_im, log_dt, b_re, b_im):
    g, p = lam_re.shape
    i = b_re.shape[-1]
    outs = pl.pallas_call(
        _s5_param_kernel,
        out_shape=[jax.ShapeDtypeStruct((g, p), F32)] * 2 + [jax.ShapeDtypeStruct((g, i, p), F32)] * 2,
        name="s5_params",
    )(lam_re, lam_im, log_dt.reshape(g, 1), b_re.transpose(0, 2, 1), b_im.transpose(0, 2, 1))
    return outs


def _s5_kernel(u_ref, bd_ref, cd_ref, are_ref, aim_ref, d_ref, wg_ref, bg_ref, h0r_ref, h0i_ref,
               o_ref, hr_ref, hi_ref, bu_ref, hs_ref, *, steps, nb):
    ns = S5_WIDTH_STATE

    @pl.when(pl.program_id(0) == 0)
    def _():
        hr_ref[...] = h0r_ref[...]
        hi_ref[...] = h0i_ref[...]

    u = u_ref[...].reshape(steps * nb, BRANCH_WIDTH)
    bu_ref[...] = _dot(u, bd_ref[...])
    a_re = jnp.broadcast_to(are_ref[...], (nb, ns))
    a_im = jnp.broadcast_to(aim_ref[...], (nb, ns))

    def step(t, carry):
        h_re, h_im = carry
        rows = pl.ds(pl.multiple_of(t * nb, nb), nb)
        n_re = a_re * h_re - a_im * h_im + bu_ref[rows, 0:ns]
        n_im = a_re * h_im + a_im * h_re + bu_ref[rows, ns:2 * ns]
        hs_ref[rows, 0:ns] = n_re
        hs_ref[rows, ns:2 * ns] = n_im
        return n_re, n_im

    h_re, h_im = lax.fori_loop(0, steps, step, (hr_ref[...], hi_ref[...]))
    hr_ref[...] = h_re
    hi_ref[...] = h_im
    y = _dot(hs_ref[...], cd_ref[...]) + d_ref[...] * u
    z = _gelu(y)
    o = z * _sigmoid(_dot(z, wg_ref[...]) + bg_ref[...])
    o_ref[...] = o.reshape(steps, nb, BRANCH_WIDTH).astype(o_ref.dtype)


def _s5(u_tb, bd, cd, a_re, a_im, d, wg, bg, h0r, h0i, steps):
    t_len, nb, _ = u_tb.shape
    full = lambda a: pl.BlockSpec(a.shape, lambda i: (0,) * a.ndim)
    blk = pl.BlockSpec((steps, nb, BRANCH_WIDTH), lambda i: (i, 0, 0))
    st = pl.BlockSpec((nb, S5_WIDTH_STATE), lambda i: (0, 0))
    return pl.pallas_call(
        functools.partial(_s5_kernel, steps=steps, nb=nb),
        grid=(t_len // steps,),
        in_specs=[blk, full(bd), full(cd), full(a_re), full(a_im), full(d), full(wg), full(bg), st, st],
        out_specs=[blk, st, st],
        out_shape=[jax.ShapeDtypeStruct((t_len, nb, BRANCH_WIDTH), F32),
                   jax.ShapeDtypeStruct((nb, S5_WIDTH_STATE), F32), jax.ShapeDtypeStruct((nb, S5_WIDTH_STATE), F32)],
        scratch_shapes=[pltpu.VMEM((steps * nb, 2 * S5_WIDTH_STATE), F32),
                        pltpu.VMEM((steps * nb, 2 * S5_WIDTH_STATE), F32)],
        compiler_params=_cparams("arbitrary"),
        name="s5",
    )(u_tb, bd, cd, a_re, a_im, d, wg, bg, h0r, h0i)


def _merge_kernel(x_ref, h1_ref, oa_ref, ob_ref, oc_ref, od_ref, g1_ref, sh2_ref, sc2_ref,
                  wg_ref, wb_ref, wo_ref, lg_ref, lb_ref, x1_ref, h2_ref, h2t_ref, *, alpha):
    d = x_ref.shape[1]
    h1 = h1_ref[...]
    merged = None
    for n, br in enumerate((oa_ref, ob_ref, oc_ref, od_ref)):
        gate = _sigmoid(jnp.dot(h1, wg_ref[:, n * d:(n + 1) * d], preferred_element_type=F32))
        up = _dot(br[...], wb_ref[n])
        merged = gate * up if merged is None else merged + gate * up
    mix = _dot(merged, wo_ref[...])
    x1 = _ln(alpha * x_ref[...] + (1.0 + g1_ref[0]) * mix) * lg_ref[...] + lb_ref[...]
    x1_ref[...] = x1
    h2 = _ln(x1) * (1.0 + sc2_ref[0]) + sh2_ref[0]
    h2_ref[...] = h2.astype(h2_ref.dtype)
    h2t_ref[...] = h2.T.astype(h2t_ref.dtype)


def _merge(x, h1, oa, ob, oc, od, g1, sh2, sc2, wg, wb, wo, lg, lb, tm, alpha):
    n, d = x.shape
    nt = n // tm
    mrow = g1.shape[1]
    per_tile = g1.shape[0] == nt and mrow == tm
    tiles_per_seq = nt // g1.shape[0] if not per_tile else 1
    mod_map = (lambda i: (i, 0, 0)) if per_tile else (lambda i: (i // tiles_per_seq, 0, 0))
    mod = pl.BlockSpec((1, mrow, d), mod_map)
    tok = lambda w: pl.BlockSpec((tm, w), lambda i: (i, 0))
    full = lambda a: pl.BlockSpec(a.shape, lambda i: (0,) * a.ndim)
    bw = BRANCH_WIDTH
    return pl.pallas_call(
        functools.partial(_merge_kernel, alpha=alpha),
        grid=(nt,),
        in_specs=[tok(d), tok(d), tok(bw), tok(bw), tok(bw), tok(bw), mod, mod, mod,
                  full(wg), full(wb), full(wo), full(lg), full(lb)],
        out_specs=[tok(d), tok(d), pl.BlockSpec((d, tm), lambda i: (0, i))],
        out_shape=[jax.ShapeDtypeStruct((n, d), F32), jax.ShapeDtypeStruct((n, d), MXU_DTYPE),
                   jax.ShapeDtypeStruct((d, n), MXU_DTYPE)],
        compiler_params=_cparams("parallel"),
        name="merge",
    )(x, h1, oa, ob, oc, od, g1, sh2, sc2, wg, wb, wo, lg, lb)


def _topk_rows(s, k, rid=None, exact=True):
    if exact and rid is None:
        rid = _iota(s.shape, 0)
    rank = jnp.full(s.shape, float(k), F32)
    vals = []
    for r in range(k):
        m = jnp.max(s, axis=0, keepdims=True)
        if exact:
            idx = jnp.min(jnp.where(s == m, rid, jnp.iinfo(jnp.int32).max), axis=0, keepdims=True)
            hit = rid == idx
        else:
            hit = s == m
        rank = jnp.where(hit, float(r), rank)
        s = jnp.where(hit, -jnp.inf, s)
        vals.append(m)
    taken = jnp.sum((rank < float(k)).astype(F32), axis=0, keepdims=True)
    return jnp.concatenate(vals, axis=0), rank, taken


def _peer_route_kernel(h_ref, wq_ref, sk_ref, cnt_ref, e0_ref, e1_ref, r1_ref, q_ref):
    kk = PEER_TOPK
    half = PEER_DKEY // 2
    q_ref[...] = jnp.dot(h_ref[...], wq_ref[...], preferred_element_type=F32)
    tn = min(PEER_ROUTE_PASS, h_ref.shape[0])
    n_pass = h_ref.shape[0] // tn

    def route(h, toks, exact):
        s, sv, rk = [], [], []
        surplus = jnp.zeros((1, tn), F32)
        for p in range(2):
            c0 = pl.multiple_of(h * PEER_DKEY + p * half, half)
            sp = _dot_nt(sk_ref[p], q_ref[toks, pl.ds(c0, half)])
            v, r, n_taken = _topk_rows(sp, kk, exact=exact)
            surplus = surplus + (n_taken - float(kk))
            s.append(sp)
            sv.append(v)
            rk.append(r)
        grid = sv[0][:, None, :] + sv[1][None, :, :]
        sub = 8
        cand = jnp.concatenate([grid[k0, 0:sub, :] for k0 in range(kk)] + [grid[0, sub:kk, :]], axis=0)
        n_cand = cand.shape[0]
        grp, loc = _iota((n_cand, tn), 0) // sub, _iota((n_cand, tn), 0) % sub
        flat_id = jnp.where(grp < kk, grp * kk + loc, sub + loc)
        _, crank, n_taken = _topk_rows(cand, kk, flat_id, exact=exact)
        surplus = surplus + (n_taken - float(kk))
        taken = crank < float(kk)
        per_grp = jnp.sum(taken.astype(F32).reshape(n_cand // sub, sub, tn), axis=1)
        cnt = per_grp[0:kk, :] + jnp.where(_iota((kk, tn), 0) == 0, per_grp[kk:kk + 1, :], 0.0)
        top = sv[0][0:1, :] + sv[1][0:1, :]
        z = jnp.sum(jnp.where(taken, jnp.exp(cand - top), 0.0), axis=0, keepdims=True)
        cnt_dense = jnp.zeros((PEER_NKEYS, tn), F32)
        for k0 in range(kk):
            cnt_dense = jnp.where(rk[0] == float(k0), cnt[k0:k0 + 1, :], cnt_dense)
        cnt_ref[h, :, toks] = cnt_dense
        e0_ref[h, :, toks] = jnp.exp(s[0] - sv[0][0:1, :])
        e1_ref[h, :, toks] = (jnp.exp(s[1] - sv[1][0:1, :]) / z).astype(e1_ref.dtype)
        r1_ref[h, :, toks] = rk[1].astype(r1_ref.dtype)
        return jnp.max(surplus) > 0.5

    @pl.loop(0, PEER_HEADS * n_pass)
    def _(it):
        h = it // n_pass
        toks = pl.ds(pl.multiple_of((it % n_pass) * tn, tn), tn)
        tie = route(h, toks, exact=False)

        @pl.when(tie)
        def _():
            route(h, toks, exact=True)


def _peer_route(h2, wq, sk, tn):
    n, d = h2.shape
    full = lambda a: pl.BlockSpec(a.shape, lambda i: (0,) * a.ndim)
    out = pl.BlockSpec((PEER_HEADS, PEER_NKEYS, tn), lambda i: (0, 0, i))
    return pl.pallas_call(
        _peer_route_kernel,
        grid=(n // tn,),
        in_specs=[pl.BlockSpec((tn, d), lambda i: (i, 0)), full(wq), full(sk)],
        out_specs=[out] * 4,
        out_shape=[jax.ShapeDtypeStruct((PEER_HEADS, PEER_NKEYS, n), t)
                   for t in (F32, F32, PEER_WEIGHT_DTYPE, PEER_WEIGHT_DTYPE)],
        scratch_shapes=[pltpu.VMEM((tn, PEER_HEADS * PEER_DKEY), F32)],
        compiler_params=_cparams("parallel"),
        name="peer_route",
    )(h2, wq, sk)


PEER_ROUTE_PASS = 256
PEER_EXPERT_TILE = 2048
PEER_PRE_ROWS = 512
PEER_STAGE_ROWS = 256
PEER_LANE_CHUNK = 256


def _peer_expert_kernel(ht_ref, x1_ref, g2_ref, cnt_ref, e0_ref, e1_ref, r1_ref, u_ref, vt_ref, lg_ref, lb_ref,
                        o_ref, acc_ref, g_ref, a_ref, *, alpha):
    e = pl.program_id(1)
    te, tn = a_ref.shape
    stage = min(PEER_STAGE_ROWS, te)
    n_stage = te // stage
    lane_chunk = min(PEER_LANE_CHUNK, tn)

    @pl.when(e == 0)
    def _():
        acc_ref[...] = jnp.zeros(acc_ref.shape, F32)

    pre_rows = min(PEER_PRE_ROWS, te)
    for r0 in range(0, te, pre_rows):
        a_ref[r0:r0 + pre_rows, :] = jnp.dot(u_ref[r0:r0 + pre_rows, :], ht_ref[...], preferred_element_type=F32)
    wdt = e1_ref.dtype

    def weighted_activation(s):
        for c in range(stage // PEER_NKEYS):
            i0_local = s * (stage // PEER_NKEYS) + c
            rows = slice(i0_local * PEER_NKEYS, (i0_local + 1) * PEER_NKEYS)
            row = pl.ds(e * (te // PEER_NKEYS) + i0_local, 1)
            cnt_rows = [cnt_ref[h, row, :].astype(wdt) for h in range(PEER_HEADS)]
            e0_rows = [e0_ref[h, row, :].astype(wdt) for h in range(PEER_HEADS)]
            for t0 in range(0, tn, lane_chunk):
                lanes = slice(t0, t0 + lane_chunk)
                w = jnp.zeros((PEER_NKEYS, lane_chunk), wdt)
                for h in range(PEER_HEADS):
                    hit = r1_ref[h, :, lanes] < cnt_rows[h][:, lanes]
                    w = w + jnp.where(hit, e1_ref[h, :, lanes], jnp.zeros((), wdt)) * e0_rows[h][:, lanes]
                g_ref[rows, lanes] = (w.astype(F32) * _gelu(a_ref[rows, lanes])).astype(g_ref.dtype)

    def project(s):
        rows = slice(s * stage, (s + 1) * stage)
        acc_ref[...] += jnp.dot(vt_ref[:, rows], g_ref[rows, :], preferred_element_type=F32)

    for s in range(n_stage + 1):
        if s < n_stage:
            weighted_activation(s)
        if s >= 1:
            project(s - 1)

    @pl.when(e == pl.num_programs(1) - 1)
    def _():
        ff = acc_ref[...].T
        o_ref[...] = _ln(alpha * x1_ref[...] + (1.0 + g2_ref[0]) * ff) * lg_ref[...] + lb_ref[...]


def _peer_expert(h2t, x1, g2, cnt, e0, e1, r1, u, vt, lg, lb, tn, te, alpha, layer):
    d, n = h2t.shape
    nt = n // tn
    n_exp = u.shape[1]
    mrow = g2.shape[1]
    per_tile = g2.shape[0] == nt and mrow == tn
    tiles_per_seq = nt // g2.shape[0] if not per_tile else 1
    mod_map = (lambda i, e: (i, 0, 0)) if per_tile else (lambda i, e: (i // tiles_per_seq, 0, 0))
    tok = pl.BlockSpec((tn, d), lambda i, e: (i, 0))
    route = pl.BlockSpec((PEER_HEADS, PEER_NKEYS, tn), lambda i, e: (0, 0, i))
    vec = pl.BlockSpec((1, d), lambda i, e: (0, 0))
    return pl.pallas_call(
        functools.partial(_peer_expert_kernel, alpha=alpha),
        grid=(nt, n_exp // te),
        in_specs=[pl.BlockSpec((d, tn), lambda i, e: (0, i)), tok, pl.BlockSpec((1, mrow, d), mod_map),
                  route, route, route, route,
                  pl.BlockSpec((None, te, d), lambda i, e: (layer, e, 0)),
                  pl.BlockSpec((None, d, te), lambda i, e: (layer, 0, e)), vec, vec],
        out_specs=tok,
        out_shape=jax.ShapeDtypeStruct((n, d), F32),
        scratch_shapes=[pltpu.VMEM((d, tn), F32), pltpu.VMEM((te, tn), MXU_DTYPE), pltpu.VMEM((te, tn), F32)],
        compiler_params=_cparams("parallel", "arbitrary"),
        name="peer_expert",
    )(h2t, x1, g2, cnt, e0, e1, r1, u, vt, lg, lb)


def _block_diag_in(bb_t):
    g, i, p = bb_t.shape
    return jnp.einsum('gip,gh->gihp', bb_t, jnp.eye(g, dtype=bb_t.dtype)).reshape(g * i, g * p)


def _block_diag_out(c):
    g, i, p = c.shape
    return jnp.einsum('gip,gh->gphi', c, jnp.eye(g, dtype=c.dtype)).reshape(g * p, g * i)


def _tile(n, pref):
    return pref if n % pref == 0 else n


def _layer_group(x, mods, w, n_seq, seq, past, layer):
    n, d = x.shape
    bw = BRANCH_WIDTH
    alpha = w['alpha']
    tm = _tile(n, 512)
    tm2 = _tile(n, 256)

    def mod_arr(m, t):
        if seq % t == 0:
            return m.reshape(n_seq, 1, d)
        return jnp.repeat(m, seq, axis=0).reshape(n // t, t, d)

    sh1, sc1, g1, sh2, sc2, g2 = mods
    (h1, gq, gk, gv, gr, mq, mk, mv, fq, fk, fv, su, la, sm) = _inproj(
        x, mod_arr(sh1, tm), mod_arr(sc1, tm), w['wa'], w['ws'], w['gw2'], w['gb'], w['fb'], tm)
    logf = sm[:, GLA_GATE_RANK:GLA_GATE_RANK + N_HEADS]

    s0_t = past['gla'].transpose(0, 1, 3, 2)
    oa, gla_t = _gla(gq, gk, gv, la, gr, w['gn'], s0_t, n_seq, seq)
    gla_state = gla_t.transpose(0, 1, 3, 2)

    if past['paged']:
        r3 = lambda a: a.reshape(n_seq, seq, bw)
        lfn = logf.reshape(n_seq, seq, N_HEADS).transpose(0, 2, 1)
        ob, oc = _attn_sample(past['page_table'], r3(mq), r3(mk), r3(mv), r3(fq), r3(fk), r3(fv), lfn,
                              past['moba_k'], past['moba_v'], past['fox_k'], past['fox_v'], past['fox_logf_t'], layer)
        ob = ob.reshape(n, bw)
        oc = oc.reshape(n, bw)
    else:
        lf_rows = logf.reshape(n_seq, seq, N_HEADS).transpose(0, 2, 1).reshape(n_seq * N_HEADS, seq)
        cum = _fox_cumsum(lf_rows).reshape(n_seq, N_HEADS, seq)
        ob = _attn_prompt(mq, mk, mv, None, n_seq, seq, True)
        oc = _attn_prompt(fq, fk, fv, cum, n_seq, seq, False)

    steps = math.gcd(seq, 128)
    u_tb = su.reshape(n_seq, seq, bw).transpose(1, 0, 2)
    od_tb, s5_re, s5_im = _s5(u_tb, w['bd'], w['cd'], w['a_re'], w['a_im'], w['s5_d'], w['wglu'], w['bglu'],
                              past['s5_re'].reshape(n_seq, S5_WIDTH_STATE), past['s5_im'].reshape(n_seq, S5_WIDTH_STATE),
                              steps)
    od = od_tb.transpose(1, 0, 2).reshape(n, bw)

    x1, h2, h2t = _merge(x, h1, oa, ob, oc, od, mod_arr(g1, tm2), mod_arr(sh2, tm2), mod_arr(sc2, tm2),
                         w['wg'], w['wb'], w['wo'], w['ln1_g'], w['ln1_b'], tm2, alpha)

    cnt, e0, e1, r1 = _peer_route(h2, w['wq'], w['sk'], tm2)
    x2 = _peer_expert(h2t, x1, mod_arr(g2, tm), cnt, e0, e1, r1, w['pu'], w['pvt'], w['ln2_g'], w['ln2_b'],
                      tm, PEER_EXPERT_TILE, alpha, layer)

    hd = lambda a: a.reshape(n_seq, seq, N_HEADS, HEAD_DIM)
    new = {'moba_k': hd(mk), 'moba_v': hd(mv), 'fox_k': hd(fk), 'fox_v': hd(fv),
           'fox_logf': logf.reshape(n_seq, seq, N_HEADS), 'gla': gla_state,
           's5_re': s5_re.reshape(n_seq, S5_GROUPS, S5_STATE), 's5_im': s5_im.reshape(n_seq, S5_GROUPS, S5_STATE)}
    return x2, new


def _prep_weights(l, depth, d, w_in, gla_w_gate2, gla_b_gate, gla_norm, fox_b_f, s5_lambda_re, s5_lambda_im, s5_log_dt,
                  s5_b_re, s5_b_im, s5_c_re, s5_c_im, s5_d, s5_w_glu, s5_b_glu, w_branch, w_out, ln1_g, ln1_b,
                  peer_w_q, peer_sub_keys, peer_u, peer_v, ln2_g, ln2_b):
    bw = BRANCH_WIDTH
    bf = lambda a: a.astype(MXU_DTYPE)
    pad_small = LANES - GLA_GATE_RANK - N_HEADS
    w = {}
    w['wa'], w['ws'], w['wg'] = _repack_w_in(w_in, l)
    w['gw2'] = bf(jnp.concatenate([gla_w_gate2[l], jnp.zeros((LANES - GLA_GATE_RANK, bw), F32)], axis=0))
    w['gb'] = gla_b_gate[l].reshape(1, bw)
    w['gn'] = gla_norm[l].reshape(1, bw)
    w['fb'] = jnp.concatenate([jnp.zeros((GLA_GATE_RANK,), F32), fox_b_f[l], jnp.zeros((pad_small,), F32)]).reshape(1, LANES)
    a_re, a_im, bb_re, bb_im = _s5_params(s5_lambda_re[l], s5_lambda_im[l], s5_log_dt[l], s5_b_re[l], s5_b_im[l])
    w['a_re'] = a_re.reshape(1, S5_WIDTH_STATE)
    w['a_im'] = a_im.reshape(1, S5_WIDTH_STATE)
    w['bd'] = bf(jnp.concatenate([_block_diag_in(bb_re), _block_diag_in(bb_im)], axis=1))
    w['cd'] = bf(jnp.concatenate([_block_diag_out(s5_c_re[l]), -_block_diag_out(s5_c_im[l])], axis=0))
    w['s5_d'] = s5_d[l].reshape(1, bw)
    w['wglu'] = bf(s5_w_glu[l])
    w['bglu'] = s5_b_glu[l].reshape(1, bw)
    w['wb'] = bf(w_branch[l])
    w['wo'] = bf(w_out[l])
    w['ln1_g'] = ln1_g[l].reshape(1, d)
    w['ln1_b'] = ln1_b[l].reshape(1, d)
    w['wq'] = bf(peer_w_q[l])
    w['sk'] = bf(peer_sub_keys[l])
    w['pu'] = peer_u
    w['pvt'] = peer_v
    w['ln2_g'] = ln2_g[l].reshape(1, d)
    w['ln2_b'] = ln2_b[l].reshape(1, d)
    w['alpha'] = (2.0 * depth) ** 0.25
    return w


def kernel(x_prompt, x_sample, c_prompt, c_sample, page_table, cache_moba_k, cache_moba_v, cache_fox_k, cache_fox_v, cache_fox_logf, state_gla, state_s5_re, state_s5_im, w_mod, b_mod, w_in, gla_w_gate2, gla_b_gate, gla_norm, fox_b_f, s5_lambda_re, s5_lambda_im, s5_log_dt, s5_b_re, s5_b_im, s5_c_re, s5_c_im, s5_d, s5_w_glu, s5_b_glu, w_branch, w_out, ln1_g, ln1_b, peer_w_q, peer_sub_keys, peer_u, peer_v, ln2_g, ln2_b):
    bp, seq, d = x_prompt.shape
    bs, dseq, _ = x_sample.shape
    depth = w_in.shape[0]
    n_pool, page = cache_moba_k.shape[1], cache_moba_k.shape[2]
    flat_cache = lambda c: c.transpose(0, 1, 3, 4, 2).reshape(depth, n_pool, BRANCH_WIDTH, page)
    paged = {'paged': True, 'page_table': page_table,
             'moba_k': flat_cache(cache_moba_k), 'moba_v': flat_cache(cache_moba_v),
             'fox_k': flat_cache(cache_fox_k), 'fox_v': flat_cache(cache_fox_v),
             'fox_logf_t': cache_fox_logf.transpose(0, 1, 3, 2)}
    c_all = jnp.concatenate([c_prompt, c_sample], axis=0)
    y_p = x_prompt.reshape(bp * seq, d)
    y_s = x_sample.reshape(bs * dseq, d)
    st_p, st_s = [], []
    pu_all = peer_u.astype(MXU_DTYPE)
    pvt_all = peer_v.astype(MXU_DTYPE).transpose(0, 2, 1)
    for l in range(depth):
        w = _prep_weights(l, depth, d, w_in, gla_w_gate2, gla_b_gate, gla_norm, fox_b_f, s5_lambda_re, s5_lambda_im,
                          s5_log_dt, s5_b_re, s5_b_im, s5_c_re, s5_c_im, s5_d, s5_w_glu, s5_b_glu, w_branch, w_out,
                          ln1_g, ln1_b, peer_w_q, peer_sub_keys, pu_all, pvt_all, ln2_g, ln2_b)
        mod = _modulation(c_all, w_mod, b_mod, l)
        mods = jnp.split(mod, 6, axis=-1)
        past_p = {'paged': False, 'gla': jnp.zeros((bp, N_HEADS, HEAD_DIM, HEAD_DIM), F32),
                  's5_re': jnp.zeros((bp, S5_GROUPS, S5_STATE), F32), 's5_im': jnp.zeros((bp, S5_GROUPS, S5_STATE), F32)}
        past_s = dict(paged, gla=state_gla[l], s5_re=state_s5_re[l], s5_im=state_s5_im[l])
        y_p, new_p = _layer_group(y_p, [m[:bp] for m in mods], w, bp, seq, past_p, l)
        y_s, new_s = _layer_group(y_s, [m[bp:] for m in mods], w, bs, dseq, past_s, l)
        st_p.append(new_p)
        st_s.append(new_s)

    stk = lambda states, name: jnp.stack([s[name] for s in states])
    names = ('moba_k', 'moba_v', 'fox_k', 'fox_v', 'fox_logf', 'gla', 's5_re', 's5_im')
    return ((y_p.reshape(bp, seq, d), y_s.reshape(bs, dseq, d))
            + tuple(stk(st_p, nm) for nm in names) + tuple(stk(st_s, nm) for nm in names))
```

```python
import functools
import math

import numpy as np
import jax
import jax.numpy as jnp
from jax import lax
from jax.experimental import pallas as pl
from jax.experimental.pallas import tpu as pltpu

F32 = jnp.float32
MXU_DTYPE = jnp.bfloat16
PEER_WEIGHT_DTYPE = jnp.bfloat16

HEAD_DIM = 64
N_HEADS = 4
BRANCH_WIDTH = N_HEADS * HEAD_DIM
N_BRANCH = 4
GLA_GATE_RANK = 16
GLA_TAU = 16.0
GLA_CHUNK = 64
GLA_SUBCHUNK = 16
GLA_UNROLL = 4
MOBA_BLOCK = 256
MOBA_TOPK = 3
S5_GROUP = 16
S5_GROUPS = BRANCH_WIDTH // S5_GROUP
S5_STATE = 64
S5_WIDTH_STATE = S5_GROUPS * S5_STATE
PEER_HEADS = 8
PEER_DKEY = 256
PEER_NKEYS = 128
PEER_TOPK = 16
LN_EPS = 1e-5
NEG_BIG = -1e30

V7X_VMEM_LIMIT = 56 * 1024 * 1024
LANES = 128


def _cparams(*sem):
    return pltpu.CompilerParams(dimension_semantics=tuple(sem), vmem_limit_bytes=V7X_VMEM_LIMIT)


def _dot(a, b):
    return jnp.dot(a.astype(MXU_DTYPE), b.astype(MXU_DTYPE), preferred_element_type=F32)


def _dot_nt(a, b):
    return lax.dot_general(a.astype(MXU_DTYPE), b.astype(MXU_DTYPE), (((1,), (1,)), ((), ())),
                           preferred_element_type=F32)


def _dot_tn(a, b):
    return lax.dot_general(a.astype(MXU_DTYPE), b.astype(MXU_DTYPE), (((0,), (0,)), ((), ())),
                           preferred_element_type=F32)


def _split3(x):
    hi = x.astype(jnp.bfloat16)
    r1 = x - hi.astype(F32)
    mid = r1.astype(jnp.bfloat16)
    lo = (r1 - mid.astype(F32)).astype(jnp.bfloat16)
    return hi, mid, lo


def _dot_exact01(x, m01):
    hi, mid, lo = _split3(x)
    m = m01.astype(jnp.bfloat16)
    d = lambda a: jnp.dot(a, m, preferred_element_type=F32)
    return d(hi) + d(mid) + d(lo)


def _dot01_exact(m01, x):
    hi, mid, lo = _split3(x)
    m = m01.astype(jnp.bfloat16)
    d = lambda a: jnp.dot(m, a, preferred_element_type=F32)
    return d(hi) + d(mid) + d(lo)


def _ln(x):
    mu = jnp.mean(x, axis=-1, keepdims=True)
    xc = x - mu
    var = jnp.mean(xc * xc, axis=-1, keepdims=True)
    return xc * lax.rsqrt(var + LN_EPS)


def _sigmoid(x):
    return 1.0 / (1.0 + jnp.exp(-x))


def _log_sigmoid(x):
    return jnp.minimum(x, 0.0) - jnp.log1p(jnp.exp(-jnp.abs(x)))


def _gelu(x):
    return 0.5 * x * (1.0 + lax.erf(x * (1.0 / math.sqrt(2.0))))


def _iota(shape, axis):
    return lax.broadcasted_iota(jnp.int32, shape, axis)


def _mod_kernel(c_ref, w_ref, b_ref, o_ref):
    c = c_ref[...]
    o_ref[...] = _dot(c * _sigmoid(c), w_ref[...]) + b_ref[...]


def _modulation(c_all, w_mod, b_mod, layer):
    n, d = c_all.shape
    depth, _, width = w_mod.shape
    tn = _tile(width, 1536)
    return pl.pallas_call(
        _mod_kernel,
        grid=(width // tn,),
        in_specs=[pl.BlockSpec((n, d), lambda j: (0, 0)),
                  pl.BlockSpec((None, d, tn), lambda j: (layer, 0, j)),
                  pl.BlockSpec((None, 1, tn), lambda j: (layer, 0, j))],
        out_specs=pl.BlockSpec((n, tn), lambda j: (0, j)),
        out_shape=jax.ShapeDtypeStruct((n, width), F32),
        compiler_params=_cparams("parallel"),
        name="modulation",
    )(c_all, w_mod, b_mod.reshape(depth, 1, width))


N_PROJ = 11
PROJ_SEGMENTS = (0, 1, 2, 3, 5, 6, 7, 8, 9, 10, 12)
SEG_GATE_A, SEG_GATE_F, SEG_BRANCH_GATE = 4, 11, 13


def _in_offsets(d):
    bw = BRANCH_WIDTH
    sizes = (bw,) * 4 + (GLA_GATE_RANK,) + (bw,) * 3 + (bw,) * 3 + (N_HEADS, bw, N_BRANCH * d)
    return [int(o) for o in np.concatenate([[0], np.cumsum(sizes)])]


def _repack_kernel(w_ref, wa_ref, ws_ref, wg_ref):
    rows, d_gate = wg_ref.shape
    offs = _in_offsets(d_gate // N_BRANCH)
    bw = BRANCH_WIDTH
    for i, seg in enumerate(PROJ_SEGMENTS):
        wa_ref[:, i * bw:(i + 1) * bw] = w_ref[:, offs[seg]:offs[seg] + bw].astype(wa_ref.dtype)
    small = jnp.concatenate([w_ref[:, offs[SEG_GATE_A]:offs[SEG_GATE_A + 1]],
                             w_ref[:, offs[SEG_GATE_F]:offs[SEG_GATE_F + 1]],
                             jnp.zeros((rows, LANES - GLA_GATE_RANK - N_HEADS), F32)], axis=1)
    ws_ref[...] = small.astype(ws_ref.dtype)
    wg_ref[...] = w_ref[:, offs[SEG_BRANCH_GATE]:offs[SEG_BRANCH_GATE + 1]].astype(wg_ref.dtype)


def _repack_w_in(w_in, layer):
    _, d, width = w_in.shape
    tr = _tile(d, 256)
    out_w = (N_PROJ * BRANCH_WIDTH, LANES, N_BRANCH * d)
    return pl.pallas_call(
        _repack_kernel,
        grid=(d // tr,),
        in_specs=[pl.BlockSpec((None, tr, width), lambda i: (layer, i, 0))],
        out_specs=[pl.BlockSpec((tr, w), lambda i: (i, 0)) for w in out_w],
        out_shape=[jax.ShapeDtypeStruct((d, w), MXU_DTYPE) for w in out_w],
        compiler_params=_cparams("parallel"),
        name="repack_w_in",
    )(w_in)


def _inproj_kernel(x_ref, sh_ref, sc_ref, wa_ref, ws_ref, gw2_ref, gb_ref, fb_ref,
                   h1_ref, gq_ref, gk_ref, gv_ref, gr_ref, mq_ref, mk_ref, mv_ref, fq_ref, fk_ref, fv_ref,
                   su_ref, la_ref, sm_ref):
    h1 = (_ln(x_ref[...]) * (1.0 + sc_ref[0]) + sh_ref[0]).astype(MXU_DTYPE)
    h1_ref[...] = h1
    outs = (gq_ref, gk_ref, gv_ref, gr_ref, mq_ref, mk_ref, mv_ref, fq_ref, fk_ref, fv_ref, su_ref)
    for i, r in enumerate(outs):
        r[...] = jnp.dot(h1, wa_ref[:, i * BRANCH_WIDTH:(i + 1) * BRANCH_WIDTH],
                         preferred_element_type=F32).astype(r.dtype)
    ps = jnp.dot(h1, ws_ref[...], preferred_element_type=F32)
    la_ref[...] = _log_sigmoid(_dot(ps, gw2_ref[...]) + gb_ref[...]) * (1.0 / GLA_TAU)
    sm_ref[...] = _log_sigmoid(ps + fb_ref[...])


def _inproj(x, sh, sc, wa, ws, gw2, gb, fb, tm):
    n, d = x.shape
    nt = n // tm
    mrow = sh.shape[1]
    per_tile = sh.shape[0] == nt and mrow == tm
    tiles_per_seq = nt // sh.shape[0] if not per_tile else 1
    mod_map = (lambda i: (i, 0, 0)) if per_tile else (lambda i: (i // tiles_per_seq, 0, 0))
    tok = lambda w: pl.BlockSpec((tm, w), lambda i: (i, 0))
    full = lambda a: pl.BlockSpec(a.shape, lambda i: (0,) * a.ndim)
    bw = BRANCH_WIDTH
    odt = [MXU_DTYPE] + [F32] * (N_PROJ + 2)
    owd = [d] + [bw] * N_PROJ + [bw, LANES]
    return pl.pallas_call(
        _inproj_kernel,
        grid=(nt,),
        in_specs=[tok(d), pl.BlockSpec((1, mrow, d), mod_map), pl.BlockSpec((1, mrow, d), mod_map),
                  full(wa), full(ws), full(gw2), full(gb), full(fb)],
        out_specs=[tok(w) for w in owd],
        out_shape=[jax.ShapeDtypeStruct((n, w), t) for w, t in zip(owd, odt)],
        compiler_params=_cparams("parallel"),
        name="inproj",
    )(x, sh, sc, wa, ws, gw2, gb, fb)


def _gla_kernel(q_ref, k_ref, v_ref, la_ref, gr_ref, gn_ref, s0_ref, o_ref, sT_ref, *, seq, chunk, sub):
    n_chunks = seq // chunk
    n_sub = chunk // sub
    scale = HEAD_DIM ** -0.5
    tril = (_iota((chunk, chunk), 1) <= _iota((chunk, chunk), 0))
    gn = gn_ref[...]

    def one_chunk(c, state):
        r0 = c * chunk if isinstance(c, int) else pl.multiple_of(c * chunk, chunk)
        rows = pl.ds(r0, chunk)
        g = la_ref[rows, :]
        bcum = _dot01_exact(tril, g)
        q = q_ref[rows, :] * scale
        k = k_ref[rows, :]
        v = v_ref[rows, :]
        b_end = bcum[chunk - 1:chunk, :]
        qe = q * jnp.exp(bcum)
        kd = k * jnp.exp(b_end - bcum)
        qs, ks = [], []
        for i in range(n_sub):
            base = bcum[i * sub - 1:i * sub, :] if i > 0 else jnp.zeros_like(b_end)
            nk = (i + 1) * sub
            qs.append(q[i * sub:nk, :] * jnp.exp(bcum[i * sub:nk, :] - base))
            ks.append(k[:nk, :] * jnp.exp(base - bcum[:nk, :]))
        heads = range(N_HEADS)
        sls = [slice(h * HEAD_DIM, (h + 1) * HEAD_DIM) for h in heads]
        o_inter = [_dot_nt(qe[:, sls[h]], state[h]) for h in heads]
        raw = [[_dot_nt(qs[i][:, sls[h]], ks[i][:, sls[h]]) for i in range(n_sub)] for h in heads]
        kv = [_dot_tn(v[:, sls[h]], kd[:, sls[h]]) for h in heads]
        atts = []
        for h in heads:
            att_rows = []
            for i in range(n_sub):
                nk = (i + 1) * sub
                causal = _iota((sub, nk), 1) <= _iota((sub, nk), 0) + i * sub
                a = jnp.where(causal, raw[h][i], 0.0)
                if nk < chunk:
                    a = jnp.concatenate([a, jnp.zeros((sub, chunk - nk), F32)], axis=1)
                att_rows.append(a)
            atts.append(att_rows[0] if n_sub == 1 else jnp.concatenate(att_rows, axis=0))
        new_state, outs = [], []
        for h in heads:
            o_h = o_inter[h] + _dot(atts[h], v[:, sls[h]])
            new_state.append(state[h] * jnp.exp(b_end[:, sls[h]]) + kv[h])
            o_h = o_h * lax.rsqrt(jnp.mean(o_h * o_h, axis=-1, keepdims=True) + LN_EPS)
            outs.append(o_h)
        o = jnp.concatenate(outs, axis=1)
        gr = gr_ref[rows, :]
        o_ref[rows, :] = (o * gn * (gr * _sigmoid(gr))).astype(o_ref.dtype)
        return tuple(new_state)

    state0 = tuple(s0_ref[0, h] for h in range(N_HEADS))
    if n_chunks == 1:
        state = one_chunk(0, state0)
    else:
        state = lax.fori_loop(0, n_chunks, one_chunk, state0, unroll=math.gcd(n_chunks, GLA_UNROLL))
    for h in range(N_HEADS):
        sT_ref[0, h] = state[h]


def _gla(gq, gk, gv, la, gr, gn, s0_t, n_seq, seq):
    chunk = math.gcd(seq, GLA_CHUNK)
    sub = math.gcd(chunk, GLA_SUBCHUNK)
    tok = pl.BlockSpec((seq, BRANCH_WIDTH), lambda b: (b, 0))
    st = pl.BlockSpec((1, N_HEADS, HEAD_DIM, HEAD_DIM), lambda b: (b, 0, 0, 0))
    return pl.pallas_call(
        functools.partial(_gla_kernel, seq=seq, chunk=chunk, sub=sub),
        grid=(n_seq,),
        in_specs=[tok, tok, tok, tok, tok, pl.BlockSpec((1, BRANCH_WIDTH), lambda b: (0, 0)), st],
        out_specs=[tok, st],
        out_shape=[jax.ShapeDtypeStruct((n_seq * seq, BRANCH_WIDTH), F32),
                   jax.ShapeDtypeStruct((n_seq, N_HEADS, HEAD_DIM, HEAD_DIM), F32)],
        compiler_params=_cparams("parallel"),
        name="gla",
    )(gq, gk, gv, la, gr, gn, s0_t)


def _cum_kernel(lf_ref, o_ref):
    rows, length = lf_ref.shape
    triu = _iota((LANES, LANES), 0) <= _iota((LANES, LANES), 1)
    carry = jnp.zeros((rows, 1), F32)
    for c in range(length // LANES):
        cs = _dot_exact01(lf_ref[:, c * LANES:(c + 1) * LANES], triu) + carry
        o_ref[:, c * LANES:(c + 1) * LANES] = cs
        carry = cs[:, LANES - 1:LANES]


def _fox_cumsum(lf_rows):
    return pl.pallas_call(
        _cum_kernel,
        out_shape=jax.ShapeDtypeStruct(lf_rows.shape, F32),
        compiler_params=pltpu.CompilerParams(vmem_limit_bytes=V7X_VMEM_LIMIT),
        name="fox_cumsum",
    )(lf_rows)


def _alibi_slope(h):
    return 2.0 ** (-8.0 * (h + 1) / N_HEADS)


def _attn_prompt_kernel(q_ref, k_ref, v_ref, *rest, seq, moba):
    if moba:
        o_ref, kx_ref, vt_ref, km_ref = rest
    else:
        cum_ref, o_ref, kx_ref, vt_ref = rest
    blk = MOBA_BLOCK
    nb = seq // blk
    i = pl.program_id(1)
    nbp = km_ref.shape[1] if moba else 0
    heads = range(N_HEADS)
    cols = [slice(h * HEAD_DIM, (h + 1) * HEAD_DIM) for h in heads]
    n_feat = kx_ref.shape[2] - HEAD_DIM

    @pl.when(i == 0)
    def _():
        pos = _iota((seq, n_feat), 0)
        slot = _iota((seq, n_feat), 1)
        for h in heads:
            kf = k_ref[:, cols[h]]
            if moba:
                km = jnp.sum(kf.reshape(nb, blk, HEAD_DIM), axis=1) * (1.0 / blk)
                km_ref[h] = jnp.concatenate([km, jnp.zeros((nbp - nb, HEAD_DIM), F32)], axis=0) if nbp > nb else km
                feat = jnp.where(slot < nbp, (slot == pos // blk).astype(F32),
                                 jnp.where(slot == nbp, (pos // blk).astype(F32),
                                           jnp.where(slot == nbp + 1, (pos % blk).astype(F32), 0.0)))
            else:
                hi, mid, lo = (p.astype(F32) for p in _split3(cum_ref[0, h:h + 1, :]))
                part = _iota((n_feat, seq), 0)
                feat = jnp.where(part == 0, hi, jnp.where(part == 1, mid, jnp.where(part == 2, lo, 0.0))).T
            kx_ref[h, :, 0:HEAD_DIM] = kf.astype(MXU_DTYPE)
            kx_ref[h, :, HEAD_DIM:HEAD_DIM + n_feat] = feat.astype(MXU_DTYPE)
            vt_ref[h] = v_ref[:, cols[h]].T.astype(MXU_DTYPE)

    q_t = q_ref[...].T
    qx = []
    for h in heads:
        q_h = q_t[h * HEAD_DIM:(h + 1) * HEAD_DIM, :]
        if moba:
            bs = _dot(km_ref[h], q_h)
            jrow = _iota((nbp, blk), 0)
            bs = jnp.where(jrow < i, bs, -jnp.inf)
            rank = jnp.zeros((nbp, blk), F32)
            for j2 in range(nb):
                row = bs[j2:j2 + 1, :]
                beats = (row > bs) | ((row == bs) & (j2 < jrow))
                rank = rank + beats.astype(F32)
            sel = ((jrow < i) & (rank < MOBA_TOPK)) | (jrow == i)
            frow = _iota((n_feat - nbp, blk), 0)
            slope = _alibi_slope(h)
            extra = jnp.concatenate([jnp.where(sel, 0.0, NEG_BIG),
                                     jnp.where(frow == 0, slope * blk, jnp.where(frow == 1, slope, 0.0))], axis=0)
        else:
            extra = jnp.where(_iota((n_feat, blk), 0) < 3, -1.0, 0.0)
        qx.append(jnp.concatenate([q_h * (HEAD_DIM ** -0.5), extra], axis=0).astype(MXU_DTYPE))

    def scores(h, c0):
        return jnp.dot(kx_ref[h, pl.ds(c0, blk), :], qx[h], preferred_element_type=F32)

    def block_update(states, c0, mask):
        s_all = [scores(h, c0) for h in heads]
        if mask is not None:
            s_all = [jnp.where(mask, s_t, NEG_BIG) for s_t in s_all]
        parts = []
        for h in heads:
            m, l, acc = states[h]
            m_new = jnp.maximum(m, jnp.max(s_all[h], axis=0, keepdims=True))
            alpha = jnp.exp(m - m_new)
            p = jnp.exp(s_all[h] - m_new)
            parts.append((m_new, alpha, alpha * l + jnp.sum(p, axis=0, keepdims=True), p.astype(MXU_DTYPE)))
        return tuple((m_new, l_new, alpha * states[h][2] + _dot(vt_ref[h, :, pl.ds(c0, blk)], p))
                     for h, (m_new, alpha, l_new, p) in zip(heads, parts))

    causal_t = _iota((blk, blk), 0) <= _iota((blk, blk), 1)
    init = tuple((jnp.full((1, blk), NEG_BIG, F32), jnp.zeros((1, blk), F32), jnp.zeros((HEAD_DIM, blk), F32))
                 for _ in heads)
    states = block_update(init, pl.multiple_of(i * blk, blk), causal_t)
    states = lax.fori_loop(0, i, lambda j, st: block_update(st, pl.multiple_of(j * blk, blk), None), states)
    o_ref[...] = jnp.concatenate([acc / l for _, l, acc in states], axis=0).T.astype(o_ref.dtype)


def _attn_prompt(q, k, v, cum, n_seq, seq, moba):
    blk = MOBA_BLOCK
    nq = seq // blk
    nbp = -(-nq // 8) * 8
    tile = pl.BlockSpec((blk, BRANCH_WIDTH), lambda b, i: (b * nq + i, 0))
    whole = pl.BlockSpec((seq, BRANCH_WIDTH), lambda b, i: (b, 0))
    in_specs = [tile, whole, whole]
    args = [q, k, v]
    scratch = [pltpu.VMEM((N_HEADS, seq, 2 * HEAD_DIM), MXU_DTYPE), pltpu.VMEM((N_HEADS, HEAD_DIM, seq), MXU_DTYPE)]
    if moba:
        scratch += [pltpu.VMEM((N_HEADS, nbp, HEAD_DIM), F32)]
    else:
        in_specs.append(pl.BlockSpec((1, N_HEADS, seq), lambda b, i: (b, 0, 0)))
        args.append(cum)
    return pl.pallas_call(
        functools.partial(_attn_prompt_kernel, seq=seq, moba=moba),
        grid=(n_seq, nq),
        in_specs=in_specs,
        out_specs=tile,
        out_shape=jax.ShapeDtypeStruct((n_seq * seq, BRANCH_WIDTH), F32),
        scratch_shapes=scratch,
        compiler_params=_cparams("parallel", "arbitrary"),
        name="moba_prompt" if moba else "fox_prompt",
    )(*args)


def _block_diag_queries(q):
    t = q.shape[0]
    col_head = _iota((t, BRANCH_WIDTH), 1) // HEAD_DIM
    return jnp.concatenate([jnp.where(col_head == h, q, jnp.zeros_like(q)) for h in range(N_HEADS)], axis=0)


def _head_diag(acc, t):
    return jnp.concatenate([acc[h * t:(h + 1) * t, h * HEAD_DIM:(h + 1) * HEAD_DIM] for h in range(N_HEADS)], axis=1)


def _attn_sample_kernel(pt_ref, mq_ref, mkn_ref, mvn_ref, fq_ref, fkn_ref, fvn_ref, lfn_ref, *refs,
                        t_new, n_pages, page, past_len):
    del pt_ref
    mk, mv, fk, fv, lfc = (refs[i * n_pages:(i + 1) * n_pages] for i in range(5))
    ob_ref, oc_ref = refs[5 * n_pages:]
    rows = N_HEADS * t_new
    scale = HEAD_DIM ** -0.5
    row_head = _iota((rows, 1), 0) // t_new
    row_t = _iota((rows, 1), 0) % t_new
    lane_n = _iota((rows, t_new), 1)
    causal = lane_n <= row_t
    cast = lambda r: r[...].astype(MXU_DTYPE)

    def softmax(s_past, s_new):
        m = jnp.maximum(jnp.max(s_past, axis=-1, keepdims=True), jnp.max(s_new, axis=-1, keepdims=True))
        p_past = jnp.exp(s_past - m)
        p_new = jnp.exp(s_new - m)
        l = jnp.sum(p_past, axis=-1, keepdims=True) + jnp.sum(p_new, axis=-1, keepdims=True)
        return p_past.astype(MXU_DTYPE), p_new, l

    def weighted_values(p_past, p_new, l, v_pages, v_new):
        acc = _dot(p_new, v_new)
        for pg in range(n_pages):
            acc = acc + _dot_nt(p_past[:, pg * page:(pg + 1) * page], cast(v_pages[pg]))
        return _head_diag(acc / l, t_new)

    qm = _block_diag_queries(mq_ref[...]).astype(MXU_DTYPE)
    qf = _block_diag_queries(fq_ref[...]).astype(MXU_DTYPE)
    pages_per_block = MOBA_BLOCK // page
    n_blocks = n_pages // pages_per_block
    qk_m = jnp.concatenate([jnp.dot(qm, cast(mk[pg]), preferred_element_type=F32) for pg in range(n_pages)], axis=1)
    qk_m_new = _dot_nt(qm, mkn_ref[...])
    qk_f = jnp.concatenate([jnp.dot(qf, cast(fk[pg]), preferred_element_type=F32) for pg in range(n_pages)], axis=1)
    qk_f_new = _dot_nt(qf, fkn_ref[...])
    lf_all = jnp.concatenate([lfc[pg][...] for pg in range(n_pages)], axis=0)
    triu = _iota((page, page), 0) <= _iota((page, page), 1)
    local = _dot_exact01(lf_all, triu)
    kmean_cols = []
    for b in range(n_blocks):
        blk = jnp.concatenate([mk[b * pages_per_block + i][...] for i in range(pages_per_block)], axis=1)
        kmean_cols.append(jnp.sum(blk, axis=1, keepdims=True) * (1.0 / MOBA_BLOCK))
    bs = _dot(qm, jnp.concatenate(kmean_cols, axis=1))
    nr = n_pages * N_HEADS
    ri, ci = _iota((nr, nr), 0), _iota((nr, nr), 1)
    earlier_page = (ci % N_HEADS == ri % N_HEADS) & (ci // N_HEADS < ri // N_HEADS)
    cum = local + _dot01_exact(earlier_page, local[:, page - 1:page])

    jl = _iota((rows, n_blocks), 1)
    rank = jnp.zeros((rows, n_blocks), F32)
    for j2 in range(n_blocks):
        col = bs[:, j2:j2 + 1]
        rank = rank + ((col > bs) | ((col == bs) & (j2 < jl))).astype(F32)
    sel = (rank < MOBA_TOPK).astype(F32)
    sel_keys = jnp.concatenate(
        [jnp.broadcast_to(sel[:, b:b + 1], (rows, MOBA_BLOCK)) for b in range(n_blocks)], axis=1) > 0.5
    slope = jnp.zeros((rows, 1), F32)
    for h in range(N_HEADS):
        slope = jnp.where(row_head == h, _alibi_slope(h), slope)
    dist = (past_len + row_t - _iota((rows, past_len), 1)).astype(F32)
    s_m = jnp.where(sel_keys, qk_m * scale - slope * dist, NEG_BIG)
    s_m_new = jnp.where(causal, qk_m_new * scale - slope * (row_t - lane_n).astype(F32), NEG_BIG)
    soft_m = softmax(s_m, s_m_new)

    bias = jnp.concatenate(
        [jnp.broadcast_to(cum[pg * N_HEADS:(pg + 1) * N_HEADS, None, :], (N_HEADS, t_new, page)).reshape(rows, page)
         for pg in range(n_pages)], axis=1)
    s_f = qk_f * scale - bias
    lfn = lfn_ref[...]
    cols, c = [], cum[nr - N_HEADS:nr, page - 1:page]
    for t in range(t_new):
        c = c + lfn[:, t:t + 1]
        cols.append(c)
    cum_n = jnp.concatenate(cols, axis=1)
    bias_n = jnp.broadcast_to(cum_n[:, None, :], (N_HEADS, t_new, t_new)).reshape(rows, t_new)
    s_f_new = jnp.where(causal, qk_f_new * scale - bias_n, NEG_BIG)
    soft_f = softmax(s_f, s_f_new)

    ob_ref[...] = weighted_values(*soft_m, mv, mvn_ref[...])
    oc_ref[...] = weighted_values(*soft_f, fv, fvn_ref[...])


def _attn_sample(page_table, mq, mkn, mvn, fq, fkn, fvn, lfn, mkc, mvc, fkc, fvc, lfc, layer):
    n_seq, t_new, _ = mq.shape
    n_pages = page_table.shape[1]
    page = mkc.shape[3]
    past_len = n_pages * page
    assert past_len % MOBA_BLOCK == 0 and MOBA_BLOCK % page == 0 and t_new <= MOBA_BLOCK
    new = pl.BlockSpec((None, t_new, BRANCH_WIDTH), lambda b, pt: (b, 0, 0))

    def pages(width):
        return [pl.BlockSpec((None, None, width, page), lambda b, pt, pg=pg: (layer, pt[b, pg], 0, 0))
                for pg in range(n_pages)]

    grid_spec = pltpu.PrefetchScalarGridSpec(
        num_scalar_prefetch=1,
        grid=(n_seq,),
        in_specs=[new, new, new, new, new, new, pl.BlockSpec((None, N_HEADS, t_new), lambda b, pt: (b, 0, 0))]
                 + 4 * pages(BRANCH_WIDTH) + pages(N_HEADS),
        out_specs=[new, new],
    )
    return pl.pallas_call(
        functools.partial(_attn_sample_kernel, t_new=t_new, n_pages=n_pages, page=page, past_len=past_len),
        grid_spec=grid_spec,
        out_shape=[jax.ShapeDtypeStruct((n_seq, t_new, BRANCH_WIDTH), F32)] * 2,
        compiler_params=_cparams("parallel"),
        name="attn_sample",
    )(page_table, mq, mkn, mvn, fq, fkn, fvn, lfn,
      *([mkc] * n_pages + [mvc] * n_pages + [fkc] * n_pages + [fvc] * n_pages + [lfc] * n_pages))


def _s5_param_kernel(lre_ref, lim_ref, ldt_ref, bre_ref, bim_ref, are_ref, aim_ref, bbre_ref, bbim_ref):
    lam_re = lre_ref[...]
    lam_im = lim_ref[...]
    dt = jnp.exp(ldt_ref[...])
    mag = jnp.exp(lam_re * dt)
    ab_re = mag * jnp.cos(lam_im * dt)
    ab_im = mag * jnp.sin(lam_im * dt)
    den = lam_re * lam_re + lam_im * lam_im
    z_re = ((ab_re - 1.0) * lam_re + ab_im * lam_im) / den
    z_im = (ab_im * lam_re - (ab_re - 1.0) * lam_im) / den
    are_ref[...] = ab_re
    aim_ref[...] = ab_im
    b_re = bre_ref[...]
    b_im = bim_ref[...]
    bbre_ref[...] = z_re[:, None, :] * b_re - z_im[:, None, :] * b_im
    bbim_ref[...] = z_re[:, None, :] * b_im + z_im[:, None, :] * b_re


def _s5_params(lam_re, lam_im, log_dt, b_re, b_im):
    g, p = lam_re.shape
    i = b_re.shape[-1]
    outs = pl.pallas_call(
        _s5_param_kernel,
        out_shape=[jax.ShapeDtypeStruct((g, p), F32)] * 2 + [jax.ShapeDtypeStruct((g, i, p), F32)] * 2,
        name="s5_params",
    )(lam_re, lam_im, log_dt.reshape(g, 1), b_re.transpose(0, 2, 1), b_im.transpose(0, 2, 1))
    return outs


def _s5_kernel(u_ref, bd_ref, cd_ref, are_ref, aim_ref, d_ref, wg_ref, bg_ref, h0r_ref, h0i_ref,
               o_ref, hr_ref, hi_ref, bu_ref, hs_ref, *, steps, nb):
    ns = S5_WIDTH_STATE

    @pl.when(pl.program_id(0) == 0)
    def _():
        hr_ref[...] = h0r_ref[...]
        hi_ref[...] = h0i_ref[...]

    u = u_ref[...].reshape(steps * nb, BRANCH_WIDTH)
    bu_ref[...] = _dot(u, bd_ref[...])
    a_re = jnp.broadcast_to(are_ref[...], (nb, ns))
    a_im = jnp.broadcast_to(aim_ref[...], (nb, ns))

    def step(t, carry):
        h_re, h_im = carry
        rows = pl.ds(pl.multiple_of(t * nb, nb), nb)
        n_re = a_re * h_re - a_im * h_im + bu_ref[rows, 0:ns]
        n_im = a_re * h_im + a_im * h_re + bu_ref[rows, ns:2 * ns]
        hs_ref[rows, 0:ns] = n_re
        hs_ref[rows, ns:2 * ns] = n_im
        return n_re, n_im

    h_re, h_im = lax.fori_loop(0, steps, step, (hr_ref[...], hi_ref[...]))
    hr_ref[...] = h_re
    hi_ref[...] = h_im
    y = _dot(hs_ref[...], cd_ref[...]) + d_ref[...] * u
    z = _gelu(y)
    o = z * _sigmoid(_dot(z, wg_ref[...]) + bg_ref[...])
    o_ref[...] = o.reshape(steps, nb, BRANCH_WIDTH).astype(o_ref.dtype)


def _s5(u_tb, bd, cd, a_re, a_im, d, wg, bg, h0r, h0i, steps):
    t_len, nb, _ = u_tb.shape
    full = lambda a: pl.BlockSpec(a.shape, lambda i: (0,) * a.ndim)
    blk = pl.BlockSpec((steps, nb, BRANCH_WIDTH), lambda i: (i, 0, 0))
    st = pl.BlockSpec((nb, S5_WIDTH_STATE), lambda i: (0, 0))
    return pl.pallas_call(
        functools.partial(_s5_kernel, steps=steps, nb=nb),
        grid=(t_len // steps,),
        in_specs=[blk, full(bd), full(cd), full(a_re), full(a_im), full(d), full(wg), full(bg), st, st],
        out_specs=[blk, st, st],
        out_shape=[jax.ShapeDtypeStruct((t_len, nb, BRANCH_WIDTH), F32),
                   jax.ShapeDtypeStruct((nb, S5_WIDTH_STATE), F32), jax.ShapeDtypeStruct((nb, S5_WIDTH_STATE), F32)],
        scratch_shapes=[pltpu.VMEM((steps * nb, 2 * S5_WIDTH_STATE), F32),
                        pltpu.VMEM((steps * nb, 2 * S5_WIDTH_STATE), F32)],
        compiler_params=_cparams("arbitrary"),
        name="s5",
    )(u_tb, bd, cd, a_re, a_im, d, wg, bg, h0r, h0i)


def _merge_kernel(x_ref, h1_ref, oa_ref, ob_ref, oc_ref, od_ref, g1_ref, sh2_ref, sc2_ref,
                  wg_ref, wb_ref, wo_ref, lg_ref, lb_ref, x1_ref, h2_ref, h2t_ref, *, alpha):
    d = x_ref.shape[1]
    h1 = h1_ref[...]
    merged = None
    for n, br in enumerate((oa_ref, ob_ref, oc_ref, od_ref)):
        gate = _sigmoid(jnp.dot(h1, wg_ref[:, n * d:(n + 1) * d], preferred_element_type=F32))
        up = _dot(br[...], wb_ref[n])
        merged = gate * up if merged is None else merged + gate * up
    mix = _dot(merged, wo_ref[...])
    x1 = _ln(alpha * x_ref[...] + (1.0 + g1_ref[0]) * mix) * lg_ref[...] + lb_ref[...]
    x1_ref[...] = x1
    h2 = _ln(x1) * (1.0 + sc2_ref[0]) + sh2_ref[0]
    h2_ref[...] = h2.astype(h2_ref.dtype)
    h2t_ref[...] = h2.T.astype(h2t_ref.dtype)


def _merge(x, h1, oa, ob, oc, od, g1, sh2, sc2, wg, wb, wo, lg, lb, tm, alpha):
    n, d = x.shape
    nt = n // tm
    mrow = g1.shape[1]
    per_tile = g1.shape[0] == nt and mrow == tm
    tiles_per_seq = nt // g1.shape[0] if not per_tile else 1
    mod_map = (lambda i: (i, 0, 0)) if per_tile else (lambda i: (i // tiles_per_seq, 0, 0))
    mod = pl.BlockSpec((1, mrow, d), mod_map)
    tok = lambda w: pl.BlockSpec((tm, w), lambda i: (i, 0))
    full = lambda a: pl.BlockSpec(a.shape, lambda i: (0,) * a.ndim)
    bw = BRANCH_WIDTH
    return pl.pallas_call(
        functools.partial(_merge_kernel, alpha=alpha),
        grid=(nt,),
        in_specs=[tok(d), tok(d), tok(bw), tok(bw), tok(bw), tok(bw), mod, mod, mod,
                  full(wg), full(wb), full(wo), full(lg), full(lb)],
        out_specs=[tok(d), tok(d), pl.BlockSpec((d, tm), lambda i: (0, i))],
        out_shape=[jax.ShapeDtypeStruct((n, d), F32), jax.ShapeDtypeStruct((n, d), MXU_DTYPE),
                   jax.ShapeDtypeStruct((d, n), MXU_DTYPE)],
        compiler_params=_cparams("parallel"),
        name="merge",
    )(x, h1, oa, ob, oc, od, g1, sh2, sc2, wg, wb, wo, lg, lb)


def _topk_rows(s, k, rid=None, exact=True, want_rank=True):
    if exact and rid is None:
        rid = _iota(s.shape, 0)
    rank = jnp.full(s.shape, float(k), F32) if want_rank else None
    vals = []
    for r in range(k):
        m = jnp.max(s, axis=0, keepdims=True)
        if exact:
            idx = jnp.min(jnp.where(s == m, rid, jnp.iinfo(jnp.int32).max), axis=0, keepdims=True)
            hit = rid == idx
        else:
            hit = s == m
        if want_rank:
            rank = jnp.where(hit, float(r), rank)
        s = jnp.where(hit, -jnp.inf, s)
        vals.append(m)
    taken = rank < float(k) if want_rank else s == -jnp.inf
    return jnp.concatenate(vals, axis=0), rank, taken, jnp.sum(taken.astype(F32), axis=0, keepdims=True)


def _peer_route_kernel(h_ref, wq_ref, sk_ref, cnt_ref, e0_ref, e1_ref, r1_ref, q_ref):
    kk = PEER_TOPK
    half = PEER_DKEY // 2
    q_ref[...] = jnp.dot(h_ref[...], wq_ref[...], preferred_element_type=F32)
    tn = min(PEER_ROUTE_PASS, h_ref.shape[0])
    n_pass = h_ref.shape[0] // tn

    def route(h, toks, exact):
        s, sv, rk = [], [], []
        surplus = jnp.zeros((1, tn), F32)
        for p in range(2):
            c0 = pl.multiple_of(h * PEER_DKEY + p * half, half)
            sp = _dot_nt(sk_ref[p], q_ref[toks, pl.ds(c0, half)])
            v, r, _, n_taken = _topk_rows(sp, kk, exact=exact, want_rank=exact or p == 1)
            surplus = surplus + (n_taken - float(kk))
            s.append(sp)
            sv.append(v)
            rk.append(r)
        grid = sv[0][:, None, :] + sv[1][None, :, :]
        sub = 8
        cand = jnp.concatenate([grid[k0, 0:sub, :] for k0 in range(kk)] + [grid[0, sub:kk, :]], axis=0)
        n_cand = cand.shape[0]
        grp, loc = _iota((n_cand, tn), 0) // sub, _iota((n_cand, tn), 0) % sub
        flat_id = jnp.where(grp < kk, grp * kk + loc, sub + loc)
        _, _, taken, n_taken = _topk_rows(cand, kk, flat_id, exact=exact, want_rank=exact)
        surplus = surplus + (n_taken - float(kk))
        per_grp = jnp.sum(taken.astype(F32).reshape(n_cand // sub, sub, tn), axis=1)
        cnt = per_grp[0:kk, :] + jnp.where(_iota((kk, tn), 0) == 0, per_grp[kk:kk + 1, :], 0.0)
        top = sv[0][0:1, :] + sv[1][0:1, :]
        z = jnp.sum(jnp.where(taken, jnp.exp(cand - top), 0.0), axis=0, keepdims=True)
        cnt_dense = jnp.zeros((PEER_NKEYS, tn), F32)
        for k0 in range(kk):
            is_k0 = rk[0] == float(k0) if exact else s[0] == sv[0][k0:k0 + 1, :]
            cnt_dense = jnp.where(is_k0, cnt[k0:k0 + 1, :], cnt_dense)
        cnt_ref[h, :, toks] = cnt_dense
        e0_ref[h, :, toks] = jnp.exp(s[0] - sv[0][0:1, :])
        e1_ref[h, :, toks] = (jnp.exp(s[1] - sv[1][0:1, :]) / z).astype(e1_ref.dtype)
        r1_ref[h, :, toks] = rk[1].astype(r1_ref.dtype)
        return jnp.max(surplus) > 0.5

    @pl.loop(0, PEER_HEADS * n_pass)
    def _(it):
        h = it // n_pass
        toks = pl.ds(pl.multiple_of((it % n_pass) * tn, tn), tn)
        tie = route(h, toks, exact=False)

        @pl.when(tie)
        def _():
            route(h, toks, exact=True)


def _peer_route(h2, wq, sk, tn):
    n, d = h2.shape
    full = lambda a: pl.BlockSpec(a.shape, lambda i: (0,) * a.ndim)
    out = pl.BlockSpec((PEER_HEADS, PEER_NKEYS, tn), lambda i: (0, 0, i))
    return pl.pallas_call(
        _peer_route_kernel,
        grid=(n // tn,),
        in_specs=[pl.BlockSpec((tn, d), lambda i: (i, 0)), full(wq), full(sk)],
        out_specs=[out] * 4,
        out_shape=[jax.ShapeDtypeStruct((PEER_HEADS, PEER_NKEYS, n), t)
                   for t in (F32, F32, PEER_WEIGHT_DTYPE, PEER_WEIGHT_DTYPE)],
        scratch_shapes=[pltpu.VMEM((tn, PEER_HEADS * PEER_DKEY), F32)],
        compiler_params=_cparams("parallel"),
        name="peer_route",
    )(h2, wq, sk)


PEER_ROUTE_PASS = 256
PEER_EXPERT_TILE = 2048
PEER_PRE_ROWS = 512
PEER_STAGE_ROWS = 256
PEER_LANE_CHUNK = 256


def _peer_expert_kernel(ht_ref, x1_ref, g2_ref, cnt_ref, e0_ref, e1_ref, r1_ref, u_ref, vt_ref, lg_ref, lb_ref,
                        o_ref, acc_ref, g_ref, a_ref, *, alpha):
    e = pl.program_id(1)
    te, tn = a_ref.shape
    stage = min(PEER_STAGE_ROWS, te)
    n_stage = te // stage
    lane_chunk = min(PEER_LANE_CHUNK, tn)

    @pl.when(e == 0)
    def _():
        acc_ref[...] = jnp.zeros(acc_ref.shape, F32)

    pre_rows = min(PEER_PRE_ROWS, te)
    for r0 in range(0, te, pre_rows):
        a_ref[r0:r0 + pre_rows, :] = jnp.dot(u_ref[r0:r0 + pre_rows, :], ht_ref[...], preferred_element_type=F32)
    wdt = e1_ref.dtype

    def weighted_activation(s):
        for c in range(stage // PEER_NKEYS):
            i0_local = s * (stage // PEER_NKEYS) + c
            rows = slice(i0_local * PEER_NKEYS, (i0_local + 1) * PEER_NKEYS)
            row = pl.ds(e * (te // PEER_NKEYS) + i0_local, 1)
            cnt_rows = [cnt_ref[h, row, :].astype(wdt) for h in range(PEER_HEADS)]
            e0_rows = [e0_ref[h, row, :].astype(wdt) for h in range(PEER_HEADS)]
            for t0 in range(0, tn, lane_chunk):
                lanes = slice(t0, t0 + lane_chunk)
                w = jnp.zeros((PEER_NKEYS, lane_chunk), wdt)
                for h in range(PEER_HEADS):
                    hit = r1_ref[h, :, lanes] < cnt_rows[h][:, lanes]
                    w = w + jnp.where(hit, e1_ref[h, :, lanes], jnp.zeros((), wdt)) * e0_rows[h][:, lanes]
                g_ref[rows, lanes] = (w.astype(F32) * _gelu(a_ref[rows, lanes])).astype(g_ref.dtype)

    def project(s):
        rows = slice(s * stage, (s + 1) * stage)
        acc_ref[...] += jnp.dot(vt_ref[:, rows], g_ref[rows, :], preferred_element_type=F32)

    for s in range(n_stage + 1):
        if s < n_stage:
            weighted_activation(s)
        if s >= 1:
            project(s - 1)

    @pl.when(e == pl.num_programs(1) - 1)
    def _():
        ff = acc_ref[...].T
        o_ref[...] = _ln(alpha * x1_ref[...] + (1.0 + g2_ref[0]) * ff) * lg_ref[...] + lb_ref[...]


def _peer_expert(h2t, x1, g2, cnt, e0, e1, r1, u, vt, lg, lb, tn, te, alpha, layer):
    d, n = h2t.shape
    nt = n // tn
    n_exp = u.shape[1]
    mrow = g2.shape[1]
    per_tile = g2.shape[0] == nt and mrow == tn
    tiles_per_seq = nt // g2.shape[0] if not per_tile else 1
    mod_map = (lambda i, e: (i, 0, 0)) if per_tile else (lambda i, e: (i // tiles_per_seq, 0, 0))
    tok = pl.BlockSpec((tn, d), lambda i, e: (i, 0))
    route = pl.BlockSpec((PEER_HEADS, PEER_NKEYS, tn), lambda i, e: (0, 0, i))
    vec = pl.BlockSpec((1, d), lambda i, e: (0, 0))
    return pl.pallas_call(
        functools.partial(_peer_expert_kernel, alpha=alpha),
        grid=(nt, n_exp // te),
        in_specs=[pl.BlockSpec((d, tn), lambda i, e: (0, i)), tok, pl.BlockSpec((1, mrow, d), mod_map),
                  route, route, route, route,
                  pl.BlockSpec((None, te, d), lambda i, e: (layer, e, 0)),
                  pl.BlockSpec((None, d, te), lambda i, e: (layer, 0, e)), vec, vec],
        out_specs=tok,
        out_shape=jax.ShapeDtypeStruct((n, d), F32),
        scratch_shapes=[pltpu.VMEM((d, tn), F32), pltpu.VMEM((te, tn), MXU_DTYPE), pltpu.VMEM((te, tn), F32)],
        compiler_params=_cparams("parallel", "arbitrary"),
        name="peer_expert",
    )(h2t, x1, g2, cnt, e0, e1, r1, u, vt, lg, lb)


def _block_diag_in(bb_t):
    g, i, p = bb_t.shape
    return jnp.einsum('gip,gh->gihp', bb_t, jnp.eye(g, dtype=bb_t.dtype)).reshape(g * i, g * p)


def _block_diag_out(c):
    g, i, p = c.shape
    return jnp.einsum('gip,gh->gphi', c, jnp.eye(g, dtype=c.dtype)).reshape(g * p, g * i)


def _tile(n, pref):
    return pref if n % pref == 0 else n


def _layer_group(x, mods, w, n_seq, seq, past, layer):
    n, d = x.shape
    bw = BRANCH_WIDTH
    alpha = w['alpha']
    tm = _tile(n, 512)
    tm2 = _tile(n, 256)

    def mod_arr(m, t):
        if seq % t == 0:
            return m.reshape(n_seq, 1, d)
        return jnp.repeat(m, seq, axis=0).reshape(n // t, t, d)

    sh1, sc1, g1, sh2, sc2, g2 = mods
    (h1, gq, gk, gv, gr, mq, mk, mv, fq, fk, fv, su, la, sm) = _inproj(
        x, mod_arr(sh1, tm), mod_arr(sc1, tm), w['wa'], w['ws'], w['gw2'], w['gb'], w['fb'], tm)
    logf = sm[:, GLA_GATE_RANK:GLA_GATE_RANK + N_HEADS]

    s0_t = past['gla'].transpose(0, 1, 3, 2)
    oa, gla_t = _gla(gq, gk, gv, la, gr, w['gn'], s0_t, n_seq, seq)
    gla_state = gla_t.transpose(0, 1, 3, 2)

    if past['paged']:
        r3 = lambda a: a.reshape(n_seq, seq, bw)
        lfn = logf.reshape(n_seq, seq, N_HEADS).transpose(0, 2, 1)
        ob, oc = _attn_sample(past['page_table'], r3(mq), r3(mk), r3(mv), r3(fq), r3(fk), r3(fv), lfn,
                              past['moba_k'], past['moba_v'], past['fox_k'], past['fox_v'], past['fox_logf_t'], layer)
        ob = ob.reshape(n, bw)
        oc = oc.reshape(n, bw)
    else:
        lf_rows = logf.reshape(n_seq, seq, N_HEADS).transpose(0, 2, 1).reshape(n_seq * N_HEADS, seq)
        cum = _fox_cumsum(lf_rows).reshape(n_seq, N_HEADS, seq)
        ob = _attn_prompt(mq, mk, mv, None, n_seq, seq, True)
        oc = _attn_prompt(fq, fk, fv, cum, n_seq, seq, False)

    steps = math.gcd(seq, 128)
    u_tb = su.reshape(n_seq, seq, bw).transpose(1, 0, 2)
    od_tb, s5_re, s5_im = _s5(u_tb, w['bd'], w['cd'], w['a_re'], w['a_im'], w['s5_d'], w['wglu'], w['bglu'],
                              past['s5_re'].reshape(n_seq, S5_WIDTH_STATE), past['s5_im'].reshape(n_seq, S5_WIDTH_STATE),
                              steps)
    od = od_tb.transpose(1, 0, 2).reshape(n, bw)

    x1, h2, h2t = _merge(x, h1, oa, ob, oc, od, mod_arr(g1, tm), mod_arr(sh2, tm), mod_arr(sc2, tm),
                         w['wg'], w['wb'], w['wo'], w['ln1_g'], w['ln1_b'], tm, alpha)

    cnt, e0, e1, r1 = _peer_route(h2, w['wq'], w['sk'], tm2)
    x2 = _peer_expert(h2t, x1, mod_arr(g2, tm), cnt, e0, e1, r1, w['pu'], w['pvt'], w['ln2_g'], w['ln2_b'],
                      tm, PEER_EXPERT_TILE, alpha, layer)

    hd = lambda a: a.reshape(n_seq, seq, N_HEADS, HEAD_DIM)
    new = {'moba_k': hd(mk), 'moba_v': hd(mv), 'fox_k': hd(fk), 'fox_v': hd(fv),
           'fox_logf': logf.reshape(n_seq, seq, N_HEADS), 'gla': gla_state,
           's5_re': s5_re.reshape(n_seq, S5_GROUPS, S5_STATE), 's5_im': s5_im.reshape(n_seq, S5_GROUPS, S5_STATE)}
    return x2, new


def _prep_weights(l, depth, d, w_in, gla_w_gate2, gla_b_gate, gla_norm, fox_b_f, s5_lambda_re, s5_lambda_im, s5_log_dt,
                  s5_b_re, s5_b_im, s5_c_re, s5_c_im, s5_d, s5_w_glu, s5_b_glu, w_branch, w_out, ln1_g, ln1_b,
                  peer_w_q, peer_sub_keys, peer_u, peer_v, ln2_g, ln2_b):
    bw = BRANCH_WIDTH
    bf = lambda a: a.astype(MXU_DTYPE)
    pad_small = LANES - GLA_GATE_RANK - N_HEADS
    w = {}
    w['wa'], w['ws'], w['wg'] = _repack_w_in(w_in, l)
    w['gw2'] = bf(jnp.concatenate([gla_w_gate2[l], jnp.zeros((LANES - GLA_GATE_RANK, bw), F32)], axis=0))
    w['gb'] = gla_b_gate[l].reshape(1, bw)
    w['gn'] = gla_norm[l].reshape(1, bw)
    w['fb'] = jnp.concatenate([jnp.zeros((GLA_GATE_RANK,), F32), fox_b_f[l], jnp.zeros((pad_small,), F32)]).reshape(1, LANES)
    a_re, a_im, bb_re, bb_im = _s5_params(s5_lambda_re[l], s5_lambda_im[l], s5_log_dt[l], s5_b_re[l], s5_b_im[l])
    w['a_re'] = a_re.reshape(1, S5_WIDTH_STATE)
    w['a_im'] = a_im.reshape(1, S5_WIDTH_STATE)
    w['bd'] = bf(jnp.concatenate([_block_diag_in(bb_re), _block_diag_in(bb_im)], axis=1))
    w['cd'] = bf(jnp.concatenate([_block_diag_out(s5_c_re[l]), -_block_diag_out(s5_c_im[l])], axis=0))
    w['s5_d'] = s5_d[l].reshape(1, bw)
    w['wglu'] = bf(s5_w_glu[l])
    w['bglu'] = s5_b_glu[l].reshape(1, bw)
    w['wb'] = bf(w_branch[l])
    w['wo'] = bf(w_out[l])
    w['ln1_g'] = ln1_g[l].reshape(1, d)
    w['ln1_b'] = ln1_b[l].reshape(1, d)
    w['wq'] = bf(peer_w_q[l])
    w['sk'] = bf(peer_sub_keys[l])
    w['pu'] = peer_u
    w['pvt'] = peer_v
    w['ln2_g'] = ln2_g[l].reshape(1, d)
    w['ln2_b'] = ln2_b[l].reshape(1, d)
    w['alpha'] = (2.0 * depth) ** 0.25
    return w


def kernel(x_prompt, x_sample, c_prompt, c_sample, page_table, cache_moba_k, cache_moba_v, cache_fox_k, cache_fox_v, cache_fox_logf, state_gla, state_s5_re, state_s5_im, w_mod, b_mod, w_in, gla_w_gate2, gla_b_gate, gla_norm, fox_b_f, s5_lambda_re, s5_lambda_im, s5_log_dt, s5_b_re, s5_b_im, s5_c_re, s5_c_im, s5_d, s5_w_glu, s5_b_glu, w_branch, w_out, ln1_g, ln1_b, peer_w_q, peer_sub_keys, peer_u, peer_v, ln2_g, ln2_b):
    bp, seq, d = x_prompt.shape
    bs, dseq, _ = x_sample.shape
    depth = w_in.shape[0]
    n_pool, page = cache_moba_k.shape[1], cache_moba_k.shape[2]
    flat_cache = lambda c: c.transpose(0, 1, 3, 4, 2).reshape(depth, n_pool, BRANCH_WIDTH, page)
    paged = {'paged': True, 'page_table': page_table,
             'moba_k': flat_cache(cache_moba_k), 'moba_v': flat_cache(cache_moba_v),
             'fox_k': flat_cache(cache_fox_k), 'fox_v': flat_cache(cache_fox_v),
             'fox_logf_t': cache_fox_logf.transpose(0, 1, 3, 2)}
    c_all = jnp.concatenate([c_prompt, c_sample], axis=0)
    y_p = x_prompt.reshape(bp * seq, d)
    y_s = x_sample.reshape(bs * dseq, d)
    st_p, st_s = [], []
    pu_all = peer_u.astype(MXU_DTYPE)
    pvt_all = peer_v.astype(MXU_DTYPE).transpose(0, 2, 1)
    for l in range(depth):
        w = _prep_weights(l, depth, d, w_in, gla_w_gate2, gla_b_gate, gla_norm, fox_b_f, s5_lambda_re, s5_lambda_im,
                          s5_log_dt, s5_b_re, s5_b_im, s5_c_re, s5_c_im, s5_d, s5_w_glu, s5_b_glu, w_branch, w_out,
                          ln1_g, ln1_b, peer_w_q, peer_sub_keys, pu_all, pvt_all, ln2_g, ln2_b)
        mod = _modulation(c_all, w_mod, b_mod, l)
        mods = jnp.split(mod, 6, axis=-1)
        past_p = {'paged': False, 'gla': jnp.zeros((bp, N_HEADS, HEAD_DIM, HEAD_DIM), F32),
                  's5_re': jnp.zeros((bp, S5_GROUPS, S5_STATE), F32), 's5_im': jnp.zeros((bp, S5_GROUPS, S5_STATE), F32)}
        past_s = dict(paged, gla=state_gla[l], s5_re=state_s5_re[l], s5_im=state_s5_im[l])
        y_p, new_p = _layer_group(y_p, [m[:bp] for m in mods], w, bp, seq, past_p, l)
        y_s, new_s = _layer_group(y_s, [m[bp:] for m in mods], w, bs, dseq, past_s, l)
        st_p.append(new_p)
        st_s.append(new_s)

    stk = lambda states, name: jnp.stack([s[name] for s in states])
    names = ('moba_k', 'moba_v', 'fox_k', 'fox_v', 'fox_logf', 'gla', 's5_re', 's5_im')
    return ((y_p.reshape(bp, seq, d), y_s.reshape(bs, dseq, d))
            + tuple(stk(st_p, nm) for nm in names) + tuple(stk(st_s, nm) for nm in names))
```

```python
import functools
import math

import numpy as np
import jax
import jax.numpy as jnp
from jax import lax
from jax.experimental import pallas as pl
from jax.experimental.pallas import tpu as pltpu

F32 = jnp.float32
MXU_DTYPE = jnp.bfloat16
PEER_WEIGHT_DTYPE = jnp.bfloat16

HEAD_DIM = 64
N_HEADS = 4
BRANCH_WIDTH = N_HEADS * HEAD_DIM
N_BRANCH = 4
GLA_GATE_RANK = 16
GLA_TAU = 16.0
GLA_CHUNK = 64
GLA_SUBCHUNK = 16
GLA_UNROLL = 4
GLA_GROUP_ROWS = 64
MOBA_BLOCK = 256
MOBA_TOPK = 3
S5_GROUP = 16
S5_GROUPS = BRANCH_WIDTH // S5_GROUP
S5_STATE = 64
S5_WIDTH_STATE = S5_GROUPS * S5_STATE
PEER_HEADS = 8
PEER_DKEY = 256
PEER_NKEYS = 128
PEER_TOPK = 16
LN_EPS = 1e-5
NEG_BIG = -1e30

V7X_VMEM_LIMIT = 56 * 1024 * 1024
LANES = 128


def _cparams(*sem):
    return pltpu.CompilerParams(dimension_semantics=tuple(sem), vmem_limit_bytes=V7X_VMEM_LIMIT)


def _dot(a, b):
    return jnp.dot(a.astype(MXU_DTYPE), b.astype(MXU_DTYPE), preferred_element_type=F32)


def _dot_nt(a, b):
    return lax.dot_general(a.astype(MXU_DTYPE), b.astype(MXU_DTYPE), (((1,), (1,)), ((), ())),
                           preferred_element_type=F32)


def _dot_tn(a, b):
    return lax.dot_general(a.astype(MXU_DTYPE), b.astype(MXU_DTYPE), (((0,), (0,)), ((), ())),
                           preferred_element_type=F32)


def _split3(x):
    hi = x.astype(jnp.bfloat16)
    r1 = x - hi.astype(F32)
    mid = r1.astype(jnp.bfloat16)
    lo = (r1 - mid.astype(F32)).astype(jnp.bfloat16)
    return hi, mid, lo


def _dot_exact01(x, m01):
    hi, mid, lo = _split3(x)
    m = m01.astype(jnp.bfloat16)
    d = lambda a: jnp.dot(a, m, preferred_element_type=F32)
    return d(hi) + d(mid) + d(lo)


def _dot01_exact(m01, x):
    hi, mid, lo = _split3(x)
    m = m01.astype(jnp.bfloat16)
    d = lambda a: jnp.dot(m, a, preferred_element_type=F32)
    return d(hi) + d(mid) + d(lo)


def _ln(x):
    mu = jnp.mean(x, axis=-1, keepdims=True)
    xc = x - mu
    var = jnp.mean(xc * xc, axis=-1, keepdims=True)
    return xc * lax.rsqrt(var + LN_EPS)


def _sigmoid(x):
    return 1.0 / (1.0 + jnp.exp(-x))


def _log_sigmoid(x):
    return jnp.minimum(x, 0.0) - jnp.log1p(jnp.exp(-jnp.abs(x)))


def _gelu(x):
    return 0.5 * x * (1.0 + lax.erf(x * (1.0 / math.sqrt(2.0))))


def _iota(shape, axis):
    return lax.broadcasted_iota(jnp.int32, shape, axis)


def _mod_kernel(c_ref, w_ref, b_ref, o_ref):
    c = c_ref[...]
    o_ref[...] = _dot(c * _sigmoid(c), w_ref[...]) + b_ref[...]


def _modulation(c_all, w_mod, b_mod, layer):
    n, d = c_all.shape
    depth, _, width = w_mod.shape
    tn = _tile(width, 1536)
    return pl.pallas_call(
        _mod_kernel,
        grid=(width // tn,),
        in_specs=[pl.BlockSpec((n, d), lambda j: (0, 0)),
                  pl.BlockSpec((None, d, tn), lambda j: (layer, 0, j)),
                  pl.BlockSpec((None, 1, tn), lambda j: (layer, 0, j))],
        out_specs=pl.BlockSpec((n, tn), lambda j: (0, j)),
        out_shape=jax.ShapeDtypeStruct((n, width), F32),
        compiler_params=_cparams("parallel"),
        name="modulation",
    )(c_all, w_mod, b_mod.reshape(depth, 1, width))


N_PROJ = 11
PROJ_SEGMENTS = (0, 1, 2, 3, 5, 6, 7, 8, 9, 10, 12)
SEG_GATE_A, SEG_GATE_F, SEG_BRANCH_GATE = 4, 11, 13


def _in_offsets(d):
    bw = BRANCH_WIDTH
    sizes = (bw,) * 4 + (GLA_GATE_RANK,) + (bw,) * 3 + (bw,) * 3 + (N_HEADS, bw, N_BRANCH * d)
    return [int(o) for o in np.concatenate([[0], np.cumsum(sizes)])]


def _repack_kernel(w_ref, wa_ref, ws_ref, wg_ref):
    rows, d_gate = wg_ref.shape
    offs = _in_offsets(d_gate // N_BRANCH)
    bw = BRANCH_WIDTH
    for i, seg in enumerate(PROJ_SEGMENTS):
        wa_ref[:, i * bw:(i + 1) * bw] = w_ref[:, offs[seg]:offs[seg] + bw].astype(wa_ref.dtype)
    small = jnp.concatenate([w_ref[:, offs[SEG_GATE_A]:offs[SEG_GATE_A + 1]],
                             w_ref[:, offs[SEG_GATE_F]:offs[SEG_GATE_F + 1]],
                             jnp.zeros((rows, LANES - GLA_GATE_RANK - N_HEADS), F32)], axis=1)
    ws_ref[...] = small.astype(ws_ref.dtype)
    wg_ref[...] = w_ref[:, offs[SEG_BRANCH_GATE]:offs[SEG_BRANCH_GATE + 1]].astype(wg_ref.dtype)


def _repack_w_in(w_in, layer):
    _, d, width = w_in.shape
    tr = _tile(d, 256)
    out_w = (N_PROJ * BRANCH_WIDTH, LANES, N_BRANCH * d)
    return pl.pallas_call(
        _repack_kernel,
        grid=(d // tr,),
        in_specs=[pl.BlockSpec((None, tr, width), lambda i: (layer, i, 0))],
        out_specs=[pl.BlockSpec((tr, w), lambda i: (i, 0)) for w in out_w],
        out_shape=[jax.ShapeDtypeStruct((d, w), MXU_DTYPE) for w in out_w],
        compiler_params=_cparams("parallel"),
        name="repack_w_in",
    )(w_in)


def _inproj_kernel(x_ref, sh_ref, sc_ref, wa_ref, ws_ref, gw2_ref, gb_ref, fb_ref,
                   h1_ref, gq_ref, gk_ref, gv_ref, gr_ref, mq_ref, mk_ref, mv_ref, fq_ref, fk_ref, fv_ref,
                   su_ref, la_ref, sm_ref):
    h1 = (_ln(x_ref[...]) * (1.0 + sc_ref[0]) + sh_ref[0]).astype(MXU_DTYPE)
    h1_ref[...] = h1
    outs = (gq_ref, gk_ref, gv_ref, gr_ref, mq_ref, mk_ref, mv_ref, fq_ref, fk_ref, fv_ref, su_ref)
    for i, r in enumerate(outs):
        r[...] = jnp.dot(h1, wa_ref[:, i * BRANCH_WIDTH:(i + 1) * BRANCH_WIDTH],
                         preferred_element_type=F32).astype(r.dtype)
    ps = jnp.dot(h1, ws_ref[...], preferred_element_type=F32)
    la_ref[...] = _log_sigmoid(_dot(ps, gw2_ref[...]) + gb_ref[...]) * (1.0 / GLA_TAU)
    sm_ref[...] = _log_sigmoid(ps + fb_ref[...])


def _inproj(x, sh, sc, wa, ws, gw2, gb, fb, tm):
    n, d = x.shape
    nt = n // tm
    mrow = sh.shape[1]
    per_tile = sh.shape[0] == nt and mrow == tm
    tiles_per_seq = nt // sh.shape[0] if not per_tile else 1
    mod_map = (lambda i: (i, 0, 0)) if per_tile else (lambda i: (i // tiles_per_seq, 0, 0))
    tok = lambda w: pl.BlockSpec((tm, w), lambda i: (i, 0))
    full = lambda a: pl.BlockSpec(a.shape, lambda i: (0,) * a.ndim)
    bw = BRANCH_WIDTH
    odt = [MXU_DTYPE] + [F32] * (N_PROJ + 2)
    owd = [d] + [bw] * N_PROJ + [bw, LANES]
    return pl.pallas_call(
        _inproj_kernel,
        grid=(nt,),
        in_specs=[tok(d), pl.BlockSpec((1, mrow, d), mod_map), pl.BlockSpec((1, mrow, d), mod_map),
                  full(wa), full(ws), full(gw2), full(gb), full(fb)],
        out_specs=[tok(w) for w in owd],
        out_shape=[jax.ShapeDtypeStruct((n, w), t) for w, t in zip(owd, odt)],
        compiler_params=_cparams("parallel"),
        name="inproj",
    )(x, sh, sc, wa, ws, gw2, gb, fb)


def _gla_kernel(q_ref, k_ref, v_ref, la_ref, gr_ref, gn_ref, s0_ref, o_ref, sT_ref, *, seq, chunk, sub, group):
    n_chunks = seq // chunk
    n_sub = chunk // sub
    scale = HEAD_DIM ** -0.5
    tril = (_iota((chunk, chunk), 1) <= _iota((chunk, chunk), 0))
    gn = gn_ref[...]
    heads = range(N_HEADS)
    seqs = range(group)
    sls = [slice(h * HEAD_DIM, (h + 1) * HEAD_DIM) for h in heads]

    def one_chunk(c, state):
        def rows_of(g):
            r0 = g * seq + c * chunk
            return pl.ds(r0 if isinstance(r0, int) else pl.multiple_of(r0, chunk), chunk)

        rows = [rows_of(g) for g in seqs]
        bcum = [_dot01_exact(tril, la_ref[rows[g], :]) for g in seqs]
        prep = []
        for g in seqs:
            q = q_ref[rows[g], :] * scale
            k = k_ref[rows[g], :]
            v = v_ref[rows[g], :]
            b = bcum[g]
            b_end = b[chunk - 1:chunk, :]
            qe = q * jnp.exp(b)
            kd = k * jnp.exp(b_end - b)
            qs, ks = [], []
            for i in range(n_sub):
                base = b[i * sub - 1:i * sub, :] if i > 0 else jnp.zeros_like(b_end)
                nk = (i + 1) * sub
                qs.append(q[i * sub:nk, :] * jnp.exp(b[i * sub:nk, :] - base))
                ks.append(k[:nk, :] * jnp.exp(base - b[:nk, :]))
            prep.append((v, b_end, qe, kd, qs, ks))
        o_inter = [[_dot_nt(prep[g][2][:, sls[h]], state[g][h]) for h in heads] for g in seqs]
        raw = [[[_dot_nt(prep[g][4][i][:, sls[h]], prep[g][5][i][:, sls[h]]) for i in range(n_sub)]
                for h in heads] for g in seqs]
        kv = [[_dot_tn(prep[g][0][:, sls[h]], prep[g][3][:, sls[h]]) for h in heads] for g in seqs]
        atts = []
        for g in seqs:
            per_head = []
            for h in heads:
                att_rows = []
                for i in range(n_sub):
                    nk = (i + 1) * sub
                    causal = _iota((sub, nk), 1) <= _iota((sub, nk), 0) + i * sub
                    a = jnp.where(causal, raw[g][h][i], 0.0)
                    if nk < chunk:
                        a = jnp.concatenate([a, jnp.zeros((sub, chunk - nk), F32)], axis=1)
                    att_rows.append(a)
                per_head.append(att_rows[0] if n_sub == 1 else jnp.concatenate(att_rows, axis=0))
            atts.append(per_head)
        o_intra = [[_dot(atts[g][h], prep[g][0][:, sls[h]]) for h in heads] for g in seqs]
        new_state = []
        for g in seqs:
            b_end = prep[g][1]
            outs = []
            for h in heads:
                o_h = o_inter[g][h] + o_intra[g][h]
                outs.append(o_h * lax.rsqrt(jnp.mean(o_h * o_h, axis=-1, keepdims=True) + LN_EPS))
            gr = gr_ref[rows[g], :]
            o_ref[rows[g], :] = (jnp.concatenate(outs, axis=1) * gn * (gr * _sigmoid(gr))).astype(o_ref.dtype)
            new_state.append(tuple(state[g][h] * jnp.exp(b_end[:, sls[h]]) + kv[g][h] for h in heads))
        return tuple(new_state)

    state0 = tuple(tuple(s0_ref[g, h] for h in heads) for g in seqs)
    if n_chunks == 1:
        state = one_chunk(0, state0)
    else:
        state = lax.fori_loop(0, n_chunks, one_chunk, state0, unroll=math.gcd(n_chunks, GLA_UNROLL))
    for g in seqs:
        for h in heads:
            sT_ref[g, h] = state[g][h]


def _gla(gq, gk, gv, la, gr, gn, s0_t, n_seq, seq):
    chunk = math.gcd(seq, GLA_CHUNK)
    sub = math.gcd(chunk, GLA_SUBCHUNK)
    group = math.gcd(n_seq, max(1, GLA_GROUP_ROWS // seq))
    tok = pl.BlockSpec((group * seq, BRANCH_WIDTH), lambda b: (b, 0))
    st = pl.BlockSpec((group, N_HEADS, HEAD_DIM, HEAD_DIM), lambda b: (b, 0, 0, 0))
    return pl.pallas_call(
        functools.partial(_gla_kernel, seq=seq, chunk=chunk, sub=sub, group=group),
        grid=(n_seq // group,),
        in_specs=[tok, tok, tok, tok, tok, pl.BlockSpec((1, BRANCH_WIDTH), lambda b: (0, 0)), st],
        out_specs=[tok, st],
        out_shape=[jax.ShapeDtypeStruct((n_seq * seq, BRANCH_WIDTH), F32),
                   jax.ShapeDtypeStruct((n_seq, N_HEADS, HEAD_DIM, HEAD_DIM), F32)],
        compiler_params=_cparams("parallel"),
        name="gla",
    )(gq, gk, gv, la, gr, gn, s0_t)


def _cum_kernel(lf_ref, o_ref):
    rows, length = lf_ref.shape
    triu = _iota((LANES, LANES), 0) <= _iota((LANES, LANES), 1)
    carry = jnp.zeros((rows, 1), F32)
    for c in range(length // LANES):
        cs = _dot_exact01(lf_ref[:, c * LANES:(c + 1) * LANES], triu) + carry
        o_ref[:, c * LANES:(c + 1) * LANES] = cs
        carry = cs[:, LANES - 1:LANES]


def _fox_cumsum(lf_rows):
    return pl.pallas_call(
        _cum_kernel,
        out_shape=jax.ShapeDtypeStruct(lf_rows.shape, F32),
        compiler_params=pltpu.CompilerParams(vmem_limit_bytes=V7X_VMEM_LIMIT),
        name="fox_cumsum",
    )(lf_rows)


def _alibi_slope(h):
    return 2.0 ** (-8.0 * (h + 1) / N_HEADS)


def _attn_prompt_kernel(q_ref, k_ref, v_ref, *rest, seq, moba):
    if moba:
        o_ref, kx_ref, vt_ref, km_ref = rest
    else:
        cum_ref, o_ref, kx_ref, vt_ref = rest
    blk = MOBA_BLOCK
    nb = seq // blk
    i = pl.program_id(1)
    nbp = km_ref.shape[1] if moba else 0
    heads = range(N_HEADS)
    cols = [slice(h * HEAD_DIM, (h + 1) * HEAD_DIM) for h in heads]
    n_feat = kx_ref.shape[2] - HEAD_DIM

    @pl.when(i == 0)
    def _():
        pos = _iota((seq, n_feat), 0)
        slot = _iota((seq, n_feat), 1)
        for h in heads:
            kf = k_ref[:, cols[h]]
            if moba:
                km = jnp.sum(kf.reshape(nb, blk, HEAD_DIM), axis=1) * (1.0 / blk)
                km_ref[h] = jnp.concatenate([km, jnp.zeros((nbp - nb, HEAD_DIM), F32)], axis=0) if nbp > nb else km
                feat = jnp.where(slot < nbp, (slot == pos // blk).astype(F32),
                                 jnp.where(slot == nbp, (pos // blk).astype(F32),
                                           jnp.where(slot == nbp + 1, (pos % blk).astype(F32), 0.0)))
            else:
                hi, mid, lo = (p.astype(F32) for p in _split3(cum_ref[0, h:h + 1, :]))
                part = _iota((n_feat, seq), 0)
                feat = jnp.where(part == 0, hi, jnp.where(part == 1, mid, jnp.where(part == 2, lo, 0.0))).T
            kx_ref[h, :, 0:HEAD_DIM] = kf.astype(MXU_DTYPE)
            kx_ref[h, :, HEAD_DIM:HEAD_DIM + n_feat] = feat.astype(MXU_DTYPE)
            vt_ref[h] = v_ref[:, cols[h]].T.astype(MXU_DTYPE)

    q_t = q_ref[...].T
    qx = []
    for h in heads:
        q_h = q_t[h * HEAD_DIM:(h + 1) * HEAD_DIM, :]
        if moba:
            bs = _dot(km_ref[h], q_h)
            jrow = _iota((nbp, blk), 0)
            bs = jnp.where(jrow < i, bs, -jnp.inf)
            rank = jnp.zeros((nbp, blk), F32)
            for j2 in range(nb):
                row = bs[j2:j2 + 1, :]
                beats = (row > bs) | ((row == bs) & (j2 < jrow))
                rank = rank + beats.astype(F32)
            sel = ((jrow < i) & (rank < MOBA_TOPK)) | (jrow == i)
            frow = _iota((n_feat - nbp, blk), 0)
            slope = _alibi_slope(h)
            extra = jnp.concatenate([jnp.where(sel, 0.0, NEG_BIG),
                                     jnp.where(frow == 0, slope * blk, jnp.where(frow == 1, slope, 0.0))], axis=0)
        else:
            extra = jnp.where(_iota((n_feat, blk), 0) < 3, -1.0, 0.0)
        qx.append(jnp.concatenate([q_h * (HEAD_DIM ** -0.5), extra], axis=0).astype(MXU_DTYPE))

    def scores(h, c0):
        return jnp.dot(kx_ref[h, pl.ds(c0, blk), :], qx[h], preferred_element_type=F32)

    def block_update(states, s_all, c0):
        parts = []
        for h in heads:
            m, l, acc = states[h]
            m_new = jnp.maximum(m, jnp.max(s_all[h], axis=0, keepdims=True))
            alpha = jnp.exp(m - m_new)
            p = jnp.exp(s_all[h] - m_new)
            parts.append((m_new, alpha, alpha * l + jnp.sum(p, axis=0, keepdims=True), p.astype(MXU_DTYPE)))
        return tuple((m_new, l_new, alpha * states[h][2] + _dot(vt_ref[h, :, pl.ds(c0, blk)], p))
                     for h, (m_new, alpha, l_new, p) in zip(heads, parts))

    causal_t = _iota((blk, blk), 0) <= _iota((blk, blk), 1)
    init = tuple((jnp.full((1, blk), NEG_BIG, F32), jnp.zeros((1, blk), F32), jnp.zeros((HEAD_DIM, blk), F32))
                 for _ in heads)
    c_own = pl.multiple_of(i * blk, blk)
    states = block_update(init, [jnp.where(causal_t, scores(h, c_own), NEG_BIG) for h in heads], c_own)

    def body(j, states):
        c0 = pl.multiple_of(j * blk, blk)
        return block_update(states, [scores(h, c0) for h in heads], c0)

    states = lax.fori_loop(0, i, body, states)
    o_ref[...] = jnp.concatenate([acc / l for _, l, acc in states], axis=0).T.astype(o_ref.dtype)


def _attn_prompt(q, k, v, cum, n_seq, seq, moba):
    blk = MOBA_BLOCK
    nq = seq // blk
    nbp = -(-nq // 8) * 8
    tile = pl.BlockSpec((blk, BRANCH_WIDTH), lambda b, i: (b * nq + i, 0))
    whole = pl.BlockSpec((seq, BRANCH_WIDTH), lambda b, i: (b, 0))
    in_specs = [tile, whole, whole]
    args = [q, k, v]
    scratch = [pltpu.VMEM((N_HEADS, seq, 2 * HEAD_DIM), MXU_DTYPE), pltpu.VMEM((N_HEADS, HEAD_DIM, seq), MXU_DTYPE)]
    if moba:
        scratch += [pltpu.VMEM((N_HEADS, nbp, HEAD_DIM), F32)]
    else:
        in_specs.append(pl.BlockSpec((1, N_HEADS, seq), lambda b, i: (b, 0, 0)))
        args.append(cum)
    return pl.pallas_call(
        functools.partial(_attn_prompt_kernel, seq=seq, moba=moba),
        grid=(n_seq, nq),
        in_specs=in_specs,
        out_specs=tile,
        out_shape=jax.ShapeDtypeStruct((n_seq * seq, BRANCH_WIDTH), F32),
        scratch_shapes=scratch,
        compiler_params=_cparams("parallel", "arbitrary"),
        name="moba_prompt" if moba else "fox_prompt",
    )(*args)


def _block_diag_queries(q):
    t = q.shape[0]
    col_head = _iota((t, BRANCH_WIDTH), 1) // HEAD_DIM
    return jnp.concatenate([jnp.where(col_head == h, q, jnp.zeros_like(q)) for h in range(N_HEADS)], axis=0)


def _head_diag(acc, t):
    return jnp.concatenate([acc[h * t:(h + 1) * t, h * HEAD_DIM:(h + 1) * HEAD_DIM] for h in range(N_HEADS)], axis=1)


def _attn_sample_kernel(pt_ref, mq_ref, mkn_ref, mvn_ref, fq_ref, fkn_ref, fvn_ref, lfn_ref, *refs,
                        t_new, n_pages, page, past_len):
    del pt_ref
    mk, mv, fk, fv, lfc = (refs[i * n_pages:(i + 1) * n_pages] for i in range(5))
    ob_ref, oc_ref = refs[5 * n_pages:]
    rows = N_HEADS * t_new
    scale = HEAD_DIM ** -0.5
    row_head = _iota((rows, 1), 0) // t_new
    row_t = _iota((rows, 1), 0) % t_new
    lane_n = _iota((rows, t_new), 1)
    causal = lane_n <= row_t
    cast = lambda r: r[...].astype(MXU_DTYPE)

    def softmax(s_past, s_new):
        m = jnp.maximum(jnp.max(s_past, axis=-1, keepdims=True), jnp.max(s_new, axis=-1, keepdims=True))
        p_past = jnp.exp(s_past - m)
        p_new = jnp.exp(s_new - m)
        l = jnp.sum(p_past, axis=-1, keepdims=True) + jnp.sum(p_new, axis=-1, keepdims=True)
        return p_past.astype(MXU_DTYPE), p_new, l

    def weighted_values(p_past, p_new, l, v_pages, v_new):
        acc = _dot(p_new, v_new)
        for pg in range(n_pages):
            acc = acc + _dot_nt(p_past[:, pg * page:(pg + 1) * page], cast(v_pages[pg]))
        return _head_diag(acc / l, t_new)

    qm = _block_diag_queries(mq_ref[...]).astype(MXU_DTYPE)
    qf = _block_diag_queries(fq_ref[...]).astype(MXU_DTYPE)
    pages_per_block = MOBA_BLOCK // page
    n_blocks = n_pages // pages_per_block
    qk_m = jnp.concatenate([jnp.dot(qm, cast(mk[pg]), preferred_element_type=F32) for pg in range(n_pages)], axis=1)
    qk_m_new = _dot_nt(qm, mkn_ref[...])
    qk_f = jnp.concatenate([jnp.dot(qf, cast(fk[pg]), preferred_element_type=F32) for pg in range(n_pages)], axis=1)
    qk_f_new = _dot_nt(qf, fkn_ref[...])
    lf_all = jnp.concatenate([lfc[pg][...] for pg in range(n_pages)], axis=0)
    triu = _iota((page, page), 0) <= _iota((page, page), 1)
    local = _dot_exact01(lf_all, triu)
    kmean_cols = []
    for b in range(n_blocks):
        blk = jnp.concatenate([mk[b * pages_per_block + i][...] for i in range(pages_per_block)], axis=1)
        kmean_cols.append(jnp.sum(blk, axis=1, keepdims=True) * (1.0 / MOBA_BLOCK))
    bs = _dot(qm, jnp.concatenate(kmean_cols, axis=1))
    nr = n_pages * N_HEADS
    ri, ci = _iota((nr, nr), 0), _iota((nr, nr), 1)
    earlier_page = (ci % N_HEADS == ri % N_HEADS) & (ci // N_HEADS < ri // N_HEADS)
    cum = local + _dot01_exact(earlier_page, local[:, page - 1:page])

    jl = _iota((rows, n_blocks), 1)
    rank = jnp.zeros((rows, n_blocks), F32)
    for j2 in range(n_blocks):
        col = bs[:, j2:j2 + 1]
        rank = rank + ((col > bs) | ((col == bs) & (j2 < jl))).astype(F32)
    sel = (rank < MOBA_TOPK).astype(F32)
    sel_keys = jnp.concatenate(
        [jnp.broadcast_to(sel[:, b:b + 1], (rows, MOBA_BLOCK)) for b in range(n_blocks)], axis=1) > 0.5
    slope = jnp.zeros((rows, 1), F32)
    for h in range(N_HEADS):
        slope = jnp.where(row_head == h, _alibi_slope(h), slope)
    dist = (past_len + row_t - _iota((rows, past_len), 1)).astype(F32)
    s_m = jnp.where(sel_keys, qk_m * scale - slope * dist, NEG_BIG)
    s_m_new = jnp.where(causal, qk_m_new * scale - slope * (row_t - lane_n).astype(F32), NEG_BIG)
    soft_m = softmax(s_m, s_m_new)

    bias = jnp.concatenate(
        [jnp.broadcast_to(cum[pg * N_HEADS:(pg + 1) * N_HEADS, None, :], (N_HEADS, t_new, page)).reshape(rows, page)
         for pg in range(n_pages)], axis=1)
    s_f = qk_f * scale - bias
    lfn = lfn_ref[...]
    cols, c = [], cum[nr - N_HEADS:nr, page - 1:page]
    for t in range(t_new):
        c = c + lfn[:, t:t + 1]
        cols.append(c)
    cum_n = jnp.concatenate(cols, axis=1)
    bias_n = jnp.broadcast_to(cum_n[:, None, :], (N_HEADS, t_new, t_new)).reshape(rows, t_new)
    s_f_new = jnp.where(causal, qk_f_new * scale - bias_n, NEG_BIG)
    soft_f = softmax(s_f, s_f_new)

    ob_ref[...] = weighted_values(*soft_m, mv, mvn_ref[...])
    oc_ref[...] = weighted_values(*soft_f, fv, fvn_ref[...])


def _attn_sample(page_table, mq, mkn, mvn, fq, fkn, fvn, lfn, mkc, mvc, fkc, fvc, lfc, layer):
    n_seq, t_new, _ = mq.shape
    n_pages = page_table.shape[1]
    page = mkc.shape[3]
    past_len = n_pages * page
    assert past_len % MOBA_BLOCK == 0 and MOBA_BLOCK % page == 0 and t_new <= MOBA_BLOCK
    new = pl.BlockSpec((None, t_new, BRANCH_WIDTH), lambda b, pt: (b, 0, 0))

    def pages(width):
        return [pl.BlockSpec((None, None, width, page), lambda b, pt, pg=pg: (layer, pt[b, pg], 0, 0))
                for pg in range(n_pages)]

    grid_spec = pltpu.PrefetchScalarGridSpec(
        num_scalar_prefetch=1,
        grid=(n_seq,),
        in_specs=[new, new, new, new, new, new, pl.BlockSpec((None, N_HEADS, t_new), lambda b, pt: (b, 0, 0))]
                 + 4 * pages(BRANCH_WIDTH) + pages(N_HEADS),
        out_specs=[new, new],
    )
    return pl.pallas_call(
        functools.partial(_attn_sample_kernel, t_new=t_new, n_pages=n_pages, page=page, past_len=past_len),
        grid_spec=grid_spec,
        out_shape=[jax.ShapeDtypeStruct((n_seq, t_new, BRANCH_WIDTH), F32)] * 2,
        compiler_params=_cparams("parallel"),
        name="attn_sample",
    )(page_table, mq, mkn, mvn, fq, fkn, fvn, lfn,
      *([mkc] * n_pages + [mvc] * n_pages + [fkc] * n_pages + [fvc] * n_pages + [lfc] * n_pages))


def _s5_param_kernel(lre_ref, lim_ref, ldt_ref, bre_ref, bim_ref, are_ref, aim_ref, bbre_ref, bbim_ref):
    lam_re = lre_ref[...]
    lam_im = lim_ref[...]
    dt = jnp.exp(ldt_ref[...])
    mag = jnp.exp(lam_re * dt)
    ab_re = mag * jnp.cos(lam_im * dt)
    ab_im = mag * jnp.sin(lam_im * dt)
    den = lam_re * lam_re + lam_im * lam_im
    z_re = ((ab_re - 1.0) * lam_re + ab_im * lam_im) / den
    z_im = (ab_im * lam_re - (ab_re - 1.0) * lam_im) / den
    are_ref[...] = ab_re
    aim_ref[...] = ab_im
    b_re = bre_ref[...]
    b_im = bim_ref[...]
    bbre_ref[...] = z_re[:, None, :] * b_re - z_im[:, None, :] * b_im
    bbim_ref[...] = z_re[:, None, :] * b_im + z_im[:, None, :] * b_re


def _s5_params(lam_re, lam_im, log_dt, b_re, b_im):
    g, p = lam_re.shape
    i = b_re.shape[-1]
    outs = pl.pallas_call(
        _s5_param_kernel,
        out_shape=[jax.ShapeDtypeStruct((g, p), F32)] * 2 + [jax.ShapeDtypeStruct((g, i, p), F32)] * 2,
        name="s5_params",
    )(lam_re, lam_im, log_dt.reshape(g, 1), b_re.transpose(0, 2, 1), b_im.transpose(0, 2, 1))
    return outs


def _s5_kernel(u_ref, bd_ref, cd_ref, are_ref, aim_ref, d_ref, wg_ref, bg_ref, h0r_ref, h0i_ref,
               o_ref, hr_ref, hi_ref, bu_ref, hs_ref, *, steps, nb):
    ns = S5_WIDTH_STATE

    @pl.when(pl.program_id(0) == 0)
    def _():
        hr_ref[...] = h0r_ref[...]
        hi_ref[...] = h0i_ref[...]

    u = u_ref[...].reshape(steps * nb, BRANCH_WIDTH)
    bu_ref[...] = _dot(u, bd_ref[...])
    a_re = jnp.broadcast_to(are_ref[...], (nb, ns))
    a_im = jnp.broadcast_to(aim_ref[...], (nb, ns))

    def step(t, carry):
        h_re, h_im = carry
        rows = pl.ds(pl.multiple_of(t * nb, nb), nb)
        n_re = a_re * h_re - a_im * h_im + bu_ref[rows, 0:ns]
        n_im = a_re * h_im + a_im * h_re + bu_ref[rows, ns:2 * ns]
        hs_ref[rows, 0:ns] = n_re
        hs_ref[rows, ns:2 * ns] = n_im
        return n_re, n_im

    h_re, h_im = lax.fori_loop(0, steps, step, (hr_ref[...], hi_ref[...]))
    hr_ref[...] = h_re
    hi_ref[...] = h_im
    y = _dot(hs_ref[...], cd_ref[...]) + d_ref[...] * u
    z = _gelu(y)
    o = z * _sigmoid(_dot(z, wg_ref[...]) + bg_ref[...])
    o_ref[...] = o.reshape(steps, nb, BRANCH_WIDTH).astype(o_ref.dtype)


def _s5(u_tb, bd, cd, a_re, a_im, d, wg, bg, h0r, h0i, steps):
    t_len, nb, _ = u_tb.shape
    full = lambda a: pl.BlockSpec(a.shape, lambda i: (0,) * a.ndim)
    blk = pl.BlockSpec((steps, nb, BRANCH_WIDTH), lambda i: (i, 0, 0))
    st = pl.BlockSpec((nb, S5_WIDTH_STATE), lambda i: (0, 0))
    return pl.pallas_call(
        functools.partial(_s5_kernel, steps=steps, nb=nb),
        grid=(t_len // steps,),
        in_specs=[blk, full(bd), full(cd), full(a_re), full(a_im), full(d), full(wg), full(bg), st, st],
        out_specs=[blk, st, st],
        out_shape=[jax.ShapeDtypeStruct((t_len, nb, BRANCH_WIDTH), F32),
                   jax.ShapeDtypeStruct((nb, S5_WIDTH_STATE), F32), jax.ShapeDtypeStruct((nb, S5_WIDTH_STATE), F32)],
        scratch_shapes=[pltpu.VMEM((steps * nb, 2 * S5_WIDTH_STATE), F32),
                        pltpu.VMEM((steps * nb, 2 * S5_WIDTH_STATE), F32)],
        compiler_params=_cparams("arbitrary"),
        name="s5",
    )(u_tb, bd, cd, a_re, a_im, d, wg, bg, h0r, h0i)


def _merge_kernel(x_ref, h1_ref, oa_ref, ob_ref, oc_ref, od_ref, g1_ref, sh2_ref, sc2_ref,
                  wg_ref, wb_ref, wo_ref, lg_ref, lb_ref, x1_ref, h2_ref, h2t_ref, *, alpha):
    d = x_ref.shape[1]
    h1 = h1_ref[...]
    merged = None
    for n, br in enumerate((oa_ref, ob_ref, oc_ref, od_ref)):
        gate = _sigmoid(jnp.dot(h1, wg_ref[:, n * d:(n + 1) * d], preferred_element_type=F32))
        up = _dot(br[...], wb_ref[n])
        merged = gate * up if merged is None else merged + gate * up
    mix = _dot(merged, wo_ref[...])
    x1 = _ln(alpha * x_ref[...] + (1.0 + g1_ref[0]) * mix) * lg_ref[...] + lb_ref[...]
    x1_ref[...] = x1
    h2 = _ln(x1) * (1.0 + sc2_ref[0]) + sh2_ref[0]
    h2_ref[...] = h2.astype(h2_ref.dtype)
    h2t_ref[...] = h2.T.astype(h2t_ref.dtype)


def _merge(x, h1, oa, ob, oc, od, g1, sh2, sc2, wg, wb, wo, lg, lb, tm, alpha):
    n, d = x.shape
    nt = n // tm
    mrow = g1.shape[1]
    per_tile = g1.shape[0] == nt and mrow == tm
    tiles_per_seq = nt // g1.shape[0] if not per_tile else 1
    mod_map = (lambda i: (i, 0, 0)) if per_tile else (lambda i: (i // tiles_per_seq, 0, 0))
    mod = pl.BlockSpec((1, mrow, d), mod_map)
    tok = lambda w: pl.BlockSpec((tm, w), lambda i: (i, 0))
    full = lambda a: pl.BlockSpec(a.shape, lambda i: (0,) * a.ndim)
    bw = BRANCH_WIDTH
    return pl.pallas_call(
        functools.partial(_merge_kernel, alpha=alpha),
        grid=(nt,),
        in_specs=[tok(d), tok(d), tok(bw), tok(bw), tok(bw), tok(bw), mod, mod, mod,
                  full(wg), full(wb), full(wo), full(lg), full(lb)],
        out_specs=[tok(d), tok(d), pl.BlockSpec((d, tm), lambda i: (0, i))],
        out_shape=[jax.ShapeDtypeStruct((n, d), F32), jax.ShapeDtypeStruct((n, d), MXU_DTYPE),
                   jax.ShapeDtypeStruct((d, n), MXU_DTYPE)],
        compiler_params=_cparams("parallel"),
        name="merge",
    )(x, h1, oa, ob, oc, od, g1, sh2, sc2, wg, wb, wo, lg, lb)


def _topk_rows(s, k, rid=None, exact=True, want_rank=True):
    if exact and rid is None:
        rid = _iota(s.shape, 0)
    rank = jnp.full(s.shape, float(k), F32) if want_rank else None
    vals = []
    for r in range(k):
        m = jnp.max(s, axis=0, keepdims=True)
        if exact:
            idx = jnp.min(jnp.where(s == m, rid, jnp.iinfo(jnp.int32).max), axis=0, keepdims=True)
            hit = rid == idx
        else:
            hit = s == m
        if want_rank:
            rank = jnp.where(hit, float(r), rank)
        s = jnp.where(hit, -jnp.inf, s)
        vals.append(m)
    taken = rank < float(k) if want_rank else s == -jnp.inf
    return jnp.concatenate(vals, axis=0), rank, taken, jnp.sum(taken.astype(F32), axis=0, keepdims=True)


def _peer_route_kernel(h_ref, wq_ref, sk_ref, cnt_ref, e0_ref, e1_ref, r1_ref, q_ref):
    kk = PEER_TOPK
    half = PEER_DKEY // 2
    q_ref[...] = jnp.dot(h_ref[...], wq_ref[...], preferred_element_type=F32)
    tn = min(PEER_ROUTE_PASS, h_ref.shape[0])
    n_pass = h_ref.shape[0] // tn

    def route(h, toks, exact):
        s, sv, rk = [], [], []
        surplus = jnp.zeros((1, tn), F32)
        for p in range(2):
            c0 = pl.multiple_of(h * PEER_DKEY + p * half, half)
            sp = _dot_nt(sk_ref[p], q_ref[toks, pl.ds(c0, half)])
            v, r, _, n_taken = _topk_rows(sp, kk, exact=exact, want_rank=exact or p == 1)
            surplus = surplus + (n_taken - float(kk))
            s.append(sp)
            sv.append(v)
            rk.append(r)
        grid = sv[0][:, None, :] + sv[1][None, :, :]
        sub = 8
        cand = jnp.concatenate([grid[k0, 0:sub, :] for k0 in range(kk)] + [grid[0, sub:kk, :]], axis=0)
        n_cand = cand.shape[0]
        grp, loc = _iota((n_cand, tn), 0) // sub, _iota((n_cand, tn), 0) % sub
        flat_id = jnp.where(grp < kk, grp * kk + loc, sub + loc)
        _, _, taken, n_taken = _topk_rows(cand, kk, flat_id, exact=exact, want_rank=exact)
        surplus = surplus + (n_taken - float(kk))
        per_grp = jnp.sum(taken.astype(F32).reshape(n_cand // sub, sub, tn), axis=1)
        cnt = per_grp[0:kk, :] + jnp.where(_iota((kk, tn), 0) == 0, per_grp[kk:kk + 1, :], 0.0)
        top = sv[0][0:1, :] + sv[1][0:1, :]
        z = jnp.sum(jnp.where(taken, jnp.exp(cand - top), 0.0), axis=0, keepdims=True)
        cnt_dense = jnp.zeros((PEER_NKEYS, tn), F32)
        for k0 in range(kk):
            is_k0 = rk[0] == float(k0) if exact else s[0] == sv[0][k0:k0 + 1, :]
            cnt_dense = jnp.where(is_k0, cnt[k0:k0 + 1, :], cnt_dense)
        cnt_ref[h, :, toks] = cnt_dense
        e0_ref[h, :, toks] = jnp.exp(s[0] - sv[0][0:1, :])
        e1_ref[h, :, toks] = (jnp.exp(s[1] - sv[1][0:1, :]) / z).astype(e1_ref.dtype)
        r1_ref[h, :, toks] = rk[1].astype(r1_ref.dtype)
        return jnp.max(surplus) > 0.5

    @pl.loop(0, PEER_HEADS // PEER_ROUTE_HEADS_PER_TRIP * n_pass)
    def _(it):
        toks = pl.ds(pl.multiple_of((it % n_pass) * tn, tn), tn)
        hs = [(it // n_pass) * PEER_ROUTE_HEADS_PER_TRIP + j for j in range(PEER_ROUTE_HEADS_PER_TRIP)]
        ties = [route(h, toks, exact=False) for h in hs]
        for h, tie in zip(hs, ties):
            @pl.when(tie)
            def _():
                route(h, toks, exact=True)


def _peer_route(h2, wq, sk, tn):
    n, d = h2.shape
    full = lambda a: pl.BlockSpec(a.shape, lambda i: (0,) * a.ndim)
    out = pl.BlockSpec((PEER_HEADS, PEER_NKEYS, tn), lambda i: (0, 0, i))
    return pl.pallas_call(
        _peer_route_kernel,
        grid=(n // tn,),
        in_specs=[pl.BlockSpec((tn, d), lambda i: (i, 0)), full(wq), full(sk)],
        out_specs=[out] * 4,
        out_shape=[jax.ShapeDtypeStruct((PEER_HEADS, PEER_NKEYS, n), t)
                   for t in (F32, F32, PEER_WEIGHT_DTYPE, PEER_WEIGHT_DTYPE)],
        scratch_shapes=[pltpu.VMEM((tn, PEER_HEADS * PEER_DKEY), F32)],
        compiler_params=_cparams("parallel"),
        name="peer_route",
    )(h2, wq, sk)


PEER_ROUTE_PASS = 256
PEER_ROUTE_HEADS_PER_TRIP = 4
PEER_EXPERT_TILE = 2048
PEER_PRE_ROWS = 512
PEER_STAGE_ROWS = 256
PEER_LANE_CHUNK = 256


def _peer_expert_kernel(ht_ref, x1_ref, g2_ref, cnt_ref, e0_ref, e1_ref, r1_ref, u_ref, vt_ref, lg_ref, lb_ref,
                        o_ref, acc_ref, g_ref, a_ref, *, alpha):
    e = pl.program_id(1)
    te, tn = a_ref.shape
    stage = min(PEER_STAGE_ROWS, te)
    n_stage = te // stage
    lane_chunk = min(PEER_LANE_CHUNK, tn)

    @pl.when(e == 0)
    def _():
        acc_ref[...] = jnp.zeros(acc_ref.shape, F32)

    pre_rows = min(PEER_PRE_ROWS, te)
    for r0 in range(0, te, pre_rows):
        a_ref[r0:r0 + pre_rows, :] = jnp.dot(u_ref[r0:r0 + pre_rows, :], ht_ref[...], preferred_element_type=F32)
    wdt = e1_ref.dtype

    def weighted_activation(s):
        for c in range(stage // PEER_NKEYS):
            i0_local = s * (stage // PEER_NKEYS) + c
            rows = slice(i0_local * PEER_NKEYS, (i0_local + 1) * PEER_NKEYS)
            row = pl.ds(e * (te // PEER_NKEYS) + i0_local, 1)
            cnt_rows = [cnt_ref[h, row, :].astype(wdt) for h in range(PEER_HEADS)]
            e0_rows = [e0_ref[h, row, :].astype(wdt) for h in range(PEER_HEADS)]
            for t0 in range(0, tn, lane_chunk):
                lanes = slice(t0, t0 + lane_chunk)
                w = jnp.zeros((PEER_NKEYS, lane_chunk), wdt)
                for h in range(PEER_HEADS):
                    hit = r1_ref[h, :, lanes] < cnt_rows[h][:, lanes]
                    w = w + jnp.where(hit, e1_ref[h, :, lanes], jnp.zeros((), wdt)) * e0_rows[h][:, lanes]
                g_ref[rows, lanes] = (w * _gelu(a_ref[rows, lanes]).astype(wdt)).astype(g_ref.dtype)

    def project(s):
        rows = slice(s * stage, (s + 1) * stage)
        acc_ref[...] += jnp.dot(vt_ref[:, rows], g_ref[rows, :], preferred_element_type=F32)

    for s in range(n_stage + 1):
        if s < n_stage:
            weighted_activation(s)
        if s >= 1:
            project(s - 1)

    @pl.when(e == pl.num_programs(1) - 1)
    def _():
        ff = acc_ref[...].T
        o_ref[...] = _ln(alpha * x1_ref[...] + (1.0 + g2_ref[0]) * ff) * lg_ref[...] + lb_ref[...]


def _peer_expert(h2t, x1, g2, cnt, e0, e1, r1, u, vt, lg, lb, tn, te, alpha, layer):
    d, n = h2t.shape
    nt = n // tn
    n_exp = u.shape[1]
    mrow = g2.shape[1]
    per_tile = g2.shape[0] == nt and mrow == tn
    tiles_per_seq = nt // g2.shape[0] if not per_tile else 1
    mod_map = (lambda i, e: (i, 0, 0)) if per_tile else (lambda i, e: (i // tiles_per_seq, 0, 0))
    tok = pl.BlockSpec((tn, d), lambda i, e: (i, 0))
    route = pl.BlockSpec((PEER_HEADS, PEER_NKEYS, tn), lambda i, e: (0, 0, i))
    vec = pl.BlockSpec((1, d), lambda i, e: (0, 0))
    return pl.pallas_call(
        functools.partial(_peer_expert_kernel, alpha=alpha),
        grid=(nt, n_exp // te),
        in_specs=[pl.BlockSpec((d, tn), lambda i, e: (0, i)), tok, pl.BlockSpec((1, mrow, d), mod_map),
                  route, route, route, route,
                  pl.BlockSpec((None, te, d), lambda i, e: (layer, e, 0)),
                  pl.BlockSpec((None, d, te), lambda i, e: (layer, 0, e)), vec, vec],
        out_specs=tok,
        out_shape=jax.ShapeDtypeStruct((n, d), F32),
        scratch_shapes=[pltpu.VMEM((d, tn), F32), pltpu.VMEM((te, tn), MXU_DTYPE), pltpu.VMEM((te, tn), F32)],
        compiler_params=_cparams("parallel", "arbitrary"),
        name="peer_expert",
    )(h2t, x1, g2, cnt, e0, e1, r1, u, vt, lg, lb)


def _block_diag_in(bb_t):
    g, i, p = bb_t.shape
    return jnp.einsum('gip,gh->gihp', bb_t, jnp.eye(g, dtype=bb_t.dtype)).reshape(g * i, g * p)


def _block_diag_out(c):
    g, i, p = c.shape
    return jnp.einsum('gip,gh->gphi', c, jnp.eye(g, dtype=c.dtype)).reshape(g * p, g * i)


def _tile(n, pref):
    return pref if n % pref == 0 else n


def _layer_group(x, mods, w, n_seq, seq, past, layer):
    n, d = x.shape
    bw = BRANCH_WIDTH
    alpha = w['alpha']
    tm = _tile(n, 512)
    tm2 = _tile(n, 256)

    def mod_arr(m, t):
        if seq % t == 0:
            return m.reshape(n_seq, 1, d)
        return jnp.repeat(m, seq, axis=0).reshape(n // t, t, d)

    sh1, sc1, g1, sh2, sc2, g2 = mods
    (h1, gq, gk, gv, gr, mq, mk, mv, fq, fk, fv, su, la, sm) = _inproj(
        x, mod_arr(sh1, tm), mod_arr(sc1, tm), w['wa'], w['ws'], w['gw2'], w['gb'], w['fb'], tm)
    logf = sm[:, GLA_GATE_RANK:GLA_GATE_RANK + N_HEADS]

    s0_t = past['gla'].transpose(0, 1, 3, 2)
    oa, gla_t = _gla(gq, gk, gv, la, gr, w['gn'], s0_t, n_seq, seq)
    gla_state = gla_t.transpose(0, 1, 3, 2)

    if past['paged']:
        r3 = lambda a: a.reshape(n_seq, seq, bw)
        lfn = logf.reshape(n_seq, seq, N_HEADS).transpose(0, 2, 1)
        ob, oc = _attn_sample(past['page_table'], r3(mq), r3(mk), r3(mv), r3(fq), r3(fk), r3(fv), lfn,
                              past['moba_k'], past['moba_v'], past['fox_k'], past['fox_v'], past['fox_logf_t'], layer)
        ob = ob.reshape(n, bw)
        oc = oc.reshape(n, bw)
    else:
        lf_rows = logf.reshape(n_seq, seq, N_HEADS).transpose(0, 2, 1).reshape(n_seq * N_HEADS, seq)
        cum = _fox_cumsum(lf_rows).reshape(n_seq, N_HEADS, seq)
        ob = _attn_prompt(mq, mk, mv, None, n_seq, seq, True)
        oc = _attn_prompt(fq, fk, fv, cum, n_seq, seq, False)

    steps = math.gcd(seq, 128)
    u_tb = su.reshape(n_seq, seq, bw).transpose(1, 0, 2)
    od_tb, s5_re, s5_im = _s5(u_tb, w['bd'], w['cd'], w['a_re'], w['a_im'], w['s5_d'], w['wglu'], w['bglu'],
                              past['s5_re'].reshape(n_seq, S5_WIDTH_STATE), past['s5_im'].reshape(n_seq, S5_WIDTH_STATE),
                              steps)
    od = od_tb.transpose(1, 0, 2).reshape(n, bw)

    x1, h2, h2t = _merge(x, h1, oa, ob, oc, od, mod_arr(g1, tm), mod_arr(sh2, tm), mod_arr(sc2, tm),
                         w['wg'], w['wb'], w['wo'], w['ln1_g'], w['ln1_b'], tm, alpha)

    cnt, e0, e1, r1 = _peer_route(h2, w['wq'], w['sk'], tm2)
    x2 = _peer_expert(h2t, x1, mod_arr(g2, tm), cnt, e0, e1, r1, w['pu'], w['pvt'], w['ln2_g'], w['ln2_b'],
                      tm, PEER_EXPERT_TILE, alpha, layer)

    hd = lambda a: a.reshape(n_seq, seq, N_HEADS, HEAD_DIM)
    new = {'moba_k': hd(mk), 'moba_v': hd(mv), 'fox_k': hd(fk), 'fox_v': hd(fv),
           'fox_logf': logf.reshape(n_seq, seq, N_HEADS), 'gla': gla_state,
           's5_re': s5_re.reshape(n_seq, S5_GROUPS, S5_STATE), 's5_im': s5_im.reshape(n_seq, S5_GROUPS, S5_STATE)}
    return x2, new


def _prep_weights(l, depth, d, w_in, gla_w_gate2, gla_b_gate, gla_norm, fox_b_f, s5_lambda_re, s5_lambda_im, s5_log_dt,
                  s5_b_re, s5_b_im, s5_c_re, s5_c_im, s5_d, s5_w_glu, s5_b_glu, w_branch, w_out, ln1_g, ln1_b,
                  peer_w_q, peer_sub_keys, peer_u, peer_v, ln2_g, ln2_b):
    bw = BRANCH_WIDTH
    bf = lambda a: a.astype(MXU_DTYPE)
    pad_small = LANES - GLA_GATE_RANK - N_HEADS
    w = {}
    w['wa'], w['ws'], w['wg'] = _repack_w_in(w_in, l)
    w['gw2'] = bf(jnp.concatenate([gla_w_gate2[l], jnp.zeros((LANES - GLA_GATE_RANK, bw), F32)], axis=0))
    w['gb'] = gla_b_gate[l].reshape(1, bw)
    w['gn'] = gla_norm[l].reshape(1, bw)
    w['fb'] = jnp.concatenate([jnp.zeros((GLA_GATE_RANK,), F32), fox_b_f[l], jnp.zeros((pad_small,), F32)]).reshape(1, LANES)
    a_re, a_im, bb_re, bb_im = _s5_params(s5_lambda_re[l], s5_lambda_im[l], s5_log_dt[l], s5_b_re[l], s5_b_im[l])
    w['a_re'] = a_re.reshape(1, S5_WIDTH_STATE)
    w['a_im'] = a_im.reshape(1, S5_WIDTH_STATE)
    w['bd'] = bf(jnp.concatenate([_block_diag_in(bb_re), _block_diag_in(bb_im)], axis=1))
    w['cd'] = bf(jnp.concatenate([_block_diag_out(s5_c_re[l]), -_block_diag_out(s5_c_im[l])], axis=0))
    w['s5_d'] = s5_d[l].reshape(1, bw)
    w['wglu'] = bf(s5_w_glu[l])
    w['bglu'] = s5_b_glu[l].reshape(1, bw)
    w['wb'] = bf(w_branch[l])
    w['wo'] = bf(w_out[l])
    w['ln1_g'] = ln1_g[l].reshape(1, d)
    w['ln1_b'] = ln1_b[l].reshape(1, d)
    w['wq'] = bf(peer_w_q[l])
    w['sk'] = bf(peer_sub_keys[l])
    w['pu'] = peer_u
    w['pvt'] = peer_v
    w['ln2_g'] = ln2_g[l].reshape(1, d)
    w['ln2_b'] = ln2_b[l].reshape(1, d)
    w['alpha'] = (2.0 * depth) ** 0.25
    return w


def kernel(x_prompt, x_sample, c_prompt, c_sample, page_table, cache_moba_k, cache_moba_v, cache_fox_k, cache_fox_v, cache_fox_logf, state_gla, state_s5_re, state_s5_im, w_mod, b_mod, w_in, gla_w_gate2, gla_b_gate, gla_norm, fox_b_f, s5_lambda_re, s5_lambda_im, s5_log_dt, s5_b_re, s5_b_im, s5_c_re, s5_c_im, s5_d, s5_w_glu, s5_b_glu, w_branch, w_out, ln1_g, ln1_b, peer_w_q, peer_sub_keys, peer_u, peer_v, ln2_g, ln2_b):
    bp, seq, d = x_prompt.shape
    bs, dseq, _ = x_sample.shape
    depth = w_in.shape[0]
    n_pool, page = cache_moba_k.shape[1], cache_moba_k.shape[2]
    flat_cache = lambda c: c.transpose(0, 1, 3, 4, 2).reshape(depth, n_pool, BRANCH_WIDTH, page)
    paged = {'paged': True, 'page_table': page_table,
             'moba_k': flat_cache(cache_moba_k), 'moba_v': flat_cache(cache_moba_v),
             'fox_k': flat_cache(cache_fox_k), 'fox_v': flat_cache(cache_fox_v),
             'fox_logf_t': cache_fox_logf.transpose(0, 1, 3, 2)}
    c_all = jnp.concatenate([c_prompt, c_sample], axis=0)
    y_p = x_prompt.reshape(bp * seq, d)
    y_s = x_sample.reshape(bs * dseq, d)
    st_p, st_s = [], []
    pu_all = peer_u.astype(MXU_DTYPE)
    pvt_all = peer_v.astype(MXU_DTYPE).transpose(0, 2, 1)
    for l in range(depth):
        w = _prep_weights(l, depth, d, w_in, gla_w_gate2, gla_b_gate, gla_norm, fox_b_f, s5_lambda_re, s5_lambda_im,
                          s5_log_dt, s5_b_re, s5_b_im, s5_c_re, s5_c_im, s5_d, s5_w_glu, s5_b_glu, w_branch, w_out,
                          ln1_g, ln1_b, peer_w_q, peer_sub_keys, pu_all, pvt_all, ln2_g, ln2_b)
        mod = _modulation(c_all, w_mod, b_mod, l)
        mods = jnp.split(mod, 6, axis=-1)
        past_p = {'paged': False, 'gla': jnp.zeros((bp, N_HEADS, HEAD_DIM, HEAD_DIM), F32),
                  's5_re': jnp.zeros((bp, S5_GROUPS, S5_STATE), F32), 's5_im': jnp.zeros((bp, S5_GROUPS, S5_STATE), F32)}
        past_s = dict(paged, gla=state_gla[l], s5_re=state_s5_re[l], s5_im=state_s5_im[l])
        y_p, new_p = _layer_group(y_p, [m[:bp] for m in mods], w, bp, seq, past_p, l)
        y_s, new_s = _layer_group(y_s, [m[bp:] for m in mods], w, bs, dseq, past_s, l)
        st_p.append(new_p)
        st_s.append(new_s)

    stk = lambda states, name: jnp.stack([s[name] for s in states])
    names = ('moba_k', 'moba_v', 'fox_k', 'fox_v', 'fox_logf', 'gla', 's5_re', 's5_im')
    return ((y_p.reshape(bp, seq, d), y_s.reshape(bs, dseq, d))
            + tuple(stk(st_p, nm) for nm in names) + tuple(stk(st_s, nm) for nm in names))
```

```python
import functools
import math

import numpy as np
import jax
import jax.numpy as jnp
from jax import lax
from jax.experimental import pallas as pl
from jax.experimental.pallas import tpu as pltpu

F32 = jnp.float32
MXU_DTYPE = jnp.bfloat16
PEER_WEIGHT_DTYPE = jnp.bfloat16

HEAD_DIM = 64
N_HEADS = 4
BRANCH_WIDTH = N_HEADS * HEAD_DIM
N_BRANCH = 4
GLA_GATE_RANK = 16
GLA_TAU = 16.0
GLA_CHUNK = 64
GLA_SUBCHUNK = 16
GLA_UNROLL = 8
GLA_GROUP_ROWS = 64
MOBA_BLOCK = 256
MOBA_TOPK = 3
S5_GROUP = 16
S5_GROUPS = BRANCH_WIDTH // S5_GROUP
S5_STATE = 64
S5_WIDTH_STATE = S5_GROUPS * S5_STATE
S5_UNROLL = 8
S5_UNROLL_MAX_STATE = 8 * 1024
PEER_HEADS = 8
PEER_DKEY = 256
PEER_NKEYS = 128
PEER_TOPK = 16
LN_EPS = 1e-5
NEG_BIG = -1e30

V7X_VMEM_LIMIT = 56 * 1024 * 1024
LANES = 128

TOKEN_TILE = 512
ROUTE_TILE = 512
MOD_COL_TILE = 1536
REPACK_ROW_TILE = 256
S5_STEPS = 128
PEER_ROUTE_PASS = 256
PEER_ROUTE_HEADS_PER_TRIP = 4
PEER_EXPERT_TILE = 2048
PEER_PRE_ROWS = 512
PEER_STAGE_ROWS = 256
PEER_LANE_CHUNK = 256


def _cparams(*sem):
    return pltpu.CompilerParams(dimension_semantics=tuple(sem), vmem_limit_bytes=V7X_VMEM_LIMIT)


def _dot(a, b):
    return jnp.dot(a.astype(MXU_DTYPE), b.astype(MXU_DTYPE), preferred_element_type=F32)


def _dot_nt(a, b):
    return lax.dot_general(a.astype(MXU_DTYPE), b.astype(MXU_DTYPE), (((1,), (1,)), ((), ())),
                           preferred_element_type=F32)


def _dot_tn(a, b):
    return lax.dot_general(a.astype(MXU_DTYPE), b.astype(MXU_DTYPE), (((0,), (0,)), ((), ())),
                           preferred_element_type=F32)


def _split3(x):
    hi = x.astype(jnp.bfloat16)
    r1 = x - hi.astype(F32)
    mid = r1.astype(jnp.bfloat16)
    lo = (r1 - mid.astype(F32)).astype(jnp.bfloat16)
    return hi, mid, lo


def _dot_exact01(x, m01):
    hi, mid, lo = _split3(x)
    m = m01.astype(jnp.bfloat16)
    d = lambda a: jnp.dot(a, m, preferred_element_type=F32)
    return d(hi) + d(mid) + d(lo)


def _dot01_exact(m01, x):
    hi, mid, lo = _split3(x)
    m = m01.astype(jnp.bfloat16)
    d = lambda a: jnp.dot(m, a, preferred_element_type=F32)
    return d(hi) + d(mid) + d(lo)


def _ln(x):
    mu = jnp.mean(x, axis=-1, keepdims=True)
    xc = x - mu
    var = jnp.mean(xc * xc, axis=-1, keepdims=True)
    return xc * lax.rsqrt(var + LN_EPS)


def _sigmoid(x):
    return 1.0 / (1.0 + jnp.exp(-x))


def _log_sigmoid(x):
    return jnp.minimum(x, 0.0) - jnp.log1p(jnp.exp(-jnp.abs(x)))


def _gelu(x):
    return 0.5 * x * (1.0 + lax.erf(x * (1.0 / math.sqrt(2.0))))


def _iota(shape, axis):
    return lax.broadcasted_iota(jnp.int32, shape, axis)


def _mod_kernel(c_ref, w_ref, b_ref, o_ref):
    c = c_ref[...]
    o_ref[...] = _dot(c * _sigmoid(c), w_ref[...]) + b_ref[...]


def _modulation(c_all, w_mod, b_mod, layer):
    n, d = c_all.shape
    depth, _, width = w_mod.shape
    tn = _tile(width, MOD_COL_TILE)
    return pl.pallas_call(
        _mod_kernel,
        grid=(width // tn,),
        in_specs=[pl.BlockSpec((n, d), lambda j: (0, 0)),
                  pl.BlockSpec((None, d, tn), lambda j: (layer, 0, j)),
                  pl.BlockSpec((None, 1, tn), lambda j: (layer, 0, j))],
        out_specs=pl.BlockSpec((n, tn), lambda j: (0, j)),
        out_shape=jax.ShapeDtypeStruct((n, width), F32),
        compiler_params=_cparams("parallel"),
        name="modulation",
    )(c_all, w_mod, b_mod.reshape(depth, 1, width))


N_PROJ = 11
PROJ_SEGMENTS = (0, 1, 2, 3, 5, 6, 7, 8, 9, 10, 12)
SEG_GATE_A, SEG_GATE_F, SEG_BRANCH_GATE = 4, 11, 13


def _in_offsets(d):
    bw = BRANCH_WIDTH
    sizes = (bw,) * 4 + (GLA_GATE_RANK,) + (bw,) * 3 + (bw,) * 3 + (N_HEADS, bw, N_BRANCH * d)
    return [int(o) for o in np.concatenate([[0], np.cumsum(sizes)])]


def _repack_kernel(w_ref, wa_ref, ws_ref, wg_ref):
    rows, d_gate = wg_ref.shape
    offs = _in_offsets(d_gate // N_BRANCH)
    bw = BRANCH_WIDTH
    for i, seg in enumerate(PROJ_SEGMENTS):
        wa_ref[:, i * bw:(i + 1) * bw] = w_ref[:, offs[seg]:offs[seg] + bw].astype(wa_ref.dtype)
    small = jnp.concatenate([w_ref[:, offs[SEG_GATE_A]:offs[SEG_GATE_A + 1]],
                             w_ref[:, offs[SEG_GATE_F]:offs[SEG_GATE_F + 1]],
                             jnp.zeros((rows, LANES - GLA_GATE_RANK - N_HEADS), F32)], axis=1)
    ws_ref[...] = small.astype(ws_ref.dtype)
    wg_ref[...] = w_ref[:, offs[SEG_BRANCH_GATE]:offs[SEG_BRANCH_GATE + 1]].astype(wg_ref.dtype)


def _repack_w_in(w_in, layer):
    _, d, width = w_in.shape
    tr = _tile(d, REPACK_ROW_TILE)
    out_w = (N_PROJ * BRANCH_WIDTH, LANES, N_BRANCH * d)
    return pl.pallas_call(
        _repack_kernel,
        grid=(d // tr,),
        in_specs=[pl.BlockSpec((None, tr, width), lambda i: (layer, i, 0))],
        out_specs=[pl.BlockSpec((tr, w), lambda i: (i, 0)) for w in out_w],
        out_shape=[jax.ShapeDtypeStruct((d, w), MXU_DTYPE) for w in out_w],
        compiler_params=_cparams("parallel"),
        name="repack_w_in",
    )(w_in)


def _inproj_kernel(x_ref, sh_ref, sc_ref, wa_ref, ws_ref, gw2_ref, gb_ref, fb_ref,
                   h1_ref, gq_ref, gk_ref, gv_ref, gr_ref, mq_ref, mk_ref, mv_ref, fq_ref, fk_ref, fv_ref,
                   su_ref, la_ref, sm_ref):
    h1 = (_ln(x_ref[...]) * (1.0 + sc_ref[0]) + sh_ref[0]).astype(MXU_DTYPE)
    h1_ref[...] = h1
    outs = (gq_ref, gk_ref, gv_ref, gr_ref, mq_ref, mk_ref, mv_ref, fq_ref, fk_ref, fv_ref, su_ref)
    for i, r in enumerate(outs):
        r[...] = jnp.dot(h1, wa_ref[:, i * BRANCH_WIDTH:(i + 1) * BRANCH_WIDTH],
                         preferred_element_type=F32).astype(r.dtype)
    ps = jnp.dot(h1, ws_ref[...], preferred_element_type=F32)
    la_ref[...] = _log_sigmoid(_dot(ps, gw2_ref[...]) + gb_ref[...]) * (1.0 / GLA_TAU)
    sm_ref[...] = _log_sigmoid(ps + fb_ref[...])


def _inproj(x, sh, sc, wa, ws, gw2, gb, fb, tm):
    n, d = x.shape
    nt = n // tm
    mrow = sh.shape[1]
    per_tile = sh.shape[0] == nt and mrow == tm
    tiles_per_seq = nt // sh.shape[0] if not per_tile else 1
    mod_map = (lambda i: (i, 0, 0)) if per_tile else (lambda i: (i // tiles_per_seq, 0, 0))
    tok = lambda w: pl.BlockSpec((tm, w), lambda i: (i, 0))
    full = lambda a: pl.BlockSpec(a.shape, lambda i: (0,) * a.ndim)
    bw = BRANCH_WIDTH
    odt = [MXU_DTYPE] + [F32] * (N_PROJ + 2)
    owd = [d] + [bw] * N_PROJ + [bw, LANES]
    return pl.pallas_call(
        _inproj_kernel,
        grid=(nt,),
        in_specs=[tok(d), pl.BlockSpec((1, mrow, d), mod_map), pl.BlockSpec((1, mrow, d), mod_map),
                  full(wa), full(ws), full(gw2), full(gb), full(fb)],
        out_specs=[tok(w) for w in owd],
        out_shape=[jax.ShapeDtypeStruct((n, w), t) for w, t in zip(owd, odt)],
        compiler_params=_cparams("parallel"),
        name="inproj",
    )(x, sh, sc, wa, ws, gw2, gb, fb)


def _gla_kernel(q_ref, k_ref, v_ref, la_ref, gr_ref, gn_ref, s0_ref, o_ref, sT_ref, *, seq, chunk, sub, group):
    n_chunks = seq // chunk
    n_sub = chunk // sub
    scale = HEAD_DIM ** -0.5
    tril = (_iota((chunk, chunk), 1) <= _iota((chunk, chunk), 0))
    gn = gn_ref[...]
    heads = range(N_HEADS)
    seqs = range(group)
    sls = [slice(h * HEAD_DIM, (h + 1) * HEAD_DIM) for h in heads]

    def one_chunk(c, state):
        def rows_of(g):
            r0 = g * seq + c * chunk
            return pl.ds(r0 if isinstance(r0, int) else pl.multiple_of(r0, chunk), chunk)

        rows = [rows_of(g) for g in seqs]
        bcum = [_dot01_exact(tril, la_ref[rows[g], :]) for g in seqs]
        prep = []
        for g in seqs:
            q = q_ref[rows[g], :] * scale
            k = k_ref[rows[g], :]
            v = v_ref[rows[g], :]
            b = bcum[g]
            b_end = b[chunk - 1:chunk, :]
            qe = q * jnp.exp(b)
            kd = k * jnp.exp(b_end - b)
            qs, ks = [], []
            for i in range(n_sub):
                base = b[i * sub - 1:i * sub, :] if i > 0 else jnp.zeros_like(b_end)
                nk = (i + 1) * sub
                qs.append(q[i * sub:nk, :] * jnp.exp(b[i * sub:nk, :] - base))
                ks.append(k[:nk, :] * jnp.exp(base - b[:nk, :]))
            prep.append((v, b_end, qe, kd, qs, ks))
        o_inter = [[_dot_nt(prep[g][2][:, sls[h]], state[g][h]) for h in heads] for g in seqs]
        raw = [[[_dot_nt(prep[g][4][i][:, sls[h]], prep[g][5][i][:, sls[h]]) for i in range(n_sub)]
                for h in heads] for g in seqs]
        kv = [[_dot_tn(prep[g][0][:, sls[h]], prep[g][3][:, sls[h]]) for h in heads] for g in seqs]
        atts = []
        for g in seqs:
            per_head = []
            for h in heads:
                att_rows = []
                for i in range(n_sub):
                    nk = (i + 1) * sub
                    causal = _iota((sub, nk), 1) <= _iota((sub, nk), 0) + i * sub
                    a = jnp.where(causal, raw[g][h][i], 0.0)
                    if nk < chunk:
                        a = jnp.concatenate([a, jnp.zeros((sub, chunk - nk), F32)], axis=1)
                    att_rows.append(a)
                per_head.append(att_rows[0] if n_sub == 1 else jnp.concatenate(att_rows, axis=0))
            atts.append(per_head)
        o_intra = [[_dot(atts[g][h], prep[g][0][:, sls[h]]) for h in heads] for g in seqs]
        new_state = []
        for g in seqs:
            b_end = prep[g][1]
            outs = []
            for h in heads:
                o_h = o_inter[g][h] + o_intra[g][h]
                outs.append(o_h * lax.rsqrt(jnp.mean(o_h * o_h, axis=-1, keepdims=True) + LN_EPS))
            gr = gr_ref[rows[g], :]
            o_ref[rows[g], :] = (jnp.concatenate(outs, axis=1) * gn * (gr * _sigmoid(gr))).astype(o_ref.dtype)
            new_state.append(tuple(state[g][h] * jnp.exp(b_end[:, sls[h]]) + kv[g][h] for h in heads))
        return tuple(new_state)

    state0 = tuple(tuple(s0_ref[g, h] for h in heads) for g in seqs)
    if n_chunks == 1:
        state = one_chunk(0, state0)
    else:
        state = lax.fori_loop(0, n_chunks, one_chunk, state0, unroll=math.gcd(n_chunks, GLA_UNROLL))
    for g in seqs:
        for h in heads:
            sT_ref[g, h] = state[g][h]


def _gla(gq, gk, gv, la, gr, gn, s0_t, n_seq, seq):
    chunk = math.gcd(seq, GLA_CHUNK)
    sub = math.gcd(chunk, GLA_SUBCHUNK)
    group = math.gcd(n_seq, max(1, GLA_GROUP_ROWS // seq))
    tok = pl.BlockSpec((group * seq, BRANCH_WIDTH), lambda b: (b, 0))
    st = pl.BlockSpec((group, N_HEADS, HEAD_DIM, HEAD_DIM), lambda b: (b, 0, 0, 0))
    return pl.pallas_call(
        functools.partial(_gla_kernel, seq=seq, chunk=chunk, sub=sub, group=group),
        grid=(n_seq // group,),
        in_specs=[tok, tok, tok, tok, tok, pl.BlockSpec((1, BRANCH_WIDTH), lambda b: (0, 0)), st],
        out_specs=[tok, st],
        out_shape=[jax.ShapeDtypeStruct((n_seq * seq, BRANCH_WIDTH), F32),
                   jax.ShapeDtypeStruct((n_seq, N_HEADS, HEAD_DIM, HEAD_DIM), F32)],
        compiler_params=_cparams("parallel"),
        name="gla",
    )(gq, gk, gv, la, gr, gn, s0_t)


def _cum_kernel(lf_ref, o_ref):
    rows, length = lf_ref.shape
    triu = _iota((LANES, LANES), 0) <= _iota((LANES, LANES), 1)
    carry = jnp.zeros((rows, 1), F32)
    for c in range(length // LANES):
        cs = _dot_exact01(lf_ref[:, c * LANES:(c + 1) * LANES], triu) + carry
        o_ref[:, c * LANES:(c + 1) * LANES] = cs
        carry = cs[:, LANES - 1:LANES]


def _fox_cumsum(lf_rows):
    return pl.pallas_call(
        _cum_kernel,
        out_shape=jax.ShapeDtypeStruct(lf_rows.shape, F32),
        compiler_params=pltpu.CompilerParams(vmem_limit_bytes=V7X_VMEM_LIMIT),
        name="fox_cumsum",
    )(lf_rows)


def _alibi_slope(h):
    return 2.0 ** (-8.0 * (h + 1) / N_HEADS)


def _attn_prompt_kernel(q_ref, k_ref, v_ref, *rest, seq, moba):
    if moba:
        o_ref, kx_ref, vt_ref, km_ref = rest
    else:
        cum_ref, o_ref, kx_ref, vt_ref = rest
    blk = MOBA_BLOCK
    nb = seq // blk
    i = pl.program_id(1)
    nbp = km_ref.shape[1] if moba else 0
    heads = range(N_HEADS)
    cols = [slice(h * HEAD_DIM, (h + 1) * HEAD_DIM) for h in heads]
    n_feat = kx_ref.shape[2] - HEAD_DIM

    @pl.when(i == 0)
    def _():
        pos = _iota((seq, n_feat), 0)
        slot = _iota((seq, n_feat), 1)
        for h in heads:
            kf = k_ref[:, cols[h]]
            if moba:
                km = jnp.sum(kf.reshape(nb, blk, HEAD_DIM), axis=1) * (1.0 / blk)
                km_ref[h] = jnp.concatenate([km, jnp.zeros((nbp - nb, HEAD_DIM), F32)], axis=0) if nbp > nb else km
                feat = jnp.where(slot < nbp, (slot == pos // blk).astype(F32),
                                 jnp.where(slot == nbp, (pos // blk).astype(F32),
                                           jnp.where(slot == nbp + 1, (pos % blk).astype(F32), 0.0)))
            else:
                hi, mid, lo = (p.astype(F32) for p in _split3(cum_ref[0, h:h + 1, :]))
                part = _iota((n_feat, seq), 0)
                feat = jnp.where(part == 0, hi, jnp.where(part == 1, mid, jnp.where(part == 2, lo, 0.0))).T
            kx_ref[h, :, 0:HEAD_DIM] = kf.astype(MXU_DTYPE)
            kx_ref[h, :, HEAD_DIM:HEAD_DIM + n_feat] = feat.astype(MXU_DTYPE)
            vt_ref[h] = v_ref[:, cols[h]].T.astype(MXU_DTYPE)

    q_t = q_ref[...].T
    qx = []
    for h in heads:
        q_h = q_t[h * HEAD_DIM:(h + 1) * HEAD_DIM, :]
        if moba:
            bs = _dot(km_ref[h], q_h)
            jrow = _iota((nbp, blk), 0)
            bs = jnp.where(jrow < i, bs, -jnp.inf)
            rank = jnp.zeros((nbp, blk), F32)
            for j2 in range(nb):
                row = bs[j2:j2 + 1, :]
                beats = (row > bs) | ((row == bs) & (j2 < jrow))
                rank = rank + beats.astype(F32)
            sel = ((jrow < i) & (rank < MOBA_TOPK)) | (jrow == i)
            frow = _iota((n_feat - nbp, blk), 0)
            slope = _alibi_slope(h)
            extra = jnp.concatenate([jnp.where(sel, 0.0, NEG_BIG),
                                     jnp.where(frow == 0, slope * blk, jnp.where(frow == 1, slope, 0.0))], axis=0)
        else:
            extra = jnp.where(_iota((n_feat, blk), 0) < 3, -1.0, 0.0)
        qx.append(jnp.concatenate([q_h * (HEAD_DIM ** -0.5), extra], axis=0).astype(MXU_DTYPE))

    def scores(h, c0):
        return jnp.dot(kx_ref[h, pl.ds(c0, blk), :], qx[h], preferred_element_type=F32)

    def block_update(states, s_all, c0):
        parts = []
        for h in heads:
            m, l, acc = states[h]
            m_new = jnp.maximum(m, jnp.max(s_all[h], axis=0, keepdims=True))
            alpha = jnp.exp(m - m_new)
            p = jnp.exp(s_all[h] - m_new)
            parts.append((m_new, alpha, alpha * l + jnp.sum(p, axis=0, keepdims=True), p.astype(MXU_DTYPE)))
        return tuple((m_new, l_new, alpha * states[h][2] + _dot(vt_ref[h, :, pl.ds(c0, blk)], p))
                     for h, (m_new, alpha, l_new, p) in zip(heads, parts))

    causal_t = _iota((blk, blk), 0) <= _iota((blk, blk), 1)
    init = tuple((jnp.full((1, blk), NEG_BIG, F32), jnp.zeros((1, blk), F32), jnp.zeros((HEAD_DIM, blk), F32))
                 for _ in heads)
    c_own = pl.multiple_of(i * blk, blk)
    states = block_update(init, [jnp.where(causal_t, scores(h, c_own), NEG_BIG) for h in heads], c_own)

    def body(j, states):
        c0 = pl.multiple_of(j * blk, blk)
        return block_update(states, [scores(h, c0) for h in heads], c0)

    states = lax.fori_loop(0, i, body, states)
    o_ref[...] = jnp.concatenate([acc / l for _, l, acc in states], axis=0).T.astype(o_ref.dtype)


def _attn_prompt(q, k, v, cum, n_seq, seq, moba):
    blk = MOBA_BLOCK
    nq = seq // blk
    nbp = -(-nq // 8) * 8
    tile = pl.BlockSpec((blk, BRANCH_WIDTH), lambda b, i: (b * nq + i, 0))
    whole = pl.BlockSpec((seq, BRANCH_WIDTH), lambda b, i: (b, 0))
    in_specs = [tile, whole, whole]
    args = [q, k, v]
    scratch = [pltpu.VMEM((N_HEADS, seq, 2 * HEAD_DIM), MXU_DTYPE), pltpu.VMEM((N_HEADS, HEAD_DIM, seq), MXU_DTYPE)]
    if moba:
        scratch += [pltpu.VMEM((N_HEADS, nbp, HEAD_DIM), F32)]
    else:
        in_specs.append(pl.BlockSpec((1, N_HEADS, seq), lambda b, i: (b, 0, 0)))
        args.append(cum)
    return pl.pallas_call(
        functools.partial(_attn_prompt_kernel, seq=seq, moba=moba),
        grid=(n_seq, nq),
        in_specs=in_specs,
        out_specs=tile,
        out_shape=jax.ShapeDtypeStruct((n_seq * seq, BRANCH_WIDTH), F32),
        scratch_shapes=scratch,
        compiler_params=_cparams("parallel", "arbitrary"),
        name="moba_prompt" if moba else "fox_prompt",
    )(*args)


def _block_diag_queries(q):
    t = q.shape[0]
    col_head = _iota((t, BRANCH_WIDTH), 1) // HEAD_DIM
    return jnp.concatenate([jnp.where(col_head == h, q, jnp.zeros_like(q)) for h in range(N_HEADS)], axis=0)


def _head_diag(acc, t):
    return jnp.concatenate([acc[h * t:(h + 1) * t, h * HEAD_DIM:(h + 1) * HEAD_DIM] for h in range(N_HEADS)], axis=1)


def _attn_sample_kernel(pt_ref, mq_ref, mkn_ref, mvn_ref, fq_ref, fkn_ref, fvn_ref, lfn_ref, *refs,
                        t_new, n_pages, page, past_len):
    del pt_ref
    mk, mv, fk, fv, lfc = (refs[i * n_pages:(i + 1) * n_pages] for i in range(5))
    ob_ref, oc_ref = refs[5 * n_pages:]
    rows = N_HEADS * t_new
    scale = HEAD_DIM ** -0.5
    row_head = _iota((rows, 1), 0) // t_new
    row_t = _iota((rows, 1), 0) % t_new
    lane_n = _iota((rows, t_new), 1)
    causal = lane_n <= row_t
    cast = lambda r: r[...].astype(MXU_DTYPE)

    def softmax(s_past, s_new):
        m = jnp.maximum(jnp.max(s_past, axis=-1, keepdims=True), jnp.max(s_new, axis=-1, keepdims=True))
        p_past = jnp.exp(s_past - m)
        p_new = jnp.exp(s_new - m)
        l = jnp.sum(p_past, axis=-1, keepdims=True) + jnp.sum(p_new, axis=-1, keepdims=True)
        return p_past.astype(MXU_DTYPE), p_new, l

    def weighted_values(p_past, p_new, l, v_pages, v_new):
        acc = _dot(p_new, v_new)
        for pg in range(n_pages):
            acc = acc + _dot_nt(p_past[:, pg * page:(pg + 1) * page], cast(v_pages[pg]))
        return _head_diag(acc / l, t_new)

    qm = _block_diag_queries(mq_ref[...]).astype(MXU_DTYPE)
    qf = _block_diag_queries(fq_ref[...]).astype(MXU_DTYPE)
    pages_per_block = MOBA_BLOCK // page
    n_blocks = n_pages // pages_per_block
    qk_m = jnp.concatenate([jnp.dot(qm, cast(mk[pg]), preferred_element_type=F32) for pg in range(n_pages)], axis=1)
    qk_m_new = _dot_nt(qm, mkn_ref[...])
    qk_f = jnp.concatenate([jnp.dot(qf, cast(fk[pg]), preferred_element_type=F32) for pg in range(n_pages)], axis=1)
    qk_f_new = _dot_nt(qf, fkn_ref[...])
    lf_all = jnp.concatenate([lfc[pg][...] for pg in range(n_pages)], axis=0)
    triu = _iota((page, page), 0) <= _iota((page, page), 1)
    local = _dot_exact01(lf_all, triu)
    kmean_cols = []
    for b in range(n_blocks):
        blk = jnp.concatenate([mk[b * pages_per_block + i][...] for i in range(pages_per_block)], axis=1)
        kmean_cols.append(jnp.sum(blk, axis=1, keepdims=True) * (1.0 / MOBA_BLOCK))
    bs = _dot(qm, jnp.concatenate(kmean_cols, axis=1))
    nr = n_pages * N_HEADS
    ri, ci = _iota((nr, nr), 0), _iota((nr, nr), 1)
    earlier_page = (ci % N_HEADS == ri % N_HEADS) & (ci // N_HEADS < ri // N_HEADS)
    cum = local + _dot01_exact(earlier_page, local[:, page - 1:page])

    jl = _iota((rows, n_blocks), 1)
    rank = jnp.zeros((rows, n_blocks), F32)
    for j2 in range(n_blocks):
        col = bs[:, j2:j2 + 1]
        rank = rank + ((col > bs) | ((col == bs) & (j2 < jl))).astype(F32)
    sel = (rank < MOBA_TOPK).astype(F32)
    sel_keys = jnp.concatenate(
        [jnp.broadcast_to(sel[:, b:b + 1], (rows, MOBA_BLOCK)) for b in range(n_blocks)], axis=1) > 0.5
    slope = jnp.zeros((rows, 1), F32)
    for h in range(N_HEADS):
        slope = jnp.where(row_head == h, _alibi_slope(h), slope)
    dist = (past_len + row_t - _iota((rows, past_len), 1)).astype(F32)
    s_m = jnp.where(sel_keys, qk_m * scale - slope * dist, NEG_BIG)
    s_m_new = jnp.where(causal, qk_m_new * scale - slope * (row_t - lane_n).astype(F32), NEG_BIG)
    soft_m = softmax(s_m, s_m_new)

    bias = jnp.concatenate(
        [jnp.broadcast_to(cum[pg * N_HEADS:(pg + 1) * N_HEADS, None, :], (N_HEADS, t_new, page)).reshape(rows, page)
         for pg in range(n_pages)], axis=1)
    s_f = qk_f * scale - bias
    lfn = lfn_ref[...]
    cols, c = [], cum[nr - N_HEADS:nr, page - 1:page]
    for t in range(t_new):
        c = c + lfn[:, t:t + 1]
        cols.append(c)
    cum_n = jnp.concatenate(cols, axis=1)
    bias_n = jnp.broadcast_to(cum_n[:, None, :], (N_HEADS, t_new, t_new)).reshape(rows, t_new)
    s_f_new = jnp.where(causal, qk_f_new * scale - bias_n, NEG_BIG)
    soft_f = softmax(s_f, s_f_new)

    ob_ref[...] = weighted_values(*soft_m, mv, mvn_ref[...])
    oc_ref[...] = weighted_values(*soft_f, fv, fvn_ref[...])


def _attn_sample(page_table, mq, mkn, mvn, fq, fkn, fvn, lfn, mkc, mvc, fkc, fvc, lfc, layer):
    n_seq, t_new, _ = mq.shape
    n_pages = page_table.shape[1]
    page = mkc.shape[3]
    past_len = n_pages * page
    assert past_len % MOBA_BLOCK == 0 and MOBA_BLOCK % page == 0 and t_new <= MOBA_BLOCK
    new = pl.BlockSpec((None, t_new, BRANCH_WIDTH), lambda b, pt: (b, 0, 0))

    def pages(width):
        return [pl.BlockSpec((None, None, width, page), lambda b, pt, pg=pg: (layer, pt[b, pg], 0, 0))
                for pg in range(n_pages)]

    grid_spec = pltpu.PrefetchScalarGridSpec(
        num_scalar_prefetch=1,
        grid=(n_seq,),
        in_specs=[new, new, new, new, new, new, pl.BlockSpec((None, N_HEADS, t_new), lambda b, pt: (b, 0, 0))]
                 + 4 * pages(BRANCH_WIDTH) + pages(N_HEADS),
        out_specs=[new, new],
    )
    return pl.pallas_call(
        functools.partial(_attn_sample_kernel, t_new=t_new, n_pages=n_pages, page=page, past_len=past_len),
        grid_spec=grid_spec,
        out_shape=[jax.ShapeDtypeStruct((n_seq, t_new, BRANCH_WIDTH), F32)] * 2,
        compiler_params=_cparams("parallel"),
        name="attn_sample",
    )(page_table, mq, mkn, mvn, fq, fkn, fvn, lfn,
      *([mkc] * n_pages + [mvc] * n_pages + [fkc] * n_pages + [fvc] * n_pages + [lfc] * n_pages))


def _s5_param_kernel(lre_ref, lim_ref, ldt_ref, bre_ref, bim_ref, are_ref, aim_ref, bbre_ref, bbim_ref):
    lam_re = lre_ref[...]
    lam_im = lim_ref[...]
    dt = jnp.exp(ldt_ref[...])
    mag = jnp.exp(lam_re * dt)
    ab_re = mag * jnp.cos(lam_im * dt)
    ab_im = mag * jnp.sin(lam_im * dt)
    den = lam_re * lam_re + lam_im * lam_im
    z_re = ((ab_re - 1.0) * lam_re + ab_im * lam_im) / den
    z_im = (ab_im * lam_re - (ab_re - 1.0) * lam_im) / den
    are_ref[...] = ab_re
    aim_ref[...] = ab_im
    b_re = bre_ref[...]
    b_im = bim_ref[...]
    bbre_ref[...] = z_re[:, None, :] * b_re - z_im[:, None, :] * b_im
    bbim_ref[...] = z_re[:, None, :] * b_im + z_im[:, None, :] * b_re


def _s5_params(lam_re, lam_im, log_dt, b_re, b_im):
    g, p = lam_re.shape
    i = b_re.shape[-1]
    outs = pl.pallas_call(
        _s5_param_kernel,
        out_shape=[jax.ShapeDtypeStruct((g, p), F32)] * 2 + [jax.ShapeDtypeStruct((g, i, p), F32)] * 2,
        name="s5_params",
    )(lam_re, lam_im, log_dt.reshape(g, 1), b_re.transpose(0, 2, 1), b_im.transpose(0, 2, 1))
    return outs


def _s5_kernel(u_ref, bd_ref, cd_ref, are_ref, aim_ref, d_ref, wg_ref, bg_ref, h0r_ref, h0i_ref,
               o_ref, hr_ref, hi_ref, bu_ref, hs_ref, *, steps, nb):
    ns = S5_WIDTH_STATE

    @pl.when(pl.program_id(0) == 0)
    def _():
        hr_ref[...] = h0r_ref[...]
        hi_ref[...] = h0i_ref[...]

    u = u_ref[...].reshape(steps * nb, BRANCH_WIDTH)
    bu_ref[...] = _dot(u, bd_ref[...])
    a_re = jnp.broadcast_to(are_ref[...], (nb, ns))
    a_im = jnp.broadcast_to(aim_ref[...], (nb, ns))

    def step(t, carry):
        h_re, h_im = carry
        rows = pl.ds(pl.multiple_of(t * nb, nb), nb)
        n_re = a_re * h_re - a_im * h_im + bu_ref[rows, 0:ns]
        n_im = a_re * h_im + a_im * h_re + bu_ref[rows, ns:2 * ns]
        hs_ref[rows, 0:ns] = n_re
        hs_ref[rows, ns:2 * ns] = n_im
        return n_re, n_im

    unroll = math.gcd(steps, S5_UNROLL) if nb * S5_WIDTH_STATE <= S5_UNROLL_MAX_STATE else 1
    h_re, h_im = lax.fori_loop(0, steps, step, (hr_ref[...], hi_ref[...]), unroll=unroll)
    hr_ref[...] = h_re
    hi_ref[...] = h_im
    y = _dot(hs_ref[...], cd_ref[...]) + d_ref[...] * u
    z = _gelu(y)
    o = z * _sigmoid(_dot(z, wg_ref[...]) + bg_ref[...])
    o_ref[...] = o.reshape(steps, nb, BRANCH_WIDTH).astype(o_ref.dtype)


def _s5(u_tb, bd, cd, a_re, a_im, d, wg, bg, h0r, h0i, steps):
    t_len, nb, _ = u_tb.shape
    full = lambda a: pl.BlockSpec(a.shape, lambda i: (0,) * a.ndim)
    blk = pl.BlockSpec((steps, nb, BRANCH_WIDTH), lambda i: (i, 0, 0))
    st = pl.BlockSpec((nb, S5_WIDTH_STATE), lambda i: (0, 0))
    return pl.pallas_call(
        functools.partial(_s5_kernel, steps=steps, nb=nb),
        grid=(t_len // steps,),
        in_specs=[blk, full(bd), full(cd), full(a_re), full(a_im), full(d), full(wg), full(bg), st, st],
        out_specs=[blk, st, st],
        out_shape=[jax.ShapeDtypeStruct((t_len, nb, BRANCH_WIDTH), F32),
                   jax.ShapeDtypeStruct((nb, S5_WIDTH_STATE), F32), jax.ShapeDtypeStruct((nb, S5_WIDTH_STATE), F32)],
        scratch_shapes=[pltpu.VMEM((steps * nb, 2 * S5_WIDTH_STATE), F32),
                        pltpu.VMEM((steps * nb, 2 * S5_WIDTH_STATE), F32)],
        compiler_params=_cparams("arbitrary"),
        name="s5",
    )(u_tb, bd, cd, a_re, a_im, d, wg, bg, h0r, h0i)


def _merge_kernel(x_ref, h1_ref, oa_ref, ob_ref, oc_ref, od_ref, g1_ref, sh2_ref, sc2_ref,
                  wg_ref, wb_ref, wo_ref, lg_ref, lb_ref, x1_ref, h2_ref, h2t_ref, *, alpha):
    d = x_ref.shape[1]
    h1 = h1_ref[...]
    merged = None
    for n, br in enumerate((oa_ref, ob_ref, oc_ref, od_ref)):
        gate = _sigmoid(jnp.dot(h1, wg_ref[:, n * d:(n + 1) * d], preferred_element_type=F32))
        up = _dot(br[...], wb_ref[n])
        merged = gate * up if merged is None else merged + gate * up
    mix = _dot(merged, wo_ref[...])
    x1 = _ln(alpha * x_ref[...] + (1.0 + g1_ref[0]) * mix) * lg_ref[...] + lb_ref[...]
    x1_ref[...] = x1
    h2 = _ln(x1) * (1.0 + sc2_ref[0]) + sh2_ref[0]
    h2_ref[...] = h2.astype(h2_ref.dtype)
    h2t_ref[...] = h2.T.astype(h2t_ref.dtype)


def _merge(x, h1, oa, ob, oc, od, g1, sh2, sc2, wg, wb, wo, lg, lb, tm, alpha):
    n, d = x.shape
    nt = n // tm
    mrow = g1.shape[1]
    per_tile = g1.shape[0] == nt and mrow == tm
    tiles_per_seq = nt // g1.shape[0] if not per_tile else 1
    mod_map = (lambda i: (i, 0, 0)) if per_tile else (lambda i: (i // tiles_per_seq, 0, 0))
    mod = pl.BlockSpec((1, mrow, d), mod_map)
    tok = lambda w: pl.BlockSpec((tm, w), lambda i: (i, 0))
    full = lambda a: pl.BlockSpec(a.shape, lambda i: (0,) * a.ndim)
    bw = BRANCH_WIDTH
    return pl.pallas_call(
        functools.partial(_merge_kernel, alpha=alpha),
        grid=(nt,),
        in_specs=[tok(d), tok(d), tok(bw), tok(bw), tok(bw), tok(bw), mod, mod, mod,
                  full(wg), full(wb), full(wo), full(lg), full(lb)],
        out_specs=[tok(d), tok(d), pl.BlockSpec((d, tm), lambda i: (0, i))],
        out_shape=[jax.ShapeDtypeStruct((n, d), F32), jax.ShapeDtypeStruct((n, d), MXU_DTYPE),
                   jax.ShapeDtypeStruct((d, n), MXU_DTYPE)],
        compiler_params=_cparams("parallel"),
        name="merge",
    )(x, h1, oa, ob, oc, od, g1, sh2, sc2, wg, wb, wo, lg, lb)


def _topk_rows(s, k, rid=None, exact=True, want_rank=True):
    if exact and rid is None:
        rid = _iota(s.shape, 0)
    rank = jnp.full(s.shape, float(k), F32) if want_rank else None
    vals = []
    for r in range(k):
        m = jnp.max(s, axis=0, keepdims=True)
        if exact:
            idx = jnp.min(jnp.where(s == m, rid, jnp.iinfo(jnp.int32).max), axis=0, keepdims=True)
            hit = rid == idx
        else:
            hit = s == m
        if want_rank:
            rank = jnp.where(hit, float(r), rank)
        s = jnp.where(hit, -jnp.inf, s)
        vals.append(m)
    taken = rank < float(k) if want_rank else s == -jnp.inf
    return jnp.concatenate(vals, axis=0), rank, taken, jnp.sum(taken.astype(F32), axis=0, keepdims=True)


def _peer_route_kernel(h_ref, wq_ref, sk_ref, cnt_ref, e0_ref, e1_ref, r1_ref, q_ref):
    kk = PEER_TOPK
    half = PEER_DKEY // 2
    q_ref[...] = jnp.dot(h_ref[...], wq_ref[...], preferred_element_type=F32)
    tn = min(PEER_ROUTE_PASS, h_ref.shape[0])
    n_pass = h_ref.shape[0] // tn

    def route(h, toks, exact):
        s, sv, rk = [], [], []
        surplus = jnp.zeros((1, tn), F32)
        for p in range(2):
            c0 = pl.multiple_of(h * PEER_DKEY + p * half, half)
            sp = _dot_nt(sk_ref[p], q_ref[toks, pl.ds(c0, half)])
            v, r, _, n_taken = _topk_rows(sp, kk, exact=exact, want_rank=exact or p == 1)
            surplus = surplus + (n_taken - float(kk))
            s.append(sp)
            sv.append(v)
            rk.append(r)
        grid = sv[0][:, None, :] + sv[1][None, :, :]
        sub = 8
        cand = jnp.concatenate([grid[k0, 0:sub, :] for k0 in range(kk)] + [grid[0, sub:kk, :]], axis=0)
        n_cand = cand.shape[0]
        grp, loc = _iota((n_cand, tn), 0) // sub, _iota((n_cand, tn), 0) % sub
        flat_id = jnp.where(grp < kk, grp * kk + loc, sub + loc)
        _, _, taken, n_taken = _topk_rows(cand, kk, flat_id, exact=exact, want_rank=exact)
        surplus = surplus + (n_taken - float(kk))
        per_grp = jnp.sum(taken.astype(F32).reshape(n_cand // sub, sub, tn), axis=1)
        cnt = per_grp[0:kk, :] + jnp.where(_iota((kk, tn), 0) == 0, per_grp[kk:kk + 1, :], 0.0)
        top = sv[0][0:1, :] + sv[1][0:1, :]
        z = jnp.sum(jnp.where(taken, jnp.exp(cand - top), 0.0), axis=0, keepdims=True)
        cnt_dense = jnp.zeros((PEER_NKEYS, tn), F32)
        for k0 in range(kk):
            is_k0 = rk[0] == float(k0) if exact else s[0] == sv[0][k0:k0 + 1, :]
            cnt_dense = jnp.where(is_k0, cnt[k0:k0 + 1, :], cnt_dense)
        cnt_ref[h, :, toks] = cnt_dense
        e0_ref[h, :, toks] = jnp.exp(s[0] - sv[0][0:1, :])
        e1_ref[h, :, toks] = (jnp.exp(s[1] - sv[1][0:1, :]) / z).astype(e1_ref.dtype)
        r1_ref[h, :, toks] = rk[1].astype(r1_ref.dtype)
        return jnp.max(surplus) > 0.5

    @pl.loop(0, PEER_HEADS // PEER_ROUTE_HEADS_PER_TRIP * n_pass)
    def _(it):
        toks = pl.ds(pl.multiple_of((it % n_pass) * tn, tn), tn)
        hs = [(it // n_pass) * PEER_ROUTE_HEADS_PER_TRIP + j for j in range(PEER_ROUTE_HEADS_PER_TRIP)]
        ties = [route(h, toks, exact=False) for h in hs]
        for h, tie in zip(hs, ties):
            @pl.when(tie)
            def _():
                route(h, toks, exact=True)


def _peer_route(h2, wq, sk, tn):
    n, d = h2.shape
    full = lambda a: pl.BlockSpec(a.shape, lambda i: (0,) * a.ndim)
    out = pl.BlockSpec((PEER_HEADS, PEER_NKEYS, tn), lambda i: (0, 0, i))
    return pl.pallas_call(
        _peer_route_kernel,
        grid=(n // tn,),
        in_specs=[pl.BlockSpec((tn, d), lambda i: (i, 0)), full(wq), full(sk)],
        out_specs=[out] * 4,
        out_shape=[jax.ShapeDtypeStruct((PEER_HEADS, PEER_NKEYS, n), t)
                   for t in (F32, F32, PEER_WEIGHT_DTYPE, PEER_WEIGHT_DTYPE)],
        scratch_shapes=[pltpu.VMEM((tn, PEER_HEADS * PEER_DKEY), F32)],
        compiler_params=_cparams("parallel"),
        name="peer_route",
    )(h2, wq, sk)


def _peer_expert_kernel(ht_ref, x1_ref, g2_ref, cnt_ref, e0_ref, e1_ref, r1_ref, u_ref, vt_ref, lg_ref, lb_ref,
                        o_ref, acc_ref, g_ref, a_ref, *, alpha):
    e = pl.program_id(1)
    te, tn = a_ref.shape
    stage = min(PEER_STAGE_ROWS, te)
    n_stage = te // stage
    lane_chunk = min(PEER_LANE_CHUNK, tn)

    @pl.when(e == 0)
    def _():
        acc_ref[...] = jnp.zeros(acc_ref.shape, F32)

    pre_rows = min(PEER_PRE_ROWS, te)
    for r0 in range(0, te, pre_rows):
        a_ref[r0:r0 + pre_rows, :] = jnp.dot(u_ref[r0:r0 + pre_rows, :], ht_ref[...], preferred_element_type=F32)
    wdt = e1_ref.dtype

    def weighted_activation(s):
        for c in range(stage // PEER_NKEYS):
            i0_local = s * (stage // PEER_NKEYS) + c
            rows = slice(i0_local * PEER_NKEYS, (i0_local + 1) * PEER_NKEYS)
            row = pl.ds(e * (te // PEER_NKEYS) + i0_local, 1)
            cnt_rows = [cnt_ref[h, row, :].astype(wdt) for h in range(PEER_HEADS)]
            e0_rows = [e0_ref[h, row, :].astype(wdt) for h in range(PEER_HEADS)]
            for t0 in range(0, tn, lane_chunk):
                lanes = slice(t0, t0 + lane_chunk)
                w = jnp.zeros((PEER_NKEYS, lane_chunk), wdt)
                for h in range(PEER_HEADS):
                    hit = r1_ref[h, :, lanes] < cnt_rows[h][:, lanes]
                    w = w + jnp.where(hit, e1_ref[h, :, lanes], jnp.zeros((), wdt)) * e0_rows[h][:, lanes]
                g_ref[rows, lanes] = (w * _gelu(a_ref[rows, lanes]).astype(wdt)).astype(g_ref.dtype)

    def project(s):
        rows = slice(s * stage, (s + 1) * stage)
        acc_ref[...] += jnp.dot(vt_ref[:, rows], g_ref[rows, :], preferred_element_type=F32)

    for s in range(n_stage + 1):
        if s < n_stage:
            weighted_activation(s)
        if s >= 1:
            project(s - 1)

    @pl.when(e == pl.num_programs(1) - 1)
    def _():
        ff = acc_ref[...].T
        o_ref[...] = _ln(alpha * x1_ref[...] + (1.0 + g2_ref[0]) * ff) * lg_ref[...] + lb_ref[...]


def _peer_expert(h2t, x1, g2, cnt, e0, e1, r1, u, vt, lg, lb, tn, te, alpha, layer):
    d, n = h2t.shape
    nt = n // tn
    n_exp = u.shape[1]
    mrow = g2.shape[1]
    per_tile = g2.shape[0] == nt and mrow == tn
    tiles_per_seq = nt // g2.shape[0] if not per_tile else 1
    mod_map = (lambda i, e: (i, 0, 0)) if per_tile else (lambda i, e: (i // tiles_per_seq, 0, 0))
    tok = pl.BlockSpec((tn, d), lambda i, e: (i, 0))
    route = pl.BlockSpec((PEER_HEADS, PEER_NKEYS, tn), lambda i, e: (0, 0, i))
    vec = pl.BlockSpec((1, d), lambda i, e: (0, 0))
    return pl.pallas_call(
        functools.partial(_peer_expert_kernel, alpha=alpha),
        grid=(nt, n_exp // te),
        in_specs=[pl.BlockSpec((d, tn), lambda i, e: (0, i)), tok, pl.BlockSpec((1, mrow, d), mod_map),
                  route, route, route, route,
                  pl.BlockSpec((None, te, d), lambda i, e: (layer, e, 0)),
                  pl.BlockSpec((None, d, te), lambda i, e: (layer, 0, e)), vec, vec],
        out_specs=tok,
        out_shape=jax.ShapeDtypeStruct((n, d), F32),
        scratch_shapes=[pltpu.VMEM((d, tn), F32), pltpu.VMEM((te, tn), MXU_DTYPE), pltpu.VMEM((te, tn), F32)],
        compiler_params=_cparams("parallel", "arbitrary"),
        name="peer_expert",
    )(h2t, x1, g2, cnt, e0, e1, r1, u, vt, lg, lb)


def _block_diag_in(bb_t):
    g, i, p = bb_t.shape
    return jnp.einsum('gip,gh->gihp', bb_t, jnp.eye(g, dtype=bb_t.dtype)).reshape(g * i, g * p)


def _block_diag_out(c):
    g, i, p = c.shape
    return jnp.einsum('gip,gh->gphi', c, jnp.eye(g, dtype=c.dtype)).reshape(g * p, g * i)


def _tile(n, pref):
    return pref if n % pref == 0 else n


def _layer_group(x, mods, w, n_seq, seq, past, layer):
    n, d = x.shape
    bw = BRANCH_WIDTH
    alpha = w['alpha']
    tm = _tile(n, TOKEN_TILE)
    tm2 = _tile(n, ROUTE_TILE)

    def mod_arr(m, t):
        if seq % t == 0:
            return m.reshape(n_seq, 1, d)
        return jnp.repeat(m, seq, axis=0).reshape(n // t, t, d)

    sh1, sc1, g1, sh2, sc2, g2 = mods
    (h1, gq, gk, gv, gr, mq, mk, mv, fq, fk, fv, su, la, sm) = _inproj(
        x, mod_arr(sh1, tm), mod_arr(sc1, tm), w['wa'], w['ws'], w['gw2'], w['gb'], w['fb'], tm)
    logf = sm[:, GLA_GATE_RANK:GLA_GATE_RANK + N_HEADS]

    s0_t = past['gla'].transpose(0, 1, 3, 2)
    oa, gla_t = _gla(gq, gk, gv, la, gr, w['gn'], s0_t, n_seq, seq)
    gla_state = gla_t.transpose(0, 1, 3, 2)

    if past['paged']:
        r3 = lambda a: a.reshape(n_seq, seq, bw)
        lfn = logf.reshape(n_seq, seq, N_HEADS).transpose(0, 2, 1)
        ob, oc = _attn_sample(past['page_table'], r3(mq), r3(mk), r3(mv), r3(fq), r3(fk), r3(fv), lfn,
                              past['moba_k'], past['moba_v'], past['fox_k'], past['fox_v'], past['fox_logf_t'], layer)
        ob = ob.reshape(n, bw)
        oc = oc.reshape(n, bw)
    else:
        lf_rows = logf.reshape(n_seq, seq, N_HEADS).transpose(0, 2, 1).reshape(n_seq * N_HEADS, seq)
        cum = _fox_cumsum(lf_rows).reshape(n_seq, N_HEADS, seq)
        ob = _attn_prompt(mq, mk, mv, None, n_seq, seq, True)
        oc = _attn_prompt(fq, fk, fv, cum, n_seq, seq, False)

    steps = math.gcd(seq, S5_STEPS)
    u_tb = su.reshape(n_seq, seq, bw).transpose(1, 0, 2)
    od_tb, s5_re, s5_im = _s5(u_tb, w['bd'], w['cd'], w['a_re'], w['a_im'], w['s5_d'], w['wglu'], w['bglu'],
                              past['s5_re'].reshape(n_seq, S5_WIDTH_STATE), past['s5_im'].reshape(n_seq, S5_WIDTH_STATE),
                              steps)
    od = od_tb.transpose(1, 0, 2).reshape(n, bw)

    x1, h2, h2t = _merge(x, h1, oa, ob, oc, od, mod_arr(g1, tm), mod_arr(sh2, tm), mod_arr(sc2, tm),
                         w['wg'], w['wb'], w['wo'], w['ln1_g'], w['ln1_b'], tm, alpha)

    cnt, e0, e1, r1 = _peer_route(h2, w['wq'], w['sk'], tm2)
    x2 = _peer_expert(h2t, x1, mod_arr(g2, tm), cnt, e0, e1, r1, w['pu'], w['pvt'], w['ln2_g'], w['ln2_b'],
                      tm, PEER_EXPERT_TILE, alpha, layer)

    hd = lambda a: a.reshape(n_seq, seq, N_HEADS, HEAD_DIM)
    new = {'moba_k': hd(mk), 'moba_v': hd(mv), 'fox_k': hd(fk), 'fox_v': hd(fv),
           'fox_logf': logf.reshape(n_seq, seq, N_HEADS), 'gla': gla_state,
           's5_re': s5_re.reshape(n_seq, S5_GROUPS, S5_STATE), 's5_im': s5_im.reshape(n_seq, S5_GROUPS, S5_STATE)}
    return x2, new


def _prep_weights(l, depth, d, w_in, gla_w_gate2, gla_b_gate, gla_norm, fox_b_f, s5_lambda_re, s5_lambda_im, s5_log_dt,
                  s5_b_re, s5_b_im, s5_c_re, s5_c_im, s5_d, s5_w_glu, s5_b_glu, w_branch, w_out, ln1_g, ln1_b,
                  peer_w_q, peer_sub_keys, peer_u, peer_v, ln2_g, ln2_b):
    bw = BRANCH_WIDTH
    bf = lambda a: a.astype(MXU_DTYPE)
    pad_small = LANES - GLA_GATE_RANK - N_HEADS
    w = {}
    w['wa'], w['ws'], w['wg'] = _repack_w_in(w_in, l)
    w['gw2'] = bf(jnp.concatenate([gla_w_gate2[l], jnp.zeros((LANES - GLA_GATE_RANK, bw), F32)], axis=0))
    w['gb'] = gla_b_gate[l].reshape(1, bw)
    w['gn'] = gla_norm[l].reshape(1, bw)
    w['fb'] = jnp.concatenate([jnp.zeros((GLA_GATE_RANK,), F32), fox_b_f[l], jnp.zeros((pad_small,), F32)]).reshape(1, LANES)
    a_re, a_im, bb_re, bb_im = _s5_params(s5_lambda_re[l], s5_lambda_im[l], s5_log_dt[l], s5_b_re[l], s5_b_im[l])
    w['a_re'] = a_re.reshape(1, S5_WIDTH_STATE)
    w['a_im'] = a_im.reshape(1, S5_WIDTH_STATE)
    w['bd'] = bf(jnp.concatenate([_block_diag_in(bb_re), _block_diag_in(bb_im)], axis=1))
    w['cd'] = bf(jnp.concatenate([_block_diag_out(s5_c_re[l]), -_block_diag_out(s5_c_im[l])], axis=0))
    w['s5_d'] = s5_d[l].reshape(1, bw)
    w['wglu'] = bf(s5_w_glu[l])
    w['bglu'] = s5_b_glu[l].reshape(1, bw)
    w['wb'] = bf(w_branch[l])
    w['wo'] = bf(w_out[l])
    w['ln1_g'] = ln1_g[l].reshape(1, d)
    w['ln1_b'] = ln1_b[l].reshape(1, d)
    w['wq'] = bf(peer_w_q[l])
    w['sk'] = bf(peer_sub_keys[l])
    w['pu'] = peer_u
    w['pvt'] = peer_v
    w['ln2_g'] = ln2_g[l].reshape(1, d)
    w['ln2_b'] = ln2_b[l].reshape(1, d)
    w['alpha'] = (2.0 * depth) ** 0.25
    return w


def kernel(x_prompt, x_sample, c_prompt, c_sample, page_table, cache_moba_k, cache_moba_v, cache_fox_k, cache_fox_v, cache_fox_logf, state_gla, state_s5_re, state_s5_im, w_mod, b_mod, w_in, gla_w_gate2, gla_b_gate, gla_norm, fox_b_f, s5_lambda_re, s5_lambda_im, s5_log_dt, s5_b_re, s5_b_im, s5_c_re, s5_c_im, s5_d, s5_w_glu, s5_b_glu, w_branch, w_out, ln1_g, ln1_b, peer_w_q, peer_sub_keys, peer_u, peer_v, ln2_g, ln2_b):
    bp, seq, d = x_prompt.shape
    bs, dseq, _ = x_sample.shape
    depth = w_in.shape[0]
    n_pool, page = cache_moba_k.shape[1], cache_moba_k.shape[2]
    flat_cache = lambda c: c.transpose(0, 1, 3, 4, 2).reshape(depth, n_pool, BRANCH_WIDTH, page)
    paged = {'paged': True, 'page_table': page_table,
             'moba_k': flat_cache(cache_moba_k), 'moba_v': flat_cache(cache_moba_v),
             'fox_k': flat_cache(cache_fox_k), 'fox_v': flat_cache(cache_fox_v),
             'fox_logf_t': cache_fox_logf.transpose(0, 1, 3, 2)}
    c_all = jnp.concatenate([c_prompt, c_sample], axis=0)
    y_p = x_prompt.reshape(bp * seq, d)
    y_s = x_sample.reshape(bs * dseq, d)
    st_p, st_s = [], []
    pu_all = peer_u.astype(MXU_DTYPE)
    pvt_all = peer_v.astype(MXU_DTYPE).transpose(0, 2, 1)
    for l in range(depth):
        w = _prep_weights(l, depth, d, w_in, gla_w_gate2, gla_b_gate, gla_norm, fox_b_f, s5_lambda_re, s5_lambda_im,
                          s5_log_dt, s5_b_re, s5_b_im, s5_c_re, s5_c_im, s5_d, s5_w_glu, s5_b_glu, w_branch, w_out,
                          ln1_g, ln1_b, peer_w_q, peer_sub_keys, pu_all, pvt_all, ln2_g, ln2_b)
        mod = _modulation(c_all, w_mod, b_mod, l)
        mods = jnp.split(mod, 6, axis=-1)
        past_p = {'paged': False, 'gla': jnp.zeros((bp, N_HEADS, HEAD_DIM, HEAD_DIM), F32),
                  's5_re': jnp.zeros((bp, S5_GROUPS, S5_STATE), F32), 's5_im': jnp.zeros((bp, S5_GROUPS, S5_STATE), F32)}
        past_s = dict(paged, gla=state_gla[l], s5_re=state_s5_re[l], s5_im=state_s5_im[l])
        y_p, new_p = _layer_group(y_p, [m[:bp] for m in mods], w, bp, seq, past_p, l)
        y_s, new_s = _layer_group(y_s, [m[bp:] for m in mods], w, bs, dseq, past_s, l)
        st_p.append(new_p)
        st_s.append(new_s)

    stk = lambda states, name: jnp.stack([s[name] for s in states])
    names = ('moba_k', 'moba_v', 'fox_k', 'fox_v', 'fox_logf', 'gla', 's5_re', 's5_im')
    return ((y_p.reshape(bp, seq, d), y_s.reshape(bs, dseq, d))
            + tuple(stk(st_p, nm) for nm in names) + tuple(stk(st_s, nm) for nm in names))
```
